```python
import math
import jax
import jax.numpy as jnp
from jax import lax
import numpy as np


D_MODEL = 1024
BATCH = 16
SEQ = 4096
DEPTH = 4

CTX_LEN = 256
GRID_W = 64
QBLK = 128
ROPE_BASE = 10000.0
NORM_EPS = 1e-6
NEG_INF = -1e30

SWA_HEADS = 8
SWA_KV_HEADS = 2
SWA_HEAD_DIM = 64
SWA_WINDOW = 128

MLA_HEADS = 8
MLA_Q_RANK = 256
MLA_KV_RANK = 128
MLA_NOPE_DIM = 64
MLA_ROPE_DIM = 32
MLA_V_DIM = 64

DIFF_HEADS = 8
DIFF_HEAD_DIM = 64
DIFF_EPS = 1e-5

N_GROUPS = 4
EXPERTS_PER_GROUP = 8
N_EXPERTS = N_GROUPS * EXPERTS_PER_GROUP
TOP_K = 2
EXPERT_FF = 512
MOE_BLK = 128

AB_IN_WIDTHS = (SWA_HEADS * SWA_HEAD_DIM, SWA_KV_HEADS * SWA_HEAD_DIM, SWA_KV_HEADS * SWA_HEAD_DIM, MLA_Q_RANK, MLA_KV_RANK, MLA_ROPE_DIM)
AB_IN_WIDTH = 512 + 128 + 128 + 256 + 128 + 32
AB_OUT_WIDTH = SWA_HEADS * SWA_HEAD_DIM + MLA_HEADS * MLA_V_DIM
DIFF_IN_WIDTHS = (DIFF_HEADS * 2 * DIFF_HEAD_DIM, DIFF_HEADS * 2 * DIFF_HEAD_DIM, DIFF_HEADS * 2 * DIFF_HEAD_DIM)
DIFF_IN_WIDTH = 3 * DIFF_HEADS * 2 * DIFF_HEAD_DIM
DIFF_OUT_WIDTH = DIFF_HEADS * 2 * DIFF_HEAD_DIM

kernel_name = 'hybrid_swa_mla_diff_hmoe_dit'


def _rmsnorm(x, g, eps=NORM_EPS):
    xf = x.astype(jnp.float32)
    y = xf * lax.rsqrt(jnp.mean(xf * xf, axis=-1, keepdims=True) + eps)
    return (y * g.astype(jnp.float32)).astype(x.dtype)


def _modulate(h, shift, scale):
    return h * (1 + scale) + shift


def _split(x, widths):
    idx = [int(i) for i in np.cumsum(widths)[:-1]]
    return jnp.split(x, idx, axis=-1)


def _axial_rope_tables(n_tokens, dim):
    rows = n_tokens // GRID_W
    row = jnp.repeat(jnp.arange(rows, dtype=jnp.float32), GRID_W)
    col = jnp.tile(jnp.arange(GRID_W, dtype=jnp.float32), rows)
    nf = dim // 4
    inv = ROPE_BASE ** (-jnp.arange(nf, dtype=jnp.float32) / nf)
    ang = jnp.concatenate([row[:, None] * inv, col[:, None] * inv], axis=-1)
    return jnp.cos(ang), jnp.sin(ang)


def _rope(x, tables):
    cos, sin = tables
    cos = cos[None, :, None, :]
    sin = sin[None, :, None, :]
    half = x.shape[-1] // 2
    x1 = x[..., :half].astype(jnp.float32)
    x2 = x[..., half:].astype(jnp.float32)
    return jnp.concatenate([x1 * cos - x2 * sin, x2 * cos + x1 * sin], axis=-1).astype(x.dtype)


def _dense_attention(q, k, v, scale):
    b, sq, h, d = q.shape
    dv = v.shape[-1]
    nb = sq // QBLK
    qb = jnp.moveaxis(q.reshape(b, nb, QBLK, h, d), 1, 0)

    def block(qi):
        s = jnp.einsum('bqhd,bkhd->bhqk', qi, k, preferred_element_type=jnp.float32) * scale
        p = jax.nn.softmax(s, axis=-1).astype(v.dtype)
        return jnp.einsum('bhqk,bkhd->bqhd', p, v)

    out = lax.map(block, qb)
    return jnp.moveaxis(out, 0, 1).reshape(b, sq, h, dv)


def _sink_softmax(s, sink_g):
    sk = jnp.broadcast_to(sink_g.astype(jnp.float32)[None, :, :, None, None], s.shape[:-1] + (1,))
    p = jax.nn.softmax(jnp.concatenate([s, sk], axis=-1), axis=-1)
    return p[..., :-1]


def _window_attention_with_sink(q, k, v, k_ctx, v_ctx, sink):
    b, s_len, hq, d = q.shape
    hkv = k.shape[2]
    g = hq // hkv
    dv = v.shape[-1]
    nb = s_len // QBLK
    kb_len = QBLK + 2 * SWA_WINDOW
    scale = d ** -0.5
    pad = ((0, 0), (SWA_WINDOW, SWA_WINDOW), (0, 0), (0, 0))
    kp = jnp.pad(k, pad)
    vp = jnp.pad(v, pad)
    qb = jnp.moveaxis(q.reshape(b, nb, QBLK, hkv, g, d), 1, 0)
    sink_g = sink.reshape(hkv, g)

    def block(args):
        i, qi = args
        start = i * QBLK
        kb = lax.dynamic_slice_in_dim(kp, start, kb_len, axis=1)
        vb = lax.dynamic_slice_in_dim(vp, start, kb_len, axis=1)
        qpos = start + jnp.arange(QBLK)
        kpos = start - SWA_WINDOW + jnp.arange(kb_len)
        mask = (jnp.abs(qpos[:, None] - kpos[None, :]) <= SWA_WINDOW) & ((kpos >= 0) & (kpos < s_len))[None, :]
        s_loc = jnp.einsum('bqhgd,bkhd->bhgqk', qi, kb, preferred_element_type=jnp.float32) * scale
        s_loc = jnp.where(mask, s_loc, NEG_INF)
        s_ctx = jnp.einsum('bqhgd,bkhd->bhgqk', qi, k_ctx, preferred_element_type=jnp.float32) * scale
        p = _sink_softmax(jnp.concatenate([s_loc, s_ctx], axis=-1), sink_g).astype(v.dtype)
        return (jnp.einsum('bhgqk,bkhd->bqhgd', p[..., :kb_len], vb)
                + jnp.einsum('bhgqk,bkhd->bqhgd', p[..., kb_len:], v_ctx))

    out = lax.map(block, (jnp.arange(nb), qb))
    return jnp.moveaxis(out, 0, 1).reshape(b, s_len, hq * dv)


def _context_sink_attention(q, k, v, sink):
    b, c_len, hq, d = q.shape
    hkv = k.shape[2]
    g = hq // hkv
    qg = q.reshape(b, c_len, hkv, g, d)
    s = jnp.einsum('bqhgd,bkhd->bhgqk', qg, k, preferred_element_type=jnp.float32) * (d ** -0.5)
    p = _sink_softmax(s, sink.reshape(hkv, g)).astype(v.dtype)
    return jnp.einsum('bhgqk,bkhd->bqhgd', p, v).reshape(b, c_len, hq * v.shape[-1])


def _swa_mla_mixer(h_lat, h_ctx, w_in, q_norm, w_uq, kv_norm, w_ukv, sink, w_out, rope_a, rope_b, ctx_out):
    def project(h):
        b, n, _ = h.shape
        aq, ak, av, cq, ckv, kr = _split(h @ w_in, AB_IN_WIDTHS)
        aq = aq.reshape(b, n, SWA_HEADS, SWA_HEAD_DIM)
        ak = ak.reshape(b, n, SWA_KV_HEADS, SWA_HEAD_DIM)
        av = av.reshape(b, n, SWA_KV_HEADS, SWA_HEAD_DIM)
        q = (_rmsnorm(cq, q_norm) @ w_uq).reshape(b, n, MLA_HEADS, MLA_NOPE_DIM + MLA_ROPE_DIM)
        kv = (_rmsnorm(ckv, kv_norm) @ w_ukv).reshape(b, n, MLA_HEADS, MLA_NOPE_DIM + MLA_V_DIM)
        return (aq, ak, av, q[..., :MLA_NOPE_DIM], q[..., MLA_NOPE_DIM:],
                kv[..., :MLA_NOPE_DIM], kr[:, :, None, :], kv[..., MLA_NOPE_DIM:])

    aq_l, ak_l, av_l, qn_l, qr_l, kn_l, kr_l, mv_l = project(h_lat)
    aq_c, ak_c, av_c, qn_c, qr_c, kn_c, kr_c, mv_c = project(h_ctx)
    aq_l = _rope(aq_l, rope_a)
    ak_l = _rope(ak_l, rope_a)
    qr_l = _rope(qr_l, rope_b)
    kr_l = _rope(kr_l, rope_b)

    def mla_qk(qn, qr, kn, kr):
        q = jnp.concatenate([qn, qr], axis=-1)
        k = jnp.concatenate([kn, jnp.broadcast_to(kr, kn.shape[:-1] + (MLA_ROPE_DIM,))], axis=-1)
        return q, k

    q_l, k_l = mla_qk(qn_l, qr_l, kn_l, kr_l)
    q_c, k_c = mla_qk(qn_c, qr_c, kn_c, kr_c)
    mla_scale = (MLA_NOPE_DIM + MLA_ROPE_DIM) ** -0.5
    b, s_len = h_lat.shape[0], h_lat.shape[1]
    o_a = _window_attention_with_sink(aq_l, ak_l, av_l, ak_c, av_c, sink)
    o_b = _dense_attention(q_l, jnp.concatenate([k_l, k_c], axis=1), jnp.concatenate([mv_l, mv_c], axis=1), mla_scale)
    out_lat = jnp.concatenate([o_a, o_b.reshape(b, s_len, -1)], axis=-1) @ w_out
    if not ctx_out:
        return out_lat, None
    c_len = h_ctx.shape[1]
    o_a_c = _context_sink_attention(aq_c, ak_c, av_c, sink)
    o_b_c = _dense_attention(q_c, k_c, mv_c, mla_scale).reshape(b, c_len, -1)
    out_ctx = jnp.concatenate([o_a_c, o_b_c], axis=-1) @ w_out
    return out_lat, out_ctx


def _diff_mixer(h_lat, h_ctx, w_in, lq1, lk1, lq2, lk2, subln, w_out, lambda_init, rope, ctx_out):
    def project(h):
        b, n, _ = h.shape
        q, k, v = _split(h @ w_in, DIFF_IN_WIDTHS)
        q = q.reshape(b, n, DIFF_HEADS, 2, DIFF_HEAD_DIM)
        k = k.reshape(b, n, DIFF_HEADS, 2, DIFF_HEAD_DIM)
        v = v.reshape(b, n, DIFF_HEADS, 2 * DIFF_HEAD_DIM)
        return q[..., 0, :], q[..., 1, :], k[..., 0, :], k[..., 1, :], v

    q1_l, q2_l, k1_l, k2_l, v_l = project(h_lat)
    q1_c, q2_c, k1_c, k2_c, v_c = project(h_ctx)
    q1_l = _rope(q1_l, rope)
    q2_l = _rope(q2_l, rope)
    k1_l = _rope(k1_l, rope)
    k2_l = _rope(k2_l, rope)
    lam = (jnp.exp(jnp.sum(lq1.astype(jnp.float32) * lk1.astype(jnp.float32)))
           - jnp.exp(jnp.sum(lq2.astype(jnp.float32) * lk2.astype(jnp.float32))) + lambda_init)
    scale = DIFF_HEAD_DIM ** -0.5

    def combine(a1, a2):
        o = a1 - lam.astype(a1.dtype) * a2
        o = _rmsnorm(o, subln, DIFF_EPS) * (1.0 - lambda_init)
        return o.reshape(o.shape[0], o.shape[1], -1) @ w_out

    k1_all = jnp.concatenate([k1_l, k1_c], axis=1)
    k2_all = jnp.concatenate([k2_l, k2_c], axis=1)
    v_all = jnp.concatenate([v_l, v_c], axis=1)
    out_lat = combine(_dense_attention(q1_l, k1_all, v_all, scale), _dense_attention(q2_l, k2_all, v_all, scale))
    if not ctx_out:
        return out_lat, None
    out_ctx = combine(_dense_attention(q1_c, k1_c, v_c, scale), _dense_attention(q2_c, k2_c, v_c, scale))
    return out_lat, out_ctx


def _hier_moe(h, w_group, b_group, w_expert, b_expert, w1, w3, w2):
    n_tok, d = h.shape
    g_logits = jnp.dot(h, w_group, preferred_element_type=jnp.float32) + b_group.astype(jnp.float32)
    g_prob = jax.nn.softmax(g_logits, axis=-1)
    grp = jnp.argmax(g_logits, axis=-1).astype(jnp.int32)
    p_grp = jnp.take_along_axis(g_prob, grp[:, None], axis=-1)
    e_logits = (jnp.dot(h, w_expert, preferred_element_type=jnp.float32) + b_expert.astype(jnp.float32)).reshape(n_tok, N_GROUPS, EXPERTS_PER_GROUP)
    e_logits = jnp.take_along_axis(e_logits, grp[:, None, None], axis=1)[:, 0]
    top_p, top_i = lax.top_k(jax.nn.softmax(e_logits, axis=-1), TOP_K)
    gate = p_grp * top_p / jnp.sum(top_p, axis=-1, keepdims=True)
    eid = grp[:, None] * EXPERTS_PER_GROUP + top_i.astype(jnp.int32)

    n_assign = n_tok * TOP_K
    flat_e = eid.reshape(n_assign)
    flat_tok = jnp.repeat(jnp.arange(n_tok, dtype=jnp.int32), TOP_K)
    flat_gate = gate.reshape(n_assign)
    order = jnp.argsort(flat_e)
    se = flat_e[order]
    stok = flat_tok[order]
    sgate = flat_gate[order]
    counts = jnp.zeros((N_EXPERTS,), jnp.int32).at[flat_e].add(1)
    padded = (counts + MOE_BLK - 1) // MOE_BLK * MOE_BLK
    pad_end = jnp.cumsum(padded)
    pad_start = pad_end - padded
    start = jnp.cumsum(counts) - counts
    dest = pad_start[se] + (jnp.arange(n_assign, dtype=jnp.int32) - start[se])
    n_blocks = (n_assign + N_EXPERTS * (MOE_BLK - 1) + MOE_BLK - 1) // MOE_BLK
    n_slots = n_blocks * MOE_BLK
    slot_tok = jnp.full((n_slots,), n_tok, jnp.int32).at[dest].set(stok)
    h_pad = jnp.concatenate([h, jnp.zeros((1, d), h.dtype)], axis=0)
    xb = h_pad[slot_tok].reshape(n_blocks, MOE_BLK, d)
    block_e = jnp.minimum(jnp.searchsorted(pad_end, jnp.arange(n_blocks, dtype=jnp.int32) * MOE_BLK, side='right'), N_EXPERTS - 1)

    def expert_block(args):
        xi, e = args
        return (jax.nn.silu(xi @ w1[e]) * (xi @ w3[e])) @ w2[e]

    yb = lax.map(expert_block, (xb, block_e)).reshape(n_slots, d)
    contrib = yb[dest] * sgate[:, None].astype(yb.dtype)
    return jnp.zeros((n_tok, d), h.dtype).at[stok].add(contrib)


def setup_inputs(seed: int = 0) -> dict:
    key = jax.random.key(seed)
    ks = jax.random.split(key, 30)
    d = D_MODEL
    n_even = (DEPTH + 1) // 2
    n_odd = DEPTH // 2

    def nrm(k, shape, scale):
        return jax.random.normal(k, shape, jnp.float32) * scale

    def gain(k, shape):
        return 1.0 + 0.1 * jax.random.normal(k, shape, jnp.float32)

    return {
        'x': nrm(ks[0], (BATCH, SEQ, d), 1.0),
        'c': nrm(ks[1], (BATCH, d), 1.0),
        'ctx': nrm(ks[2], (BATCH, CTX_LEN, d), 1.0),
        'c_ctx': nrm(ks[3], (d,), 1.0),
        'norm_mix': gain(ks[4], (DEPTH, d)),
        'norm_ffn': gain(ks[5], (DEPTH, d)),
        'ada_w': nrm(ks[6], (DEPTH, d, 6 * d), 0.5 * d ** -0.5),
        'ada_b': nrm(ks[7], (DEPTH, 6 * d), 0.02),
        'ab_w_in': nrm(ks[8], (n_even, d, AB_IN_WIDTH), d ** -0.5),
        'mla_q_norm': gain(ks[9], (n_even, MLA_Q_RANK)),
        'mla_w_uq': nrm(ks[10], (n_even, MLA_Q_RANK, MLA_HEADS * (MLA_NOPE_DIM + MLA_ROPE_DIM)), MLA_Q_RANK ** -0.5),
        'mla_kv_norm': gain(ks[11], (n_even, MLA_KV_RANK)),
        'mla_w_ukv': nrm(ks[12], (n_even, MLA_KV_RANK, MLA_HEADS * (MLA_NOPE_DIM + MLA_V_DIM)), MLA_KV_RANK ** -0.5),
        'swa_sink': nrm(ks[13], (n_even, SWA_HEADS), 1.0),
        'ab_w_out': nrm(ks[14], (n_even, AB_OUT_WIDTH, d), AB_OUT_WIDTH ** -0.5),
        'diff_w_in': nrm(ks[15], (n_odd, d, DIFF_IN_WIDTH), d ** -0.5),
        'diff_lambda_q1': nrm(ks[16], (n_odd, DIFF_HEAD_DIM), 0.1),
        'diff_lambda_k1': nrm(ks[17], (n_odd, DIFF_HEAD_DIM), 0.1),
        'diff_lambda_q2': nrm(ks[18], (n_odd, DIFF_HEAD_DIM), 0.1),
        'diff_lambda_k2': nrm(ks[19], (n_odd, DIFF_HEAD_DIM), 0.1),
        'diff_subln': gain(ks[20], (n_odd, 2 * DIFF_HEAD_DIM)),
        'diff_w_out': nrm(ks[21], (n_odd, DIFF_OUT_WIDTH, d), DIFF_OUT_WIDTH ** -0.5),
        'router_group_w': nrm(ks[22], (DEPTH, d, N_GROUPS), d ** -0.5),
        'router_group_b': nrm(ks[23], (DEPTH, N_GROUPS), 0.01),
        'router_expert_w': nrm(ks[24], (DEPTH, d, N_EXPERTS), d ** -0.5),
        'router_expert_b': nrm(ks[25], (DEPTH, N_EXPERTS), 0.01),
        'expert_w1': nrm(ks[26], (DEPTH, N_EXPERTS, d, EXPERT_FF), d ** -0.5),
        'expert_w3': nrm(ks[27], (DEPTH, N_EXPERTS, d, EXPERT_FF), d ** -0.5),
        'expert_w2': nrm(ks[28], (DEPTH, N_EXPERTS, EXPERT_FF, d), EXPERT_FF ** -0.5),
        'final_norm': gain(ks[29], (d,)),
    }


def reference(x, c, ctx, c_ctx, norm_mix, norm_ffn, ada_w, ada_b, ab_w_in, mla_q_norm, mla_w_uq, mla_kv_norm, mla_w_ukv, swa_sink, ab_w_out, diff_w_in, diff_lambda_q1, diff_lambda_k1, diff_lambda_q2, diff_lambda_k2, diff_subln, diff_w_out, router_group_w, router_group_b, router_expert_w, router_expert_b, expert_w1, expert_w3, expert_w2, final_norm):
    b, s_len, d = x.shape
    rope_64 = _axial_rope_tables(s_len, SWA_HEAD_DIM)
    rope_32 = _axial_rope_tables(s_len, MLA_ROPE_DIM)
    sc = jax.nn.silu(c)
    scc = jax.nn.silu(c_ctx)
    x_lat = x
    x_ctx = ctx
    n_lat = b * s_len
    for l in range(DEPTH):
        last = l == DEPTH - 1
        mod_l = [m[:, None, :] for m in jnp.split(sc @ ada_w[l] + ada_b[l], 6, axis=-1)]
        mod_c = jnp.split(scc @ ada_w[l] + ada_b[l], 6, axis=-1)
        h_lat = _modulate(_rmsnorm(x_lat, norm_mix[l]), mod_l[0], mod_l[1])
        h_ctx = _modulate(_rmsnorm(x_ctx, norm_mix[l]), mod_c[0], mod_c[1])
        j = l // 2
        if l % 2 == 0:
            o_lat, o_ctx = _swa_mla_mixer(h_lat, h_ctx, ab_w_in[j], mla_q_norm[j], mla_w_uq[j], mla_kv_norm[j], mla_w_ukv[j], swa_sink[j], ab_w_out[j], rope_64, rope_32, not last)
        else:
            lambda_init = 0.8 - 0.6 * math.exp(-0.3 * l)
            o_lat, o_ctx = _diff_mixer(h_lat, h_ctx, diff_w_in[j], diff_lambda_q1[j], diff_lambda_k1[j], diff_lambda_q2[j], diff_lambda_k2[j], diff_subln[j], diff_w_out[j], lambda_init, rope_64, not last)
        x_lat = x_lat + mod_l[2] * o_lat
        h_lat = _modulate(_rmsnorm(x_lat, norm_ffn[l]), mod_l[3], mod_l[4])
        if last:
            tokens = h_lat.reshape(n_lat, d)
        else:
            x_ctx = x_ctx + mod_c[2] * o_ctx
            h_ctx = _modulate(_rmsnorm(x_ctx, norm_ffn[l]), mod_c[3], mod_c[4])
            tokens = jnp.concatenate([h_lat.reshape(n_lat, d), h_ctx.reshape(-1, d)], axis=0)
        y = _hier_moe(tokens, router_group_w[l], router_group_b[l], router_expert_w[l], router_expert_b[l], expert_w1[l], expert_w3[l], expert_w2[l])
        x_lat = x_lat + mod_l[5] * y[:n_lat].reshape(b, s_len, d)
        if not last:
            x_ctx = x_ctx + mod_c[5] * y[n_lat:].reshape(x_ctx.shape)
    return _rmsnorm(x_lat, final_norm)
```

```python
import functools
import math

import numpy as np
import jax
import jax.numpy as jnp
from jax import lax
from jax.experimental import pallas as pl
from jax.experimental.pallas import tpu as pltpu

F32 = jnp.float32
BF16 = jnp.bfloat16

D_MODEL = 1024
GRID_W = 64
ROPE_BASE = 10000.0
NORM_EPS = 1e-6
DIFF_EPS = 1e-5
NEG = -1e30

SWA_WINDOW = 128
MLA_SCALE = (64 + 32) ** -0.5
HEAD_SCALE = 64 ** -0.5

N_GROUPS = 4
EXPERTS_PER_GROUP = 8
N_EXPERTS = 32
TOP_K = 2
EXPERT_FF = 512

LANES = 128
TM = 256
TK = 256
TMOE = 256
VMEM_LIMIT = 56 * 1024 * 1024


def _cparams(n_axes):
    return pltpu.CompilerParams(dimension_semantics=("arbitrary",) * n_axes,
                                vmem_limit_bytes=VMEM_LIMIT)


def _rms(x, g, eps):
    return x * lax.rsqrt(jnp.mean(x * x, axis=-1, keepdims=True) + eps) * g


def _rope_block(x, c, s):
    return x * c + pltpu.roll(x, 64, 1) * s


_PAIR_PERM = np.concatenate([np.arange(0, 32), np.arange(64, 96), np.arange(32, 64), np.arange(96, 128)])


def _rope_tables(s_len, c_len):
    rows = s_len // GRID_W
    row = jnp.repeat(jnp.arange(rows, dtype=F32), GRID_W)
    col = jnp.tile(jnp.arange(GRID_W, dtype=F32), rows)

    def tab(dim):
        nf = dim // 4
        inv = ROPE_BASE ** (-jnp.arange(nf, dtype=F32) / nf)
        ang = jnp.concatenate([row[:, None] * inv, col[:, None] * inv], axis=-1)
        return jnp.cos(ang), jnp.sin(ang)

    cos64, sin64 = tab(64)
    cos32, sin32 = tab(32)
    c64 = jnp.concatenate([cos64] * 4, axis=-1)
    s64 = jnp.concatenate([-sin64, -sin64, sin64, sin64], axis=-1)
    one = jnp.ones((s_len, 32), F32)
    zero = jnp.zeros((s_len, 32), F32)
    c32 = jnp.concatenate([cos32, cos32, one, cos32, cos32, one], axis=-1)
    s32 = jnp.concatenate([-sin32, -sin32, zero, sin32, sin32, zero], axis=-1)

    def ext(t, fill):
        return jnp.concatenate([t, jnp.full((c_len, LANES), fill, F32)], axis=0)

    return ext(c64, 1.0), ext(s64, 0.0), ext(c32, 1.0), ext(s32, 0.0)


def _ada_kernel(x_ref, w_ref, b_ref, o_ref):
    x = x_ref[...]
    sx = x * jax.nn.sigmoid(x)
    o_ref[0] = jnp.dot(sx.astype(BF16), w_ref[0].astype(BF16), preferred_element_type=F32) + b_ref[0]


def _ada_call(rows, ada_w, ada_b):
    depth, d, n6 = ada_w.shape
    r = rows.shape[0]
    tn = 1536
    return pl.pallas_call(
        _ada_kernel,
        grid=(depth, n6 // tn),
        in_specs=[pl.BlockSpec((r, d), lambda l, j: (0, 0)),
                  pl.BlockSpec((1, d, tn), lambda l, j: (l, 0, j)),
                  pl.BlockSpec((1, 1, tn), lambda l, j: (l, 0, j))],
        out_specs=pl.BlockSpec((1, r, tn), lambda l, j: (l, 0, j)),
        out_shape=jax.ShapeDtypeStruct((depth, r, n6), F32),
        compiler_params=_cparams(2),
        name="ada_mod",
    )(rows, ada_w, ada_b.reshape(depth, 1, n6))


def _prenorm(has_res, x_ref, y_ref, pmod_ref, mod_ref, g_ref, xo_ref):
    x = x_ref[0]
    if has_res:
        x = x + pmod_ref[0, 0][5:6, :] * y_ref[0]
        xo_ref[0] = x
    m = mod_ref[0, 0]
    return _rms(x, g_ref[...], NORM_EPS) * (1.0 + m[1:2, :]) + m[0:1, :]


def _ab_proj_kernel(has_res, *refs):
    if has_res:
        x_ref, y_ref, pmod_ref = refs[:3]
        refs = refs[3:]
    else:
        x_ref, y_ref, pmod_ref = refs[0], None, None
        refs = refs[1:]
    (mod_ref, g_ref, w1_ref, qn_ref, kvn_ref, wuq_ref, wukv_ref,
     c64_ref, s64_ref, c32_ref, s32_ref) = refs[:11]
    outs = refs[11:]
    if has_res:
        xo_ref, outs = outs[0], outs[1:]
    else:
        xo_ref = None
    sq_ref, sk_ref, sv_ref, mq_ref, mk_ref, mv_ref = outs

    h = _prenorm(has_res, x_ref, y_ref, pmod_ref, mod_ref, g_ref, xo_ref)
    p = jnp.dot(h.astype(BF16), w1_ref[...], preferred_element_type=F32)
    c64, s64, c32, s32 = c64_ref[...], s64_ref[...], c32_ref[...], s32_ref[...]
    for j in range(4):
        blk = _rope_block(p[:, j * 128:(j + 1) * 128], c64, s64)
        sq_ref[0, :, j * 128:(j + 1) * 128] = (blk * HEAD_SCALE).astype(BF16)
    for j in range(2):
        blk = _rope_block(p[:, 512 + j * 128:512 + (j + 1) * 128], c64, s64)
        sk_ref[0, :, j * 128:(j + 1) * 128] = blk.astype(BF16)
    sv_ref[0] = p[:, 768:1024].astype(BF16)
    cq = p[:, 1024:1280]
    ckv = p[:, 1280:1408]
    kr = _rope_block(p[:, 1408:1536], c32, s32).astype(BF16)
    qm = jnp.dot(_rms(cq, qn_ref[...], NORM_EPS).astype(BF16), wuq_ref[...], preferred_element_type=F32)
    kv = jnp.dot(_rms(ckv, kvn_ref[...], NORM_EPS).astype(BF16), wukv_ref[...], preferred_element_type=F32)
    for j in range(4):
        mq_ref[0, :, j * 256:j * 256 + 128] = (qm[:, j * 256:j * 256 + 128] * MLA_SCALE).astype(BF16)
        rr = _rope_block(qm[:, j * 256 + 128:(j + 1) * 256], c32, s32)
        mq_ref[0, :, j * 256 + 128:(j + 1) * 256] = (rr * MLA_SCALE).astype(BF16)
        mk_ref[0, :, j * 256:j * 256 + 128] = kv[:, j * 128:(j + 1) * 128].astype(BF16)
        mk_ref[0, :, j * 256 + 128:(j + 1) * 256] = kr
    mv_ref[0] = kv[:, 512:1024].astype(BF16)


def _diff_proj_kernel(has_res, *refs):
    if has_res:
        x_ref, y_ref, pmod_ref = refs[:3]
        refs = refs[3:]
    else:
        x_ref, y_ref, pmod_ref = refs[0], None, None
        refs = refs[1:]
    mod_ref, g_ref, w_ref, c64_ref, s64_ref = refs[:5]
    outs = refs[5:]
    if has_res:
        xo_ref, outs = outs[0], outs[1:]
    else:
        xo_ref = None
    q_ref, k_ref, v_ref = outs
    h = _prenorm(has_res, x_ref, y_ref, pmod_ref, mod_ref, g_ref, xo_ref)
    p = jnp.dot(h.astype(BF16), w_ref[...], preferred_element_type=F32)
    c64, s64 = c64_ref[...], s64_ref[...]
    for j in range(8):
        blk = _rope_block(p[:, j * 128:(j + 1) * 128], c64, s64)
        q_ref[0, :, j * 128:(j + 1) * 128] = (blk * HEAD_SCALE).astype(BF16)
        blk = _rope_block(p[:, 1024 + j * 128:1024 + (j + 1) * 128], c64, s64)
        k_ref[0, :, j * 128:(j + 1) * 128] = blk.astype(BF16)
    v_ref[0] = p[:, 2048:3072].astype(BF16)


def _row_spec(width):
    return pl.BlockSpec((1, TM, width), lambda b, i: (b, i, 0))


def _mod_spec(nlat):
    return pl.BlockSpec((1, 1, 8, D_MODEL), lambda b, i: (b, i // nlat, 0, 0))


def _full_spec(shape):
    nd = len(shape)
    return pl.BlockSpec(shape, lambda b, i: (0,) * nd)


def _tab_spec():
    return pl.BlockSpec((TM, LANES), lambda b, i: (i, 0))


def _proj_call(kind, x, res, mod, g, weights, tables, out_widths):
    bsz, t, d = x.shape
    nt = t // TM
    nlat = nt - 1
    has_res = res is not None
    ins, specs = [x], [_row_spec(d)]
    if has_res:
        y, pmod = res
        ins += [y, pmod]
        specs += [_row_spec(d), _mod_spec(nlat)]
    ins += [mod, g.reshape(1, d)]
    specs += [_mod_spec(nlat), _full_spec((1, d))]
    for w in weights:
        ins.append(w)
        specs.append(_full_spec(w.shape))
    for tb in tables:
        ins.append(tb)
        specs.append(_tab_spec())
    out_shapes, out_specs = [], []
    if has_res:
        out_shapes.append(jax.ShapeDtypeStruct((bsz, t, d), F32))
        out_specs.append(_row_spec(d))
    for w in out_widths:
        out_shapes.append(jax.ShapeDtypeStruct((bsz, t, w), BF16))
        out_specs.append(_row_spec(w))
    body = _ab_proj_kernel if kind == "ab" else _diff_proj_kernel
    outs = pl.pallas_call(
        functools.partial(body, has_res),
        grid=(bsz, nt),
        in_specs=specs,
        out_specs=out_specs,
        out_shape=out_shapes,
        compiler_params=_cparams(2),
        name=kind + "_proj",
    )(*ins)
    if has_res:
        return outs[0], outs[1:]
    return x, outs


def _pair_masks(mode, width):
    lane = lax.broadcasted_iota(jnp.int32, (1, width), 1)
    if mode == "mla":
        in_a = (lane < 64) | ((lane >= 128) & (lane < 144)) | ((lane >= 192) & (lane < 208))
        in_b = ((lane >= 64) & (lane < 128)) | ((lane >= 144) & (lane < 160)) | ((lane >= 208) & (lane < 224))
    else:
        in_a = (lane < 32) | ((lane >= 64) & (lane < 96))
        in_b = ((lane >= 32) & (lane < 64)) | (lane >= 96)
    return in_a, in_b


def _qk(q, k):
    return lax.dot_general(q, k, (((1,), (1,)), ((), ())), preferred_element_type=F32)


def _dense_attn_kernel(mode, lambda_init, nlat, *refs):
    if mode == "diff":
        q_ref, k_ref, v_ref, lq1_ref, lk1_ref, lq2_ref, lk2_ref, sub_ref, o_ref = refs
    else:
        q_ref, k_ref, v_ref, o_ref = refs
    i = pl.program_id(2)
    q = q_ref[0]
    tq, width = q.shape
    in_a, in_b = _pair_masks(mode, width)
    zero = jnp.zeros_like(q)
    qa = jnp.where(in_a, q, zero)
    qb = jnp.where(in_b, q, zero)
    lo = jnp.where(i == nlat, nlat, 0)

    def online(qx, k, v, m, l, acc):
        s = _qk(qx, k)
        mn = jnp.maximum(m, jnp.max(s, axis=1, keepdims=True))
        a = jnp.exp(m - mn)
        p = jnp.exp(s - mn)
        l = a * l + jnp.sum(p, axis=1, keepdims=True)
        acc = a * acc + jnp.dot(p.astype(BF16), v, preferred_element_type=F32)
        return mn, l, acc

    def body(c, carry):
        ma, la, acca, mb, lb, accb = carry
        off = pl.multiple_of(c * TK, TK)
        k = k_ref[0, pl.ds(off, TK), :]
        v = v_ref[0, pl.ds(off, TK), :]
        ma, la, acca = online(qa, k, v, ma, la, acca)
        mb, lb, accb = online(qb, k, v, mb, lb, accb)
        return ma, la, acca, mb, lb, accb

    m0 = jnp.full((tq, 1), NEG, F32)
    l0 = jnp.zeros((tq, 1), F32)
    a0 = jnp.zeros((tq, LANES), F32)
    ma, la, acca, mb, lb, accb = lax.fori_loop(lo, nlat + 1, body, (m0, l0, a0, m0, l0, a0))
    oa = acca * (1.0 / la)
    ob = accb * (1.0 / lb)
    if mode == "diff":
        lam = (jnp.exp(jnp.sum(lq1_ref[...] * lk1_ref[...], axis=1, keepdims=True))
               - jnp.exp(jnp.sum(lq2_ref[...] * lk2_ref[...], axis=1, keepdims=True)) + lambda_init)
        o = oa - lam * ob
        o = _rms(o, sub_ref[...], DIFF_EPS) * (1.0 - lambda_init)
    else:
        lane = lax.broadcasted_iota(jnp.int32, (1, LANES), 1)
        o = jnp.where(lane < 64, oa, ob)
    o_ref[0] = o.astype(BF16)


def _dense_attn_call(mode, q, k, v, extra=(), lambda_init=0.0):
    bsz, t, qtot = q.shape
    width = 256 if mode == "mla" else 128
    npairs = qtot // width
    nt = t // TM
    ins = [q, k, v]
    specs = [pl.BlockSpec((1, TM, width), lambda b, j, i: (b, i, j)),
             pl.BlockSpec((1, t, width), lambda b, j, i: (b, 0, j)),
             pl.BlockSpec((1, t, LANES), lambda b, j, i: (b, 0, j))]
    for e in extra:
        ins.append(e)
        specs.append(pl.BlockSpec(e.shape, lambda b, j, i: (0, 0)))
    return pl.pallas_call(
        functools.partial(_dense_attn_kernel, mode, lambda_init, nt - 1),
        grid=(bsz, npairs, nt),
        in_specs=specs,
        out_specs=pl.BlockSpec((1, TM, LANES), lambda b, j, i: (b, i, j)),
        out_shape=jax.ShapeDtypeStruct((bsz, t, npairs * LANES), BF16),
        compiler_params=_cparams(3),
        name=mode + "_attn",
    )(*ins)


def _swa_kernel(nlat, sink_ref, q_ref, k_ref, v_ref, o_ref):
    j = pl.program_id(1)
    i = pl.program_id(2)
    s_len = nlat * TM
    kb = TM + 2 * SWA_WINDOW
    q = q_ref[0]
    in_a, in_b = _pair_masks("pair", LANES)
    zero = jnp.zeros_like(q)
    start = i * TM
    kstart = pl.multiple_of(jnp.clip(start - SWA_WINDOW, 0, s_len - kb), SWA_WINDOW)
    k_loc = k_ref[0, pl.ds(kstart, kb), :]
    v_loc = v_ref[0, pl.ds(kstart, kb), :]
    k_ctx = k_ref[0, s_len:s_len + TM, :]
    v_ctx = v_ref[0, s_len:s_len + TM, :]
    qpos = start + lax.broadcasted_iota(jnp.int32, (TM, kb), 0)
    kpos = kstart + lax.broadcasted_iota(jnp.int32, (TM, kb), 1)
    mask = (jnp.abs(qpos - kpos) <= SWA_WINDOW) & (i < nlat)

    def head(in_x, sink):
        qx = jnp.where(in_x, q, zero)
        s_loc = jnp.where(mask, _qk(qx, k_loc), NEG)
        s_ctx = _qk(qx, k_ctx)
        m = jnp.maximum(jnp.maximum(jnp.max(s_loc, axis=1, keepdims=True),
                                    jnp.max(s_ctx, axis=1, keepdims=True)), sink)
        p_loc = jnp.exp(s_loc - m)
        p_ctx = jnp.exp(s_ctx - m)
        l = (jnp.sum(p_loc, axis=1, keepdims=True) + jnp.sum(p_ctx, axis=1, keepdims=True)
             + jnp.exp(sink - m))
        o = (jnp.dot(p_loc.astype(BF16), v_loc, preferred_element_type=F32)
             + jnp.dot(p_ctx.astype(BF16), v_ctx, preferred_element_type=F32))
        return o * (1.0 / l)

    oa = head(in_a, sink_ref[2 * j])
    ob = head(in_b, sink_ref[2 * j + 1])
    lane = lax.broadcasted_iota(jnp.int32, (1, LANES), 1)
    o_ref[0] = jnp.where(lane < 64, oa, ob).astype(BF16)


def _swa_call(sink, q, k, v):
    bsz, t, qtot = q.shape
    npairs = qtot // LANES
    nt = t // TM
    grid_spec = pltpu.PrefetchScalarGridSpec(
        num_scalar_prefetch=1,
        grid=(bsz, npairs, nt),
        in_specs=[pl.BlockSpec((1, TM, LANES), lambda b, j, i, s: (b, i, j)),
                  pl.BlockSpec((1, t, LANES), lambda b, j, i, s: (b, 0, j // 2)),
                  pl.BlockSpec((1, t, LANES), lambda b, j, i, s: (b, 0, j // 2))],
        out_specs=pl.BlockSpec((1, TM, LANES), lambda b, j, i, s: (b, i, j)),
    )
    return pl.pallas_call(
        functools.partial(_swa_kernel, nt - 1),
        grid_spec=grid_spec,
        out_shape=jax.ShapeDtypeStruct((bsz, t, qtot), BF16),
        compiler_params=_cparams(3),
        name="swa_attn",
    )(sink, q, k, v)


def _out_proj_kernel(n_o, *refs):
    o_refs = refs[:n_o]
    w_refs = refs[n_o:2 * n_o]
    x_ref, mod_ref, g_ref, wr_ref, br_ref, xo_ref, h_ref, lg_ref = refs[2 * n_o:]
    acc = jnp.dot(o_refs[0][0], w_refs[0][...], preferred_element_type=F32)
    for n in range(1, n_o):
        acc = acc + jnp.dot(o_refs[n][0], w_refs[n][...], preferred_element_type=F32)
    m = mod_ref[0, 0]
    x = x_ref[0] + m[2:3, :] * acc
    xo_ref[0] = x
    h = _rms(x, g_ref[...], NORM_EPS) * (1.0 + m[4:5, :]) + m[3:4, :]
    h_ref[0] = h.astype(BF16)
    lg_ref[0] = jnp.dot(h, wr_ref[...], precision=lax.Precision.HIGHEST,
                        preferred_element_type=F32) + br_ref[...]


def _out_proj_call(os_, ws, x, mod, g, wr, br):
    bsz, t, d = x.shape
    nt = t // TM
    nlat = nt - 1
    n_o = len(os_)
    ins = list(os_) + list(ws) + [x, mod, g.reshape(1, d), wr, br]
    specs = ([_row_spec(o.shape[-1]) for o in os_] + [_full_spec(w.shape) for w in ws]
             + [_row_spec(d), _mod_spec(nlat), _full_spec((1, d)), _full_spec(wr.shape), _full_spec(br.shape)])
    return pl.pallas_call(
        functools.partial(_out_proj_kernel, n_o),
        grid=(bsz, nt),
        in_specs=specs,
        out_specs=[_row_spec(d), _row_spec(d), _row_spec(LANES)],
        out_shape=[jax.ShapeDtypeStruct((bsz, t, d), F32),
                   jax.ShapeDtypeStruct((bsz, t, d), BF16),
                   jax.ShapeDtypeStruct((bsz, t, LANES), F32)],
        compiler_params=_cparams(2),
        name="out_proj",
    )(*ins)


def _moe_kernel(be_ref, nu_ref, x_ref, g_ref, w1_ref, w3_ref, w2_ref, o_ref):
    i = pl.program_id(0)

    @pl.when(i < nu_ref[0])
    def _():
        x = x_ref[...]
        a = jnp.dot(x, w1_ref[0], preferred_element_type=F32)
        b = jnp.dot(x, w3_ref[0], preferred_element_type=F32)
        hmid = (a * jax.nn.sigmoid(a)) * b
        y = jnp.dot(hmid.astype(BF16), w2_ref[0], preferred_element_type=F32)
        o_ref[...] = y * g_ref[...]

    @pl.when(i >= nu_ref[0])
    def _():
        o_ref[...] = jnp.zeros_like(o_ref)


def _moe_call(block_e, n_used, xb, slot_gate, w1, w3, w2):
    n_slots, d = xb.shape
    n_blocks = n_slots // TMOE
    ff = w1.shape[-1]
    grid_spec = pltpu.PrefetchScalarGridSpec(
        num_scalar_prefetch=2,
        grid=(n_blocks,),
        in_specs=[pl.BlockSpec((TMOE, d), lambda i, be, nu: (i, 0)),
                  pl.BlockSpec((TMOE, 1), lambda i, be, nu: (i, 0)),
                  pl.BlockSpec((1, d, ff), lambda i, be, nu: (be[i], 0, 0)),
                  pl.BlockSpec((1, d, ff), lambda i, be, nu: (be[i], 0, 0)),
                  pl.BlockSpec((1, ff, d), lambda i, be, nu: (be[i], 0, 0))],
        out_specs=pl.BlockSpec((TMOE, d), lambda i, be, nu: (i, 0)),
    )
    return pl.pallas_call(
        _moe_kernel,
        grid_spec=grid_spec,
        out_shape=jax.ShapeDtypeStruct((n_slots, d), F32),
        compiler_params=_cparams(1),
        name="moe_experts",
    )(block_e, n_used, xb, slot_gate, w1, w3, w2)


def _route_and_dispatch(logits):
    n_tok = logits.shape[0]
    g_logits = logits[:, :N_GROUPS]
    g_prob = jax.nn.softmax(g_logits, axis=-1)
    grp = jnp.argmax(g_logits, axis=-1).astype(jnp.int32)
    p_grp = jnp.take_along_axis(g_prob, grp[:, None], axis=-1)
    e_logits = logits[:, N_GROUPS:N_GROUPS + N_EXPERTS].reshape(n_tok, N_GROUPS, EXPERTS_PER_GROUP)
    e_logits = jnp.take_along_axis(e_logits, grp[:, None, None], axis=1)[:, 0]
    top_p, top_i = lax.top_k(jax.nn.softmax(e_logits, axis=-1), TOP_K)
    gate = p_grp * top_p / jnp.sum(top_p, axis=-1, keepdims=True)
    eid = grp[:, None] * EXPERTS_PER_GROUP + top_i.astype(jnp.int32)

    onehot = jnp.sum((eid[:, :, None] == jnp.arange(N_EXPERTS, dtype=jnp.int32)).astype(jnp.int32), axis=1)
    cum = jnp.cumsum(onehot, axis=0) - onehot
    counts = jnp.sum(onehot, axis=0)
    rank = jnp.take_along_axis(cum, eid, axis=1)
    padded = (counts + TMOE - 1) // TMOE * TMOE
    pad_end = jnp.cumsum(padded)
    pad_start = pad_end - padded
    dest = pad_start[eid] + rank
    n_assign = n_tok * TOP_K
    n_blocks = (n_assign + N_EXPERTS * (TMOE - 1) + TMOE - 1) // TMOE
    n_slots = n_blocks * TMOE
    flat_dest = dest.reshape(n_assign)
    flat_tok = jnp.repeat(jnp.arange(n_tok, dtype=jnp.int32), TOP_K)
    slot_tok = jnp.zeros((n_slots,), jnp.int32).at[flat_dest].set(flat_tok)
    slot_gate = jnp.zeros((n_slots,), F32).at[flat_dest].set(gate.reshape(n_assign))
    block_e = jnp.minimum(jnp.searchsorted(pad_end, jnp.arange(n_blocks, dtype=jnp.int32) * TMOE, side='right'),
                          N_EXPERTS - 1).astype(jnp.int32)
    n_used = (pad_end[-1:] // TMOE).astype(jnp.int32)
    return slot_tok, slot_gate.reshape(n_slots, 1), block_e, n_used, dest


def _final_kernel(x_ref, y_ref, pmod_ref, g_ref, o_ref):
    x = x_ref[0] + pmod_ref[0, 0][5:6, :] * y_ref[0]
    o_ref[0] = _rms(x, g_ref[...], NORM_EPS)


def _final_call(x, y, pmod, g, s_len):
    bsz, t, d = x.shape
    return pl.pallas_call(
        _final_kernel,
        grid=(bsz, s_len // TM),
        in_specs=[_row_spec(d), _row_spec(d),
                  pl.BlockSpec((1, 1, 8, d), lambda b, i: (b, 0, 0, 0)),
                  _full_spec((1, d))],
        out_specs=_row_spec(d),
        out_shape=jax.ShapeDtypeStruct((bsz, s_len, d), F32),
        compiler_params=_cparams(2),
        name="final_norm",
    )(x, y, pmod, g.reshape(1, d))


def _take_cols(w, idx):
    wz = jnp.concatenate([w, jnp.zeros((w.shape[0], 1), w.dtype)], axis=1)
    return jnp.take(wz, jnp.asarray(idx, dtype=jnp.int32), axis=1).astype(BF16)


def _ab_layouts():
    zc = 1184
    cols = []
    for j in range(4):
        cols.append(j * 128 + _PAIR_PERM)
    for g in range(2):
        base = 512 + g * 64
        cols.append(base + np.concatenate([np.arange(0, 32), np.arange(0, 32), np.arange(32, 64), np.arange(32, 64)]))
    for g in range(2):
        base = 640 + g * 64
        cols.append(base + np.concatenate([np.arange(64), np.arange(64)]))
    cols.append(768 + np.arange(256))
    cols.append(1024 + np.arange(128))
    kr = 1152
    z32 = np.full((32,), zc)
    cols.append(np.concatenate([kr + np.arange(16), kr + np.arange(16), z32,
                                kr + 16 + np.arange(16), kr + 16 + np.arange(16), z32]))
    w1_idx = np.concatenate(cols)

    zq = 768
    uq = []
    z32q = np.full((32,), zq)
    for j in range(4):
        a, b = 2 * j * 96, (2 * j + 1) * 96
        uq.append(np.concatenate([a + np.arange(64), b + np.arange(64),
                                  a + 64 + np.arange(16), b + 64 + np.arange(16), z32q,
                                  a + 80 + np.arange(16), b + 80 + np.arange(16), z32q]))
    uq_idx = np.concatenate(uq)

    kn, mv = [], []
    for h in range(8):
        kn.append(h * 128 + np.arange(64))
        mv.append(h * 128 + 64 + np.arange(64))
    ukv_idx = np.concatenate(kn + mv)
    return w1_idx, uq_idx, ukv_idx


def _diff_layout():
    cols = []
    for part in range(2):
        for h in range(8):
            cols.append(part * 1024 + h * 128 + _PAIR_PERM)
    cols.append(2048 + np.arange(1024))
    return np.concatenate(cols)


def kernel(x, c, ctx, c_ctx, norm_mix, norm_ffn, ada_w, ada_b, ab_w_in, mla_q_norm, mla_w_uq, mla_kv_norm, mla_w_ukv, swa_sink, ab_w_out, diff_w_in, diff_lambda_q1, diff_lambda_k1, diff_lambda_q2, diff_lambda_k2, diff_subln, diff_w_out, router_group_w, router_group_b, router_expert_w, router_expert_b, expert_w1, expert_w3, expert_w2, final_norm):
    bsz, s_len, d = x.shape
    c_len = ctx.shape[1]
    depth = ada_w.shape[0]
    assert d == D_MODEL and c_len == TM and s_len % TM == 0 and s_len >= TM + 2 * SWA_WINDOW
    t = s_len + c_len

    xs = jnp.concatenate([x, ctx], axis=1)
    tables = _rope_tables(s_len, c_len)
    c64, s64, c32, s32 = tables

    n_rows = (bsz + 1 + 7) // 8 * 8
    rows = jnp.concatenate([c, c_ctx[None, :], jnp.zeros((n_rows - bsz - 1, d), F32)], axis=0)
    mod_all = _ada_call(rows, ada_w, ada_b)
    mod_lat = mod_all[:, :bsz].reshape(depth, bsz, 1, 6, d)
    mod_ctx = jnp.broadcast_to(mod_all[:, bsz].reshape(depth, 1, 1, 6, d), (depth, bsz, 1, 6, d))
    mods = jnp.concatenate([mod_lat, mod_ctx], axis=2)
    mods = jnp.concatenate([mods, jnp.zeros((depth, bsz, 2, 2, d), F32)], axis=3)

    w1_idx, uq_idx, ukv_idx = _ab_layouts()
    diff_idx = _diff_layout()

    res = None
    for l in range(depth):
        j = l // 2
        mod = mods[l]
        if l % 2 == 0:
            weights = [_take_cols(ab_w_in[j], w1_idx), mla_q_norm[j].reshape(1, -1), mla_kv_norm[j].reshape(1, -1),
                       _take_cols(mla_w_uq[j], uq_idx), _take_cols(mla_w_ukv[j], ukv_idx)]
            xs, (sq, sk, sv, mq, mk, mv) = _proj_call("ab", xs, res, mod, norm_mix[l], weights,
                                                      [c64, s64, c32, s32], [512, 256, 256, 1024, 1024, 512])
            o_a = _swa_call(swa_sink[j], sq, sk, sv)
            o_b = _dense_attn_call("mla", mq, mk, mv)
            w_out = ab_w_out[j].astype(BF16)
            attn_outs, out_ws = [o_a, o_b], [w_out[:512], w_out[512:]]
        else:
            lambda_init = 0.8 - 0.6 * math.exp(-0.3 * l)
            weights = [_take_cols(diff_w_in[j], diff_idx)]
            xs, (dq, dk, dv) = _proj_call("diff", xs, res, mod, norm_mix[l], weights, [c64, s64], [1024, 1024, 1024])
            extra = [diff_lambda_q1[j].reshape(1, -1), diff_lambda_k1[j].reshape(1, -1),
                     diff_lambda_q2[j].reshape(1, -1), diff_lambda_k2[j].reshape(1, -1),
                     diff_subln[j].reshape(1, -1)]
            o_d = _dense_attn_call("diff", dq, dk, dv, extra=extra, lambda_init=lambda_init)
            attn_outs, out_ws = [o_d], [diff_w_out[j].astype(BF16)]

        wr = jnp.concatenate([router_group_w[l], router_expert_w[l],
                              jnp.zeros((d, LANES - N_GROUPS - N_EXPERTS), F32)], axis=1)
        br = jnp.concatenate([router_group_b[l], router_expert_b[l],
                              jnp.zeros((LANES - N_GROUPS - N_EXPERTS,), F32)]).reshape(1, LANES)
        xs, h2, logits = _out_proj_call(attn_outs, out_ws, xs, mod, norm_ffn[l], wr, br)

        n_tok = bsz * t
        slot_tok, slot_gate, block_e, n_used, dest = _route_and_dispatch(logits.reshape(n_tok, LANES))
        xb = jnp.take(h2.reshape(n_tok, d), slot_tok, axis=0)
        yb = _moe_call(block_e, n_used, xb, slot_gate, expert_w1[l].astype(BF16), expert_w3[l].astype(BF16),
                       expert_w2[l].astype(BF16))
        y = (jnp.take(yb, dest[:, 0], axis=0) + jnp.take(yb, dest[:, 1], axis=0)).reshape(bsz, t, d)
        res = (y, mod)

    return _final_call(xs, res[0], res[1], final_norm, s_len)
```

```python
import functools
import math

import numpy as np
import jax
import jax.numpy as jnp
from jax import lax
from jax.experimental import pallas as pl
from jax.experimental.pallas import tpu as pltpu

F32 = jnp.float32
BF16 = jnp.bfloat16

D_MODEL = 1024
GRID_W = 64
ROPE_BASE = 10000.0
NORM_EPS = 1e-6
DIFF_EPS = 1e-5
NEG = -1e30

SWA_WINDOW = 128
MLA_SCALE = (64 + 32) ** -0.5
HEAD_SCALE = 64 ** -0.5
LOG2E = math.log2(math.e)

N_GROUPS = 4
EXPERTS_PER_GROUP = 8
N_EXPERTS = 32
TOP_K = 2
EXPERT_FF = 512

LANES = 128
TM = 256
TK = 256
TMOE = 256
VMEM_LIMIT = 56 * 1024 * 1024


def _cparams(n_axes):
    return pltpu.CompilerParams(dimension_semantics=("arbitrary",) * n_axes,
                                vmem_limit_bytes=VMEM_LIMIT)


def _rms(x, g, eps):
    return x * lax.rsqrt(jnp.mean(x * x, axis=-1, keepdims=True) + eps) * g


def _rope_block(x, c, s):
    return x * c + pltpu.roll(x, 64, 1) * s


_PAIR_PERM = np.concatenate([np.arange(0, 32), np.arange(64, 96), np.arange(32, 64), np.arange(96, 128)])


def _rope_tables(s_len, c_len):
    rows = s_len // GRID_W
    row = jnp.repeat(jnp.arange(rows, dtype=F32), GRID_W)
    col = jnp.tile(jnp.arange(GRID_W, dtype=F32), rows)

    def tab(dim):
        nf = dim // 4
        inv = ROPE_BASE ** (-jnp.arange(nf, dtype=F32) / nf)
        ang = jnp.concatenate([row[:, None] * inv, col[:, None] * inv], axis=-1)
        return jnp.cos(ang), jnp.sin(ang)

    cos64, sin64 = tab(64)
    cos32, sin32 = tab(32)
    c64 = jnp.concatenate([cos64] * 4, axis=-1)
    s64 = jnp.concatenate([-sin64, -sin64, sin64, sin64], axis=-1)
    one = jnp.ones((s_len, 32), F32)
    zero = jnp.zeros((s_len, 32), F32)
    c32 = jnp.concatenate([cos32, cos32, one, cos32, cos32, one], axis=-1)
    s32 = jnp.concatenate([-sin32, -sin32, zero, sin32, sin32, zero], axis=-1)

    def ext(t, fill):
        return jnp.concatenate([t, jnp.full((c_len, LANES), fill, F32)], axis=0)

    return ext(c64, 1.0), ext(s64, 0.0), ext(c32, 1.0), ext(s32, 0.0)


def _ada_kernel(x_ref, w_ref, b_ref, o_ref):
    x = x_ref[...]
    sx = x * jax.nn.sigmoid(x)
    o_ref[0] = jnp.dot(sx.astype(BF16), w_ref[0].astype(BF16), preferred_element_type=F32) + b_ref[0]


def _ada_call(rows, ada_w, ada_b):
    depth, d, n6 = ada_w.shape
    r = rows.shape[0]
    tn = 1536
    return pl.pallas_call(
        _ada_kernel,
        grid=(depth, n6 // tn),
        in_specs=[pl.BlockSpec((r, d), lambda l, j: (0, 0)),
                  pl.BlockSpec((1, d, tn), lambda l, j: (l, 0, j)),
                  pl.BlockSpec((1, 1, tn), lambda l, j: (l, 0, j))],
        out_specs=pl.BlockSpec((1, r, tn), lambda l, j: (l, 0, j)),
        out_shape=jax.ShapeDtypeStruct((depth, r, n6), F32),
        compiler_params=_cparams(2),
        name="ada_mod",
    )(rows, ada_w, ada_b.reshape(depth, 1, n6))


def _prenorm(has_res, x_ref, y_ref, pmod_ref, mod_ref, g_ref, xo_ref):
    x = x_ref[0]
    if has_res:
        x = x + pmod_ref[0, 0][5:6, :] * y_ref[0]
        xo_ref[0] = x
    m = mod_ref[0, 0]
    return _rms(x, g_ref[...], NORM_EPS) * (1.0 + m[1:2, :]) + m[0:1, :]


def _ab_proj_kernel(has_res, *refs):
    if has_res:
        x_ref, y_ref, pmod_ref = refs[:3]
        refs = refs[3:]
    else:
        x_ref, y_ref, pmod_ref = refs[0], None, None
        refs = refs[1:]
    (mod_ref, g_ref, w1_ref, qn_ref, kvn_ref, wuq_ref, wukv_ref,
     c64_ref, s64_ref, c32_ref, s32_ref) = refs[:11]
    outs = refs[11:]
    if has_res:
        xo_ref, outs = outs[0], outs[1:]
    else:
        xo_ref = None
    sq_ref, sk_ref, sv_ref, mq_ref, mk_ref, mv_ref = outs

    h = _prenorm(has_res, x_ref, y_ref, pmod_ref, mod_ref, g_ref, xo_ref)
    p = jnp.dot(h.astype(BF16), w1_ref[...], preferred_element_type=F32)
    c64, s64, c32, s32 = c64_ref[...], s64_ref[...], c32_ref[...], s32_ref[...]
    for j in range(4):
        blk = _rope_block(p[:, j * 128:(j + 1) * 128], c64, s64)
        sq_ref[0, :, j * 128:(j + 1) * 128] = (blk * HEAD_SCALE).astype(BF16)
    for j in range(2):
        blk = _rope_block(p[:, 512 + j * 128:512 + (j + 1) * 128], c64, s64)
        sk_ref[0, :, j * 128:(j + 1) * 128] = blk.astype(BF16)
    sv_ref[0] = p[:, 768:1024].astype(BF16)
    cq = p[:, 1024:1280]
    ckv = p[:, 1280:1408]
    kr = _rope_block(p[:, 1408:1536], c32, s32).astype(BF16)
    qm = jnp.dot(_rms(cq, qn_ref[...], NORM_EPS).astype(BF16), wuq_ref[...], preferred_element_type=F32)
    kv = jnp.dot(_rms(ckv, kvn_ref[...], NORM_EPS).astype(BF16), wukv_ref[...], preferred_element_type=F32)
    for j in range(4):
        mq_ref[0, :, j * 256:j * 256 + 128] = (qm[:, j * 256:j * 256 + 128] * (MLA_SCALE * LOG2E)).astype(BF16)
        rr = _rope_block(qm[:, j * 256 + 128:(j + 1) * 256], c32, s32)
        mq_ref[0, :, j * 256 + 128:(j + 1) * 256] = (rr * (MLA_SCALE * LOG2E)).astype(BF16)
        mk_ref[0, :, j * 256:j * 256 + 128] = kv[:, j * 128:(j + 1) * 128].astype(BF16)
        mk_ref[0, :, j * 256 + 128:(j + 1) * 256] = kr
    mv_ref[0, 0] = kv[:, 512:1024].T.astype(BF16)


def _diff_proj_kernel(has_res, *refs):
    if has_res:
        x_ref, y_ref, pmod_ref = refs[:3]
        refs = refs[3:]
    else:
        x_ref, y_ref, pmod_ref = refs[0], None, None
        refs = refs[1:]
    mod_ref, g_ref, w_ref, c64_ref, s64_ref = refs[:5]
    outs = refs[5:]
    if has_res:
        xo_ref, outs = outs[0], outs[1:]
    else:
        xo_ref = None
    q_ref, k_ref, v_ref = outs
    h = _prenorm(has_res, x_ref, y_ref, pmod_ref, mod_ref, g_ref, xo_ref)
    p = jnp.dot(h.astype(BF16), w_ref[...], preferred_element_type=F32)
    c64, s64 = c64_ref[...], s64_ref[...]
    for j in range(8):
        blk = _rope_block(p[:, j * 128:(j + 1) * 128], c64, s64)
        q_ref[0, :, j * 128:(j + 1) * 128] = (blk * (HEAD_SCALE * LOG2E)).astype(BF16)
        blk = _rope_block(p[:, 1024 + j * 128:1024 + (j + 1) * 128], c64, s64)
        k_ref[0, :, j * 128:(j + 1) * 128] = blk.astype(BF16)
    v_ref[0, 0] = p[:, 2048:3072].T.astype(BF16)


def _row_spec(width):
    return pl.BlockSpec((1, TM, width), lambda b, i: (b, i, 0))


def _mod_spec(nlat):
    return pl.BlockSpec((1, 1, 8, D_MODEL), lambda b, i: (b, i // nlat, 0, 0))


def _full_spec(shape):
    nd = len(shape)
    return pl.BlockSpec(shape, lambda b, i: (0,) * nd)


def _tab_spec():
    return pl.BlockSpec((TM, LANES), lambda b, i: (i, 0))


def _proj_call(kind, x, res, mod, g, weights, tables, out_widths):
    bsz, t, d = x.shape
    nt = t // TM
    nlat = nt - 1
    has_res = res is not None
    ins, specs = [x], [_row_spec(d)]
    if has_res:
        y, pmod = res
        ins += [y, pmod]
        specs += [_row_spec(d), _mod_spec(nlat)]
    ins += [mod, g.reshape(1, d)]
    specs += [_mod_spec(nlat), _full_spec((1, d))]
    for w in weights:
        ins.append(w)
        specs.append(_full_spec(w.shape))
    for tb in tables:
        ins.append(tb)
        specs.append(_tab_spec())
    out_shapes, out_specs = [], []
    if has_res:
        out_shapes.append(jax.ShapeDtypeStruct((bsz, t, d), F32))
        out_specs.append(_row_spec(d))
    for w in out_widths:
        if w < 0:
            out_shapes.append(jax.ShapeDtypeStruct((bsz, nt, -w, TM), BF16))
            out_specs.append(pl.BlockSpec((1, 1, -w, TM), lambda b, i: (b, i, 0, 0)))
        else:
            out_shapes.append(jax.ShapeDtypeStruct((bsz, t, w), BF16))
            out_specs.append(_row_spec(w))
    body = _ab_proj_kernel if kind == "ab" else _diff_proj_kernel
    outs = pl.pallas_call(
        functools.partial(body, has_res),
        grid=(bsz, nt),
        in_specs=specs,
        out_specs=out_specs,
        out_shape=out_shapes,
        compiler_params=_cparams(2),
        name=kind + "_proj",
    )(*ins)
    if has_res:
        return outs[0], outs[1:]
    return x, outs


def _pair_masks(mode, lane):
    if mode == "mla":
        in_a = (lane < 64) | ((lane >= 128) & (lane < 144)) | ((lane >= 192) & (lane < 208))
        in_b = ((lane >= 64) & (lane < 128)) | ((lane >= 144) & (lane < 160)) | ((lane >= 208) & (lane < 224))
    else:
        in_a = (lane < 32) | ((lane >= 64) & (lane < 96))
        in_b = ((lane >= 32) & (lane < 64)) | (lane >= 96)
    return in_a, in_b


def _qk(q, k):
    return lax.dot_general(q, k, (((1,), (1,)), ((), ())), preferred_element_type=F32)


def _dense_attn_kernel(mode, lambda_init, nlat, *refs):
    if mode == "diff":
        q_ref, k_ref, vt_ref, lq1_ref, lk1_ref, lq2_ref, lk2_ref, sub_ref, o_ref, s_buf, acc_buf = refs
    else:
        q_ref, k_ref, vt_ref, o_ref, s_buf, acc_buf = refs
    i = pl.program_id(2)
    qt = q_ref[0].T
    width, tq = qt.shape
    row = lax.broadcasted_iota(jnp.int32, (width, 1), 0)
    in_a, in_b = _pair_masks(mode, row)
    zero = jnp.zeros_like(qt)
    q2 = jnp.concatenate([jnp.where(in_a, qt, zero), jnp.where(in_b, qt, zero)], axis=1)

    def scores(c):
        off = pl.multiple_of(c * TK, TK)
        return jnp.dot(k_ref[0, pl.ds(off, TK), :], q2, preferred_element_type=F32)

    def absorb(slot, c, m, l):
        s = s_buf[slot]
        mn = jnp.maximum(m, jnp.max(s, axis=0, keepdims=True))
        a = jnp.exp2(m - mn)
        p = jnp.exp2(s - mn)
        l = a * l + jnp.sum(p, axis=0, keepdims=True)
        acc_buf[...] = a * acc_buf[...] + jnp.dot(vt_ref[0, c], p.astype(BF16), preferred_element_type=F32)
        return mn, l

    s = scores(nlat)
    m = jnp.max(s, axis=0, keepdims=True)
    p = jnp.exp2(s - m)
    l = jnp.sum(p, axis=0, keepdims=True)
    acc_buf[...] = jnp.dot(vt_ref[0, nlat], p.astype(BF16), preferred_element_type=F32)
    s_buf[0] = scores(0)

    def body(cc, carry):
        m, l = carry
        c0 = 2 * cc
        s_buf[1] = scores(c0 + 1)
        m, l = absorb(0, c0, m, l)
        s_buf[0] = scores(jnp.minimum(c0 + 2, nlat - 2))
        m, l = absorb(1, c0 + 1, m, l)
        return m, l

    m, l = lax.fori_loop(0, jnp.where(i == nlat, 0, nlat // 2), body, (m, l))
    o2 = acc_buf[...] * (1.0 / l)
    oa, ob = o2[:, :tq], o2[:, tq:]
    if mode == "diff":
        lam = (jnp.exp(jnp.sum(lq1_ref[...] * lk1_ref[...], axis=1, keepdims=True))
               - jnp.exp(jnp.sum(lq2_ref[...] * lk2_ref[...], axis=1, keepdims=True)) + lambda_init)
        o = (oa - lam * ob).T
        o = _rms(o, sub_ref[...], DIFF_EPS) * (1.0 - lambda_init)
    else:
        vrow = lax.broadcasted_iota(jnp.int32, (LANES, 1), 0)
        o = jnp.where(vrow < 64, oa, ob).T
    o_ref[0] = o.astype(BF16)


def _dense_attn_call(mode, q, k, vt, extra=(), lambda_init=0.0):
    bsz, t, qtot = q.shape
    width = 256 if mode == "mla" else 128
    npairs = qtot // width
    nt = t // TM
    ins = [q, k, vt]
    specs = [pl.BlockSpec((1, TM, width), lambda b, j, i: (b, i, j)),
             pl.BlockSpec((1, t, width), lambda b, j, i: (b, 0, j)),
             pl.BlockSpec((1, t // TK, LANES, TK), lambda b, j, i: (b, 0, j, 0))]
    for e in extra:
        ins.append(e)
        specs.append(pl.BlockSpec(e.shape, lambda b, j, i: (0, 0)))
    return pl.pallas_call(
        functools.partial(_dense_attn_kernel, mode, lambda_init, nt - 1),
        grid=(bsz, npairs, nt),
        in_specs=specs,
        out_specs=pl.BlockSpec((1, TM, LANES), lambda b, j, i: (b, i, j)),
        out_shape=jax.ShapeDtypeStruct((bsz, t, npairs * LANES), BF16),
        scratch_shapes=[pltpu.VMEM((2, TK, 2 * TM), F32), pltpu.VMEM((LANES, 2 * TM), F32)],
        compiler_params=_cparams(3),
        name=mode + "_attn",
    )(*ins)


def _swa_kernel(nlat, sink_ref, q_ref, k_ref, v_ref, o_ref):
    j = pl.program_id(1)
    i = pl.program_id(2)
    s_len = nlat * TM
    kb = TM + 2 * SWA_WINDOW
    q = q_ref[0]
    in_a, in_b = _pair_masks("pair", lax.broadcasted_iota(jnp.int32, (1, LANES), 1))
    zero = jnp.zeros_like(q)
    start = i * TM
    kstart = pl.multiple_of(jnp.clip(start - SWA_WINDOW, 0, s_len - kb), SWA_WINDOW)
    k_loc = k_ref[0, pl.ds(kstart, kb), :]
    v_loc = v_ref[0, pl.ds(kstart, kb), :]
    k_ctx = k_ref[0, s_len:s_len + TM, :]
    v_ctx = v_ref[0, s_len:s_len + TM, :]
    qpos = start + lax.broadcasted_iota(jnp.int32, (TM, kb), 0)
    kpos = kstart + lax.broadcasted_iota(jnp.int32, (TM, kb), 1)
    mask = (jnp.abs(qpos - kpos) <= SWA_WINDOW) & (i < nlat)

    def head(in_x, sink):
        qx = jnp.where(in_x, q, zero)
        s_loc = jnp.where(mask, _qk(qx, k_loc), NEG)
        s_ctx = _qk(qx, k_ctx)
        m = jnp.maximum(jnp.maximum(jnp.max(s_loc, axis=1, keepdims=True),
                                    jnp.max(s_ctx, axis=1, keepdims=True)), sink)
        p_loc = jnp.exp(s_loc - m)
        p_ctx = jnp.exp(s_ctx - m)
        l = (jnp.sum(p_loc, axis=1, keepdims=True) + jnp.sum(p_ctx, axis=1, keepdims=True)
             + jnp.exp(sink - m))
        o = (jnp.dot(p_loc.astype(BF16), v_loc, preferred_element_type=F32)
             + jnp.dot(p_ctx.astype(BF16), v_ctx, preferred_element_type=F32))
        return o * (1.0 / l)

    oa = head(in_a, sink_ref[2 * j])
    ob = head(in_b, sink_ref[2 * j + 1])
    lane = lax.broadcasted_iota(jnp.int32, (1, LANES), 1)
    o_ref[0] = jnp.where(lane < 64, oa, ob).astype(BF16)


def _swa_call(sink, q, k, v):
    bsz, t, qtot = q.shape
    npairs = qtot // LANES
    nt = t // TM
    grid_spec = pltpu.PrefetchScalarGridSpec(
        num_scalar_prefetch=1,
        grid=(bsz, npairs, nt),
        in_specs=[pl.BlockSpec((1, TM, LANES), lambda b, j, i, s: (b, i, j)),
                  pl.BlockSpec((1, t, LANES), lambda b, j, i, s: (b, 0, j // 2)),
                  pl.BlockSpec((1, t, LANES), lambda b, j, i, s: (b, 0, j // 2))],
        out_specs=pl.BlockSpec((1, TM, LANES), lambda b, j, i, s: (b, i, j)),
    )
    return pl.pallas_call(
        functools.partial(_swa_kernel, nt - 1),
        grid_spec=grid_spec,
        out_shape=jax.ShapeDtypeStruct((bsz, t, qtot), BF16),
        compiler_params=_cparams(3),
        name="swa_attn",
    )(sink, q, k, v)


def _out_proj_kernel(n_o, *refs):
    o_refs = refs[:n_o]
    w_refs = refs[n_o:2 * n_o]
    x_ref, mod_ref, g_ref, wr_ref, br_ref, xo_ref, h_ref, lg_ref = refs[2 * n_o:]
    acc = jnp.dot(o_refs[0][0], w_refs[0][...], preferred_element_type=F32)
    for n in range(1, n_o):
        acc = acc + jnp.dot(o_refs[n][0], w_refs[n][...], preferred_element_type=F32)
    m = mod_ref[0, 0]
    x = x_ref[0] + m[2:3, :] * acc
    xo_ref[0] = x
    h = _rms(x, g_ref[...], NORM_EPS) * (1.0 + m[4:5, :]) + m[3:4, :]
    h_ref[0] = h.astype(BF16)
    lg_ref[0] = jnp.dot(h, wr_ref[...], precision=lax.Precision.HIGHEST,
                        preferred_element_type=F32) + br_ref[...]


def _out_proj_call(os_, ws, x, mod, g, wr, br):
    bsz, t, d = x.shape
    nt = t // TM
    nlat = nt - 1
    n_o = len(os_)
    ins = list(os_) + list(ws) + [x, mod, g.reshape(1, d), wr, br]
    specs = ([_row_spec(o.shape[-1]) for o in os_] + [_full_spec(w.shape) for w in ws]
             + [_row_spec(d), _mod_spec(nlat), _full_spec((1, d)), _full_spec(wr.shape), _full_spec(br.shape)])
    return pl.pallas_call(
        functools.partial(_out_proj_kernel, n_o),
        grid=(bsz, nt),
        in_specs=specs,
        out_specs=[_row_spec(d), _row_spec(d), _row_spec(LANES)],
        out_shape=[jax.ShapeDtypeStruct((bsz, t, d), F32),
                   jax.ShapeDtypeStruct((bsz, t, d), BF16),
                   jax.ShapeDtypeStruct((bsz, t, LANES), F32)],
        compiler_params=_cparams(2),
        name="out_proj",
    )(*ins)


def _moe_kernel(be_ref, nu_ref, x_ref, g_ref, w1_ref, w3_ref, w2_ref, o_ref):
    i = pl.program_id(0)

    @pl.when(i < nu_ref[0])
    def _():
        x = x_ref[...]
        a = jnp.dot(x, w1_ref[0], preferred_element_type=F32)
        b = jnp.dot(x, w3_ref[0], preferred_element_type=F32)
        hmid = (a * jax.nn.sigmoid(a)) * b
        y = jnp.dot(hmid.astype(BF16), w2_ref[0], preferred_element_type=F32)
        o_ref[...] = y * g_ref[...]

    @pl.when(i >= nu_ref[0])
    def _():
        o_ref[...] = jnp.zeros_like(o_ref)


def _moe_call(block_e, n_used, xb, slot_gate, w1, w3, w2):
    n_slots, d = xb.shape
    n_blocks = n_slots // TMOE
    ff = w1.shape[-1]
    grid_spec = pltpu.PrefetchScalarGridSpec(
        num_scalar_prefetch=2,
        grid=(n_blocks,),
        in_specs=[pl.BlockSpec((TMOE, d), lambda i, be, nu: (i, 0)),
                  pl.BlockSpec((TMOE, 1), lambda i, be, nu: (i, 0)),
                  pl.BlockSpec((1, d, ff), lambda i, be, nu: (be[i], 0, 0)),
                  pl.BlockSpec((1, d, ff), lambda i, be, nu: (be[i], 0, 0)),
                  pl.BlockSpec((1, ff, d), lambda i, be, nu: (be[i], 0, 0))],
        out_specs=pl.BlockSpec((TMOE, d), lambda i, be, nu: (i, 0)),
    )
    return pl.pallas_call(
        _moe_kernel,
        grid_spec=grid_spec,
        out_shape=jax.ShapeDtypeStruct((n_slots, d), F32),
        compiler_params=_cparams(1),
        name="moe_experts",
    )(block_e, n_used, xb, slot_gate, w1, w3, w2)


def _route_and_dispatch(logits):
    n_tok = logits.shape[0]
    g_logits = logits[:, :N_GROUPS]
    g_prob = jax.nn.softmax(g_logits, axis=-1)
    grp = jnp.argmax(g_logits, axis=-1).astype(jnp.int32)
    p_grp = jnp.take_along_axis(g_prob, grp[:, None], axis=-1)
    e_logits = logits[:, N_GROUPS:N_GROUPS + N_EXPERTS].reshape(n_tok, N_GROUPS, EXPERTS_PER_GROUP)
    e_logits = jnp.take_along_axis(e_logits, grp[:, None, None], axis=1)[:, 0]
    top_p, top_i = lax.top_k(jax.nn.softmax(e_logits, axis=-1), TOP_K)
    gate = p_grp * top_p / jnp.sum(top_p, axis=-1, keepdims=True)
    eid = grp[:, None] * EXPERTS_PER_GROUP + top_i.astype(jnp.int32)

    onehot = jnp.sum((eid[:, :, None] == jnp.arange(N_EXPERTS, dtype=jnp.int32)).astype(jnp.int32), axis=1)
    cum = jnp.cumsum(onehot, axis=0) - onehot
    counts = jnp.sum(onehot, axis=0)
    rank = jnp.take_along_axis(cum, eid, axis=1)
    padded = (counts + TMOE - 1) // TMOE * TMOE
    pad_end = jnp.cumsum(padded)
    pad_start = pad_end - padded
    dest = pad_start[eid] + rank
    n_assign = n_tok * TOP_K
    n_blocks = (n_assign + N_EXPERTS * (TMOE - 1) + TMOE - 1) // TMOE
    n_slots = n_blocks * TMOE
    flat_dest = dest.reshape(n_assign)
    flat_tok = jnp.repeat(jnp.arange(n_tok, dtype=jnp.int32), TOP_K)
    slot_tok = jnp.zeros((n_slots,), jnp.int32).at[flat_dest].set(flat_tok)
    slot_gate = jnp.zeros((n_slots,), F32).at[flat_dest].set(gate.reshape(n_assign))
    block_e = jnp.minimum(jnp.searchsorted(pad_end, jnp.arange(n_blocks, dtype=jnp.int32) * TMOE, side='right'),
                          N_EXPERTS - 1).astype(jnp.int32)
    n_used = (pad_end[-1:] // TMOE).astype(jnp.int32)
    return slot_tok, slot_gate.reshape(n_slots, 1), block_e, n_used, dest


def _final_kernel(x_ref, y_ref, pmod_ref, g_ref, o_ref):
    x = x_ref[0] + pmod_ref[0, 0][5:6, :] * y_ref[0]
    o_ref[0] = _rms(x, g_ref[...], NORM_EPS)


def _final_call(x, y, pmod, g, s_len):
    bsz, t, d = x.shape
    return pl.pallas_call(
        _final_kernel,
        grid=(bsz, s_len // TM),
        in_specs=[_row_spec(d), _row_spec(d),
                  pl.BlockSpec((1, 1, 8, d), lambda b, i: (b, 0, 0, 0)),
                  _full_spec((1, d))],
        out_specs=_row_spec(d),
        out_shape=jax.ShapeDtypeStruct((bsz, s_len, d), F32),
        compiler_params=_cparams(2),
        name="final_norm",
    )(x, y, pmod, g.reshape(1, d))


def _take_cols(w, idx):
    wz = jnp.concatenate([w, jnp.zeros((w.shape[0], 1), w.dtype)], axis=1)
    return jnp.take(wz, jnp.asarray(idx, dtype=jnp.int32), axis=1).astype(BF16)


def _ab_layouts():
    zc = 1184
    cols = []
    for j in range(4):
        cols.append(j * 128 + _PAIR_PERM)
    for g in range(2):
        base = 512 + g * 64
        cols.append(base + np.concatenate([np.arange(0, 32), np.arange(0, 32), np.arange(32, 64), np.arange(32, 64)]))
    for g in range(2):
        base = 640 + g * 64
        cols.append(base + np.concatenate([np.arange(64), np.arange(64)]))
    cols.append(768 + np.arange(256))
    cols.append(1024 + np.arange(128))
    kr = 1152
    z32 = np.full((32,), zc)
    cols.append(np.concatenate([kr + np.arange(16), kr + np.arange(16), z32,
                                kr + 16 + np.arange(16), kr + 16 + np.arange(16), z32]))
    w1_idx = np.concatenate(cols)

    zq = 768
    uq = []
    z32q = np.full((32,), zq)
    for j in range(4):
        a, b = 2 * j * 96, (2 * j + 1) * 96
        uq.append(np.concatenate([a + np.arange(64), b + np.arange(64),
                                  a + 64 + np.arange(16), b + 64 + np.arange(16), z32q,
                                  a + 80 + np.arange(16), b + 80 + np.arange(16), z32q]))
    uq_idx = np.concatenate(uq)

    kn, mv = [], []
    for h in range(8):
        kn.append(h * 128 + np.arange(64))
        mv.append(h * 128 + 64 + np.arange(64))
    ukv_idx = np.concatenate(kn + mv)
    return w1_idx, uq_idx, ukv_idx


def _diff_layout():
    cols = []
    for part in range(2):
        for h in range(8):
            cols.append(part * 1024 + h * 128 + _PAIR_PERM)
    cols.append(2048 + np.arange(1024))
    return np.concatenate(cols)


def kernel(x, c, ctx, c_ctx, norm_mix, norm_ffn, ada_w, ada_b, ab_w_in, mla_q_norm, mla_w_uq, mla_kv_norm, mla_w_ukv, swa_sink, ab_w_out, diff_w_in, diff_lambda_q1, diff_lambda_k1, diff_lambda_q2, diff_lambda_k2, diff_subln, diff_w_out, router_group_w, router_group_b, router_expert_w, router_expert_b, expert_w1, expert_w3, expert_w2, final_norm):
    bsz, s_len, d = x.shape
    c_len = ctx.shape[1]
    depth = ada_w.shape[0]
    assert d == D_MODEL and c_len == TM and s_len % TM == 0 and s_len >= TM + 2 * SWA_WINDOW
    t = s_len + c_len

    xs = jnp.concatenate([x, ctx], axis=1)
    tables = _rope_tables(s_len, c_len)
    c64, s64, c32, s32 = tables

    n_rows = (bsz + 1 + 7) // 8 * 8
    rows = jnp.concatenate([c, c_ctx[None, :], jnp.zeros((n_rows - bsz - 1, d), F32)], axis=0)
    mod_all = _ada_call(rows, ada_w, ada_b)
    mod_lat = mod_all[:, :bsz].reshape(depth, bsz, 1, 6, d)
    mod_ctx = jnp.broadcast_to(mod_all[:, bsz].reshape(depth, 1, 1, 6, d), (depth, bsz, 1, 6, d))
    mods = jnp.concatenate([mod_lat, mod_ctx], axis=2)
    mods = jnp.concatenate([mods, jnp.zeros((depth, bsz, 2, 2, d), F32)], axis=3)

    w1_idx, uq_idx, ukv_idx = _ab_layouts()
    diff_idx = _diff_layout()

    res = None
    for l in range(depth):
        j = l // 2
        mod = mods[l]
        if l % 2 == 0:
            weights = [_take_cols(ab_w_in[j], w1_idx), mla_q_norm[j].reshape(1, -1), mla_kv_norm[j].reshape(1, -1),
                       _take_cols(mla_w_uq[j], uq_idx), _take_cols(mla_w_ukv[j], ukv_idx)]
            xs, (sq, sk, sv, mq, mk, mv) = _proj_call("ab", xs, res, mod, norm_mix[l], weights,
                                                      [c64, s64, c32, s32], [512, 256, 256, 1024, 1024, -512])
            o_a = _swa_call(swa_sink[j], sq, sk, sv)
            o_b = _dense_attn_call("mla", mq, mk, mv)
            w_out = ab_w_out[j].astype(BF16)
            attn_outs, out_ws = [o_a, o_b], [w_out[:512], w_out[512:]]
        else:
            lambda_init = 0.8 - 0.6 * math.exp(-0.3 * l)
            weights = [_take_cols(diff_w_in[j], diff_idx)]
            xs, (dq, dk, dv) = _proj_call("diff", xs, res, mod, norm_mix[l], weights, [c64, s64], [1024, 1024, -1024])
            extra = [diff_lambda_q1[j].reshape(1, -1), diff_lambda_k1[j].reshape(1, -1),
                     diff_lambda_q2[j].reshape(1, -1), diff_lambda_k2[j].reshape(1, -1),
                     diff_subln[j].reshape(1, -1)]
            o_d = _dense_attn_call("diff", dq, dk, dv, extra=extra, lambda_init=lambda_init)
            attn_outs, out_ws = [o_d], [diff_w_out[j].astype(BF16)]

        wr = jnp.concatenate([router_group_w[l], router_expert_w[l],
                              jnp.zeros((d, LANES - N_GROUPS - N_EXPERTS), F32)], axis=1)
        br = jnp.concatenate([router_group_b[l], router_expert_b[l],
                              jnp.zeros((LANES - N_GROUPS - N_EXPERTS,), F32)]).reshape(1, LANES)
        xs, h2, logits = _out_proj_call(attn_outs, out_ws, xs, mod, norm_ffn[l], wr, br)

        n_tok = bsz * t
        slot_tok, slot_gate, block_e, n_used, dest = _route_and_dispatch(logits.reshape(n_tok, LANES))
        xb = jnp.take(h2.reshape(n_tok, d), slot_tok, axis=0)
        yb = _moe_call(block_e, n_used, xb, slot_gate, expert_w1[l].astype(BF16), expert_w3[l].astype(BF16),
                       expert_w2[l].astype(BF16))
        y = (jnp.take(yb, dest[:, 0], axis=0) + jnp.take(yb, dest[:, 1], axis=0)).reshape(bsz, t, d)
        res = (y, mod)

    return _final_call(xs, res[0], res[1], final_norm, s_len)
```

```python
import functools
import math

import numpy as np
import jax
import jax.numpy as jnp
from jax import lax
from jax.experimental import pallas as pl
from jax.experimental.pallas import tpu as pltpu

F32 = jnp.float32
BF16 = jnp.bfloat16

D_MODEL = 1024
GRID_W = 64
ROPE_BASE = 10000.0
NORM_EPS = 1e-6
DIFF_EPS = 1e-5
NEG = -1e30

SWA_WINDOW = 128
MLA_SCALE = (64 + 32) ** -0.5
HEAD_SCALE = 64 ** -0.5
LOG2E = math.log2(math.e)

N_GROUPS = 4
EXPERTS_PER_GROUP = 8
N_EXPERTS = 32
TOP_K = 2
EXPERT_FF = 512

LANES = 128
TM = 256
TK = 256
TQ = 512
TMOE = 256
VMEM_LIMIT = 56 * 1024 * 1024


def _cparams(n_axes):
    return pltpu.CompilerParams(dimension_semantics=("arbitrary",) * n_axes,
                                vmem_limit_bytes=VMEM_LIMIT)


def _rms(x, g, eps):
    return x * lax.rsqrt(jnp.mean(x * x, axis=-1, keepdims=True) + eps) * g


def _rope_block(x, c, s):
    return x * c + pltpu.roll(x, 64, 1) * s


_PAIR_PERM = np.concatenate([np.arange(0, 32), np.arange(64, 96), np.arange(32, 64), np.arange(96, 128)])


def _rope_tables(s_len, c_len):
    rows = s_len // GRID_W
    row = jnp.repeat(jnp.arange(rows, dtype=F32), GRID_W)
    col = jnp.tile(jnp.arange(GRID_W, dtype=F32), rows)

    def tab(dim):
        nf = dim // 4
        inv = ROPE_BASE ** (-jnp.arange(nf, dtype=F32) / nf)
        ang = jnp.concatenate([row[:, None] * inv, col[:, None] * inv], axis=-1)
        return jnp.cos(ang), jnp.sin(ang)

    cos64, sin64 = tab(64)
    cos32, sin32 = tab(32)
    c64 = jnp.concatenate([cos64] * 4, axis=-1)
    s64 = jnp.concatenate([-sin64, -sin64, sin64, sin64], axis=-1)
    one = jnp.ones((s_len, 32), F32)
    zero = jnp.zeros((s_len, 32), F32)
    c32 = jnp.concatenate([cos32, cos32, one, cos32, cos32, one], axis=-1)
    s32 = jnp.concatenate([-sin32, -sin32, zero, sin32, sin32, zero], axis=-1)

    def ext(t, fill):
        return jnp.concatenate([t, jnp.full((c_len, LANES), fill, F32)], axis=0)

    return ext(c64, 1.0), ext(s64, 0.0), ext(c32, 1.0), ext(s32, 0.0)


def _ada_kernel(x_ref, w_ref, b_ref, o_ref):
    x = x_ref[...]
    sx = x * jax.nn.sigmoid(x)
    o_ref[0] = jnp.dot(sx.astype(BF16), w_ref[0].astype(BF16), preferred_element_type=F32) + b_ref[0]


def _ada_call(rows, ada_w, ada_b):
    depth, d, n6 = ada_w.shape
    r = rows.shape[0]
    tn = 1536
    return pl.pallas_call(
        _ada_kernel,
        grid=(depth, n6 // tn),
        in_specs=[pl.BlockSpec((r, d), lambda l, j: (0, 0)),
                  pl.BlockSpec((1, d, tn), lambda l, j: (l, 0, j)),
                  pl.BlockSpec((1, 1, tn), lambda l, j: (l, 0, j))],
        out_specs=pl.BlockSpec((1, r, tn), lambda l, j: (l, 0, j)),
        out_shape=jax.ShapeDtypeStruct((depth, r, n6), F32),
        compiler_params=_cparams(2),
        name="ada_mod",
    )(rows, ada_w, ada_b.reshape(depth, 1, n6))


def _prenorm(has_res, x_ref, y_ref, pmod_ref, mod_ref, g_ref, xo_ref):
    x = x_ref[0]
    if has_res:
        x = x + pmod_ref[0, 0][5:6, :] * y_ref[0]
        xo_ref[0] = x
    m = mod_ref[0, 0]
    return _rms(x, g_ref[...], NORM_EPS) * (1.0 + m[1:2, :]) + m[0:1, :]


def _ab_proj_kernel(has_res, *refs):
    if has_res:
        x_ref, y_ref, pmod_ref = refs[:3]
        refs = refs[3:]
    else:
        x_ref, y_ref, pmod_ref = refs[0], None, None
        refs = refs[1:]
    (mod_ref, g_ref, w1_ref, qn_ref, kvn_ref, wuq_ref, wukv_ref,
     c64_ref, s64_ref, c32_ref, s32_ref) = refs[:11]
    outs = refs[11:]
    if has_res:
        xo_ref, outs = outs[0], outs[1:]
    else:
        xo_ref = None
    sq_ref, sk_ref, sv_ref, mq_ref, mk_ref, mv_ref = outs

    h = _prenorm(has_res, x_ref, y_ref, pmod_ref, mod_ref, g_ref, xo_ref)
    p = jnp.dot(h.astype(BF16), w1_ref[...], preferred_element_type=F32)
    c64, s64, c32, s32 = c64_ref[...], s64_ref[...], c32_ref[...], s32_ref[...]
    for j in range(4):
        blk = _rope_block(p[:, j * 128:(j + 1) * 128], c64, s64)
        sq_ref[0, :, j * 128:(j + 1) * 128] = (blk * HEAD_SCALE).astype(BF16)
    for j in range(2):
        blk = _rope_block(p[:, 512 + j * 128:512 + (j + 1) * 128], c64, s64)
        sk_ref[0, :, j * 128:(j + 1) * 128] = blk.astype(BF16)
    sv_ref[0] = p[:, 768:1024].astype(BF16)
    cq = p[:, 1024:1280]
    ckv = p[:, 1280:1408]
    kr = _rope_block(p[:, 1408:1536], c32, s32).astype(BF16)
    qm = jnp.dot(_rms(cq, qn_ref[...], NORM_EPS).astype(BF16), wuq_ref[...], preferred_element_type=F32)
    kv = jnp.dot(_rms(ckv, kvn_ref[...], NORM_EPS).astype(BF16), wukv_ref[...], preferred_element_type=F32)
    for j in range(4):
        mq_ref[0, :, j * 256:j * 256 + 128] = (qm[:, j * 256:j * 256 + 128] * (MLA_SCALE * LOG2E)).astype(BF16)
        rr = _rope_block(qm[:, j * 256 + 128:(j + 1) * 256], c32, s32)
        mq_ref[0, :, j * 256 + 128:(j + 1) * 256] = (rr * (MLA_SCALE * LOG2E)).astype(BF16)
        mk_ref[0, :, j * 256:j * 256 + 128] = kv[:, j * 128:(j + 1) * 128].astype(BF16)
        mk_ref[0, :, j * 256 + 128:(j + 1) * 256] = kr
    mv_ref[0, 0] = kv[:, 512:1024].T.astype(BF16)


def _diff_proj_kernel(has_res, *refs):
    if has_res:
        x_ref, y_ref, pmod_ref = refs[:3]
        refs = refs[3:]
    else:
        x_ref, y_ref, pmod_ref = refs[0], None, None
        refs = refs[1:]
    mod_ref, g_ref, w_ref, c64_ref, s64_ref = refs[:5]
    outs = refs[5:]
    if has_res:
        xo_ref, outs = outs[0], outs[1:]
    else:
        xo_ref = None
    q_ref, k_ref, v_ref = outs
    h = _prenorm(has_res, x_ref, y_ref, pmod_ref, mod_ref, g_ref, xo_ref)
    p = jnp.dot(h.astype(BF16), w_ref[...], preferred_element_type=F32)
    c64, s64 = c64_ref[...], s64_ref[...]
    for j in range(8):
        blk = _rope_block(p[:, j * 128:(j + 1) * 128], c64, s64)
        q_ref[0, :, j * 128:(j + 1) * 128] = (blk * (HEAD_SCALE * LOG2E)).astype(BF16)
        blk = _rope_block(p[:, 1024 + j * 128:1024 + (j + 1) * 128], c64, s64)
        k_ref[0, :, j * 128:(j + 1) * 128] = blk.astype(BF16)
    v_ref[0, 0] = p[:, 2048:3072].T.astype(BF16)


def _row_spec(width):
    return pl.BlockSpec((1, TM, width), lambda b, i: (b, i, 0))


def _mod_spec(nlat):
    return pl.BlockSpec((1, 1, 8, D_MODEL), lambda b, i: (b, i // nlat, 0, 0))


def _full_spec(shape):
    nd = len(shape)
    return pl.BlockSpec(shape, lambda b, i: (0,) * nd)


def _tab_spec():
    return pl.BlockSpec((TM, LANES), lambda b, i: (i, 0))


def _proj_call(kind, x, res, mod, g, weights, tables, out_widths):
    bsz, t, d = x.shape
    nt = t // TM
    nlat = nt - 1
    has_res = res is not None
    ins, specs = [x], [_row_spec(d)]
    if has_res:
        y, pmod = res
        ins += [y, pmod]
        specs += [_row_spec(d), _mod_spec(nlat)]
    ins += [mod, g.reshape(1, d)]
    specs += [_mod_spec(nlat), _full_spec((1, d))]
    for w in weights:
        ins.append(w)
        specs.append(_full_spec(w.shape))
    for tb in tables:
        ins.append(tb)
        specs.append(_tab_spec())
    out_shapes, out_specs = [], []
    if has_res:
        out_shapes.append(jax.ShapeDtypeStruct((bsz, t, d), F32))
        out_specs.append(_row_spec(d))
    for w in out_widths:
        if w < 0:
            out_shapes.append(jax.ShapeDtypeStruct((bsz, nt, -w, TM), BF16))
            out_specs.append(pl.BlockSpec((1, 1, -w, TM), lambda b, i: (b, i, 0, 0)))
        else:
            out_shapes.append(jax.ShapeDtypeStruct((bsz, t, w), BF16))
            out_specs.append(_row_spec(w))
    body = _ab_proj_kernel if kind == "ab" else _diff_proj_kernel
    outs = pl.pallas_call(
        functools.partial(body, has_res),
        grid=(bsz, nt),
        in_specs=specs,
        out_specs=out_specs,
        out_shape=out_shapes,
        compiler_params=_cparams(2),
        name=kind + "_proj",
    )(*ins)
    if has_res:
        return outs[0], outs[1:]
    return x, outs


def _pair_masks(mode, lane):
    if mode == "mla":
        in_a = (lane < 64) | ((lane >= 128) & (lane < 144)) | ((lane >= 192) & (lane < 208))
        in_b = ((lane >= 64) & (lane < 128)) | ((lane >= 144) & (lane < 160)) | ((lane >= 208) & (lane < 224))
    else:
        in_a = (lane < 32) | ((lane >= 64) & (lane < 96))
        in_b = ((lane >= 32) & (lane < 64)) | (lane >= 96)
    return in_a, in_b


def _qk(q, k):
    return lax.dot_general(q, k, (((1,), (1,)), ((), ())), preferred_element_type=F32)


def _dense_attn_kernel(mode, lambda_init, nlat, *refs):
    n_in = 8 if mode == "diff" else 3
    q_ref, k_ref, vt_ref = refs[:3]
    o_ref = refs[n_in]
    s_bufs = refs[n_in + 1:n_in + 3]
    p_bufs = refs[n_in + 3:n_in + 5]
    acc_buf, m_buf, l_buf, a_buf = refs[n_in + 5:]
    i = pl.program_id(2)
    qt = q_ref[0].T
    width, tq = qt.shape
    row = lax.broadcasted_iota(jnp.int32, (width, 1), 0)
    in_a, in_b = _pair_masks(mode, row)
    zero = jnp.zeros_like(qt)
    q2 = jnp.concatenate([jnp.where(in_a, qt, zero), jnp.where(in_b, qt, zero)], axis=1)
    chunks = [(nlat, 1)] + [(2 * c, 2) for c in range(nlat // 2)]
    n_lat_q = (nlat * TK) // tq

    def scores(chunk, s_buf):
        k0, n = chunk
        s_buf[0:n * TK, :] = jnp.dot(k_ref[0, k0 * TK:(k0 + n) * TK, :], q2, preferred_element_type=F32)

    def pv(chunk, p_buf):
        k0, n = chunk
        out = jnp.dot(vt_ref[0, k0], p_buf[0:TK, :], preferred_element_type=F32)
        for r in range(1, n):
            out = out + jnp.dot(vt_ref[0, k0 + r], p_buf[r * TK:(r + 1) * TK, :], preferred_element_type=F32)
        return out

    def softmax(chunk, s_buf, p_buf, first):
        n = chunk[1]
        s = s_buf[0:n * TK, :]
        mx = jnp.max(s, axis=0, keepdims=True)
        if first:
            mn = mx
        else:
            m = m_buf[...]
            mn = jnp.maximum(m, mx)
            a_buf[...] = jnp.exp2(m - mn)
        p = jnp.exp2(s - mn)
        ps = jnp.sum(p, axis=0, keepdims=True)
        l_buf[...] = ps if first else a_buf[...] * l_buf[...] + ps
        m_buf[...] = mn
        p_buf[0:n * TK, :] = p.astype(BF16)

    def accumulate(chunk, p_buf, first):
        if first:
            acc_buf[...] = pv(chunk, p_buf)
        else:
            acc_buf[...] = a_buf[...] * acc_buf[...] + pv(chunk, p_buf)

    def pipeline(chs):
        scores(chs[0], s_bufs[0])
        for c, ch in enumerate(chs):
            if c >= 2:
                accumulate(chs[c - 1], p_bufs[(c - 1) % 2], first=False)
            if c + 1 < len(chs):
                scores(chs[c + 1], s_bufs[(c + 1) % 2])
            softmax(ch, s_bufs[c % 2], p_bufs[c % 2], first=c == 0)
            if c == 1:
                accumulate(chs[0], p_bufs[0], first=True)
        last = len(chs) - 1
        if last == 0:
            accumulate(chs[0], p_bufs[0], first=True)
        else:
            accumulate(chs[last], p_bufs[last % 2], first=False)

    @pl.when(i < n_lat_q)
    def _():
        pipeline(chunks)

    @pl.when(i >= n_lat_q)
    def _():
        pipeline(chunks[:1])

    o2 = acc_buf[...] * (1.0 / l_buf[...])
    oa, ob = o2[:, :tq], o2[:, tq:]
    if mode == "diff":
        lq1_ref, lk1_ref, lq2_ref, lk2_ref, sub_ref = refs[3:8]
        lam = (jnp.exp(jnp.sum(lq1_ref[...] * lk1_ref[...], axis=1, keepdims=True))
               - jnp.exp(jnp.sum(lq2_ref[...] * lk2_ref[...], axis=1, keepdims=True)) + lambda_init)
        o = (oa - lam * ob).T
        o = _rms(o, sub_ref[...], DIFF_EPS) * (1.0 - lambda_init)
    else:
        vrow = lax.broadcasted_iota(jnp.int32, (LANES, 1), 0)
        o = jnp.where(vrow < 64, oa, ob).T
    o_ref[0] = o.astype(BF16)


def _dense_attn_call(mode, q, k, vt, extra=(), lambda_init=0.0):
    bsz, t, qtot = q.shape
    width = 256 if mode == "mla" else 128
    npairs = qtot // width
    nq = pl.cdiv(t, TQ)
    ins = [q, k, vt]
    specs = [pl.BlockSpec((1, TQ, width), lambda b, j, i: (b, i, j)),
             pl.BlockSpec((1, t, width), lambda b, j, i: (b, 0, j)),
             pl.BlockSpec((1, t // TK, LANES, TK), lambda b, j, i: (b, 0, j, 0))]
    for e in extra:
        ins.append(e)
        specs.append(pl.BlockSpec(e.shape, lambda b, j, i: (0, 0)))
    return pl.pallas_call(
        functools.partial(_dense_attn_kernel, mode, lambda_init, t // TK - 1),
        grid=(bsz, npairs, nq),
        in_specs=specs,
        out_specs=pl.BlockSpec((1, TQ, LANES), lambda b, j, i: (b, i, j)),
        out_shape=jax.ShapeDtypeStruct((bsz, t, npairs * LANES), BF16),
        scratch_shapes=[pltpu.VMEM((2 * TK, 2 * TQ), F32), pltpu.VMEM((2 * TK, 2 * TQ), F32),
                        pltpu.VMEM((2 * TK, 2 * TQ), BF16), pltpu.VMEM((2 * TK, 2 * TQ), BF16),
                        pltpu.VMEM((LANES, 2 * TQ), F32), pltpu.VMEM((1, 2 * TQ), F32),
                        pltpu.VMEM((1, 2 * TQ), F32), pltpu.VMEM((1, 2 * TQ), F32)],
        compiler_params=_cparams(3),
        name=mode + "_attn",
    )(*ins)


def _swa_kernel(nlat, sink_ref, q_ref, k_ref, v_ref, o_ref):
    j = pl.program_id(1)
    i = pl.program_id(2)
    s_len = nlat * TM
    kb = TM + 2 * SWA_WINDOW
    q = q_ref[0]
    in_a, in_b = _pair_masks("pair", lax.broadcasted_iota(jnp.int32, (1, LANES), 1))
    zero = jnp.zeros_like(q)
    start = i * TM
    kstart = pl.multiple_of(jnp.clip(start - SWA_WINDOW, 0, s_len - kb), SWA_WINDOW)
    k_loc = k_ref[0, pl.ds(kstart, kb), :]
    v_loc = v_ref[0, pl.ds(kstart, kb), :]
    k_ctx = k_ref[0, s_len:s_len + TM, :]
    v_ctx = v_ref[0, s_len:s_len + TM, :]
    qpos = start + lax.broadcasted_iota(jnp.int32, (TM, kb), 0)
    kpos = kstart + lax.broadcasted_iota(jnp.int32, (TM, kb), 1)
    mask = (jnp.abs(qpos - kpos) <= SWA_WINDOW) & (i < nlat)

    def head(in_x, sink):
        qx = jnp.where(in_x, q, zero)
        s_loc = jnp.where(mask, _qk(qx, k_loc), NEG)
        s_ctx = _qk(qx, k_ctx)
        m = jnp.maximum(jnp.maximum(jnp.max(s_loc, axis=1, keepdims=True),
                                    jnp.max(s_ctx, axis=1, keepdims=True)), sink)
        p_loc = jnp.exp(s_loc - m)
        p_ctx = jnp.exp(s_ctx - m)
        l = (jnp.sum(p_loc, axis=1, keepdims=True) + jnp.sum(p_ctx, axis=1, keepdims=True)
             + jnp.exp(sink - m))
        o = (jnp.dot(p_loc.astype(BF16), v_loc, preferred_element_type=F32)
             + jnp.dot(p_ctx.astype(BF16), v_ctx, preferred_element_type=F32))
        return o * (1.0 / l)

    oa = head(in_a, sink_ref[2 * j])
    ob = head(in_b, sink_ref[2 * j + 1])
    lane = lax.broadcasted_iota(jnp.int32, (1, LANES), 1)
    o_ref[0] = jnp.where(lane < 64, oa, ob).astype(BF16)


def _swa_call(sink, q, k, v):
    bsz, t, qtot = q.shape
    npairs = qtot // LANES
    nt = t // TM
    grid_spec = pltpu.PrefetchScalarGridSpec(
        num_scalar_prefetch=1,
        grid=(bsz, npairs, nt),
        in_specs=[pl.BlockSpec((1, TM, LANES), lambda b, j, i, s: (b, i, j)),
                  pl.BlockSpec((1, t, LANES), lambda b, j, i, s: (b, 0, j // 2)),
                  pl.BlockSpec((1, t, LANES), lambda b, j, i, s: (b, 0, j // 2))],
        out_specs=pl.BlockSpec((1, TM, LANES), lambda b, j, i, s: (b, i, j)),
    )
    return pl.pallas_call(
        functools.partial(_swa_kernel, nt - 1),
        grid_spec=grid_spec,
        out_shape=jax.ShapeDtypeStruct((bsz, t, qtot), BF16),
        compiler_params=_cparams(3),
        name="swa_attn",
    )(sink, q, k, v)


def _out_proj_kernel(n_o, *refs):
    o_refs = refs[:n_o]
    w_refs = refs[n_o:2 * n_o]
    x_ref, mod_ref, g_ref, wr_ref, br_ref, xo_ref, h_ref, lg_ref = refs[2 * n_o:]
    acc = jnp.dot(o_refs[0][0], w_refs[0][...], preferred_element_type=F32)
    for n in range(1, n_o):
        acc = acc + jnp.dot(o_refs[n][0], w_refs[n][...], preferred_element_type=F32)
    m = mod_ref[0, 0]
    x = x_ref[0] + m[2:3, :] * acc
    xo_ref[0] = x
    h = _rms(x, g_ref[...], NORM_EPS) * (1.0 + m[4:5, :]) + m[3:4, :]
    h_ref[0] = h.astype(BF16)
    lg_ref[0] = jnp.dot(h, wr_ref[...], precision=lax.Precision.HIGHEST,
                        preferred_element_type=F32) + br_ref[...]


def _out_proj_call(os_, ws, x, mod, g, wr, br):
    bsz, t, d = x.shape
    nt = t // TM
    nlat = nt - 1
    n_o = len(os_)
    ins = list(os_) + list(ws) + [x, mod, g.reshape(1, d), wr, br]
    specs = ([_row_spec(o.shape[-1]) for o in os_] + [_full_spec(w.shape) for w in ws]
             + [_row_spec(d), _mod_spec(nlat), _full_spec((1, d)), _full_spec(wr.shape), _full_spec(br.shape)])
    return pl.pallas_call(
        functools.partial(_out_proj_kernel, n_o),
        grid=(bsz, nt),
        in_specs=specs,
        out_specs=[_row_spec(d), _row_spec(d), _row_spec(LANES)],
        out_shape=[jax.ShapeDtypeStruct((bsz, t, d), F32),
                   jax.ShapeDtypeStruct((bsz, t, d), BF16),
                   jax.ShapeDtypeStruct((bsz, t, LANES), F32)],
        compiler_params=_cparams(2),
        name="out_proj",
    )(*ins)


def _moe_kernel(be_ref, nu_ref, x_ref, w1_ref, w3_ref, w2_ref, o_ref, w1c, w3c, w2c):
    i = pl.program_id(0)
    e = be_ref[i]

    @pl.when((i == 0) | (e != be_ref[jnp.maximum(i - 1, 0)]))
    def _():
        w1c[...] = w1_ref[0].astype(BF16)
        w3c[...] = w3_ref[0].astype(BF16)
        w2c[...] = w2_ref[0].astype(BF16)

    @pl.when(i < nu_ref[0])
    def _():
        x = x_ref[...]
        a = jnp.dot(x, w1c[...], preferred_element_type=F32)
        b = jnp.dot(x, w3c[...], preferred_element_type=F32)
        hmid = (a * jax.nn.sigmoid(a)) * b
        o_ref[...] = jnp.dot(hmid.astype(BF16), w2c[...], preferred_element_type=F32)

    @pl.when(i >= nu_ref[0])
    def _():
        o_ref[...] = jnp.zeros_like(o_ref)


def _moe_call(block_e, n_used, xb, w1, w3, w2):
    n_slots, d = xb.shape
    n_blocks = n_slots // TMOE
    ff = w1.shape[-1]
    grid_spec = pltpu.PrefetchScalarGridSpec(
        num_scalar_prefetch=2,
        grid=(n_blocks,),
        in_specs=[pl.BlockSpec((TMOE, d), lambda i, be, nu: (i, 0)),
                  pl.BlockSpec((1, d, ff), lambda i, be, nu: (be[i], 0, 0)),
                  pl.BlockSpec((1, d, ff), lambda i, be, nu: (be[i], 0, 0)),
                  pl.BlockSpec((1, ff, d), lambda i, be, nu: (be[i], 0, 0))],
        out_specs=pl.BlockSpec((TMOE, d), lambda i, be, nu: (i, 0)),
        scratch_shapes=[pltpu.VMEM((d, ff), BF16), pltpu.VMEM((d, ff), BF16), pltpu.VMEM((ff, d), BF16)],
    )
    return pl.pallas_call(
        _moe_kernel,
        grid_spec=grid_spec,
        out_shape=jax.ShapeDtypeStruct((n_slots, d), F32),
        compiler_params=_cparams(1),
        name="moe_experts",
    )(block_e, n_used, xb, w1, w3, w2)


def _route_and_dispatch(logits):
    n_tok = logits.shape[0]
    g_logits = logits[:, :N_GROUPS]
    g_prob = jax.nn.softmax(g_logits, axis=-1)
    grp = jnp.argmax(g_logits, axis=-1).astype(jnp.int32)
    p_grp = jnp.take_along_axis(g_prob, grp[:, None], axis=-1)
    e_logits = logits[:, N_GROUPS:N_GROUPS + N_EXPERTS].reshape(n_tok, N_GROUPS, EXPERTS_PER_GROUP)
    e_logits = jnp.take_along_axis(e_logits, grp[:, None, None], axis=1)[:, 0]
    top_p, top_i = lax.top_k(jax.nn.softmax(e_logits, axis=-1), TOP_K)
    gate = p_grp * top_p / jnp.sum(top_p, axis=-1, keepdims=True)
    eid = grp[:, None] * EXPERTS_PER_GROUP + top_i.astype(jnp.int32)

    onehot = jnp.sum((eid[:, :, None] == jnp.arange(N_EXPERTS, dtype=jnp.int32)).astype(jnp.int32), axis=1)
    cum = jnp.cumsum(onehot, axis=0) - onehot
    counts = jnp.sum(onehot, axis=0)
    rank = jnp.take_along_axis(cum, eid, axis=1)
    padded = (counts + TMOE - 1) // TMOE * TMOE
    pad_end = jnp.cumsum(padded)
    pad_start = pad_end - padded
    dest = pad_start[eid] + rank
    n_assign = n_tok * TOP_K
    n_blocks = (n_assign + N_EXPERTS * (TMOE - 1) + TMOE - 1) // TMOE
    n_slots = n_blocks * TMOE
    flat_tok = jnp.repeat(jnp.arange(n_tok, dtype=jnp.int32), TOP_K)
    slot_tok = jnp.zeros((n_slots,), jnp.int32).at[dest.reshape(n_assign)].set(flat_tok)
    block_start = jnp.arange(n_blocks, dtype=jnp.int32) * TMOE
    block_e = jnp.minimum(jnp.sum((block_start[:, None] >= pad_end[None, :]).astype(jnp.int32), axis=1),
                          N_EXPERTS - 1).astype(jnp.int32)
    n_used = (pad_end[-1:] // TMOE).astype(jnp.int32)
    return slot_tok, gate, block_e, n_used, dest


def _final_kernel(x_ref, y_ref, pmod_ref, g_ref, o_ref):
    x = x_ref[0] + pmod_ref[0, 0][5:6, :] * y_ref[0]
    o_ref[0] = _rms(x, g_ref[...], NORM_EPS)


def _final_call(x, y, pmod, g, s_len):
    bsz, t, d = x.shape
    return pl.pallas_call(
        _final_kernel,
        grid=(bsz, s_len // TM),
        in_specs=[_row_spec(d), _row_spec(d),
                  pl.BlockSpec((1, 1, 8, d), lambda b, i: (b, 0, 0, 0)),
                  _full_spec((1, d))],
        out_specs=_row_spec(d),
        out_shape=jax.ShapeDtypeStruct((bsz, s_len, d), F32),
        compiler_params=_cparams(2),
        name="final_norm",
    )(x, y, pmod, g.reshape(1, d))


def _take_cols(w, idx):
    wz = jnp.concatenate([w, jnp.zeros((w.shape[0], 1), w.dtype)], axis=1)
    return jnp.take(wz, jnp.asarray(idx, dtype=jnp.int32), axis=1).astype(BF16)


def _ab_layouts():
    zc = 1184
    cols = []
    for j in range(4):
        cols.append(j * 128 + _PAIR_PERM)
    for g in range(2):
        base = 512 + g * 64
        cols.append(base + np.concatenate([np.arange(0, 32), np.arange(0, 32), np.arange(32, 64), np.arange(32, 64)]))
    for g in range(2):
        base = 640 + g * 64
        cols.append(base + np.concatenate([np.arange(64), np.arange(64)]))
    cols.append(768 + np.arange(256))
    cols.append(1024 + np.arange(128))
    kr = 1152
    z32 = np.full((32,), zc)
    cols.append(np.concatenate([kr + np.arange(16), kr + np.arange(16), z32,
                                kr + 16 + np.arange(16), kr + 16 + np.arange(16), z32]))
    w1_idx = np.concatenate(cols)

    zq = 768
    uq = []
    z32q = np.full((32,), zq)
    for j in range(4):
        a, b = 2 * j * 96, (2 * j + 1) * 96
        uq.append(np.concatenate([a + np.arange(64), b + np.arange(64),
                                  a + 64 + np.arange(16), b + 64 + np.arange(16), z32q,
                                  a + 80 + np.arange(16), b + 80 + np.arange(16), z32q]))
    uq_idx = np.concatenate(uq)

    kn, mv = [], []
    for h in range(8):
        kn.append(h * 128 + np.arange(64))
        mv.append(h * 128 + 64 + np.arange(64))
    ukv_idx = np.concatenate(kn + mv)
    return w1_idx, uq_idx, ukv_idx


def _diff_layout():
    cols = []
    for part in range(2):
        for h in range(8):
            cols.append(part * 1024 + h * 128 + _PAIR_PERM)
    cols.append(2048 + np.arange(1024))
    return np.concatenate(cols)


def kernel(x, c, ctx, c_ctx, norm_mix, norm_ffn, ada_w, ada_b, ab_w_in, mla_q_norm, mla_w_uq, mla_kv_norm, mla_w_ukv, swa_sink, ab_w_out, diff_w_in, diff_lambda_q1, diff_lambda_k1, diff_lambda_q2, diff_lambda_k2, diff_subln, diff_w_out, router_group_w, router_group_b, router_expert_w, router_expert_b, expert_w1, expert_w3, expert_w2, final_norm):
    bsz, s_len, d = x.shape
    c_len = ctx.shape[1]
    depth = ada_w.shape[0]
    assert d == D_MODEL and c_len == TM and s_len % TQ == 0 and s_len % (2 * TK) == 0
    t = s_len + c_len

    xs = jnp.concatenate([x, ctx], axis=1)
    tables = _rope_tables(s_len, c_len)
    c64, s64, c32, s32 = tables

    n_rows = (bsz + 1 + 7) // 8 * 8
    rows = jnp.concatenate([c, c_ctx[None, :], jnp.zeros((n_rows - bsz - 1, d), F32)], axis=0)
    mod_all = _ada_call(rows, ada_w, ada_b)
    mod_lat = mod_all[:, :bsz].reshape(depth, bsz, 1, 6, d)
    mod_ctx = jnp.broadcast_to(mod_all[:, bsz].reshape(depth, 1, 1, 6, d), (depth, bsz, 1, 6, d))
    mods = jnp.concatenate([mod_lat, mod_ctx], axis=2)
    mods = jnp.concatenate([mods, jnp.zeros((depth, bsz, 2, 2, d), F32)], axis=3)

    w1_idx, uq_idx, ukv_idx = _ab_layouts()
    diff_idx = _diff_layout()

    res = None
    for l in range(depth):
        j = l // 2
        mod = mods[l]
        if l % 2 == 0:
            weights = [_take_cols(ab_w_in[j], w1_idx), mla_q_norm[j].reshape(1, -1), mla_kv_norm[j].reshape(1, -1),
                       _take_cols(mla_w_uq[j], uq_idx), _take_cols(mla_w_ukv[j], ukv_idx)]
            xs, (sq, sk, sv, mq, mk, mv) = _proj_call("ab", xs, res, mod, norm_mix[l], weights,
                                                      [c64, s64, c32, s32], [512, 256, 256, 1024, 1024, -512])
            o_a = _swa_call(swa_sink[j], sq, sk, sv)
            o_b = _dense_attn_call("mla", mq, mk, mv)
            w_out = ab_w_out[j].astype(BF16)
            attn_outs, out_ws = [o_a, o_b], [w_out[:512], w_out[512:]]
        else:
            lambda_init = 0.8 - 0.6 * math.exp(-0.3 * l)
            weights = [_take_cols(diff_w_in[j], diff_idx)]
            xs, (dq, dk, dv) = _proj_call("diff", xs, res, mod, norm_mix[l], weights, [c64, s64], [1024, 1024, -1024])
            extra = [diff_lambda_q1[j].reshape(1, -1), diff_lambda_k1[j].reshape(1, -1),
                     diff_lambda_q2[j].reshape(1, -1), diff_lambda_k2[j].reshape(1, -1),
                     diff_subln[j].reshape(1, -1)]
            o_d = _dense_attn_call("diff", dq, dk, dv, extra=extra, lambda_init=lambda_init)
            attn_outs, out_ws = [o_d], [diff_w_out[j].astype(BF16)]

        wr = jnp.concatenate([router_group_w[l], router_expert_w[l],
                              jnp.zeros((d, LANES - N_GROUPS - N_EXPERTS), F32)], axis=1)
        br = jnp.concatenate([router_group_b[l], router_expert_b[l],
                              jnp.zeros((LANES - N_GROUPS - N_EXPERTS,), F32)]).reshape(1, LANES)
        xs, h2, logits = _out_proj_call(attn_outs, out_ws, xs, mod, norm_ffn[l], wr, br)

        n_tok = bsz * t
        slot_tok, gate, block_e, n_used, dest = _route_and_dispatch(logits.reshape(n_tok, LANES))
        xb = jnp.take(h2.reshape(n_tok, d), slot_tok, axis=0)
        yb = _moe_call(block_e, n_used, xb, expert_w1[l], expert_w3[l], expert_w2[l])
        y = (gate[:, 0:1] * jnp.take(yb, dest[:, 0], axis=0)
             + gate[:, 1:2] * jnp.take(yb, dest[:, 1], axis=0)).reshape(bsz, t, d)
        res = (y, mod)

    return _final_call(xs, res[0], res[1], final_norm, s_len)
```

```python
import functools
import math

import numpy as np
import jax
import jax.numpy as jnp
from jax import lax
from jax.experimental import pallas as pl
from jax.experimental.pallas import tpu as pltpu

F32 = jnp.float32
BF16 = jnp.bfloat16

D_MODEL = 1024
GRID_W = 64
ROPE_BASE = 10000.0
NORM_EPS = 1e-6
DIFF_EPS = 1e-5
NEG = -1e30

SWA_WINDOW = 128
MLA_SCALE = (64 + 32) ** -0.5
HEAD_SCALE = 64 ** -0.5
LOG2E = math.log2(math.e)

N_GROUPS = 4
EXPERTS_PER_GROUP = 8
N_EXPERTS = 32
TOP_K = 2
EXPERT_FF = 512

LANES = 128
TM = 256
TK = 256
TQ = 512
TMOE = 256
VMEM_LIMIT = 56 * 1024 * 1024


def _cparams(n_axes):
    return pltpu.CompilerParams(dimension_semantics=("arbitrary",) * n_axes,
                                vmem_limit_bytes=VMEM_LIMIT)


def _rms(x, g, eps):
    return x * lax.rsqrt(jnp.mean(x * x, axis=-1, keepdims=True) + eps) * g


def _rope_block(x, c, s):
    return x * c + pltpu.roll(x, 64, 1) * s


_PAIR_PERM = np.concatenate([np.arange(0, 32), np.arange(64, 96), np.arange(32, 64), np.arange(96, 128)])


def _rope_tables(s_len, c_len):
    rows = s_len // GRID_W
    row = jnp.repeat(jnp.arange(rows, dtype=F32), GRID_W)
    col = jnp.tile(jnp.arange(GRID_W, dtype=F32), rows)

    def tab(dim):
        nf = dim // 4
        inv = ROPE_BASE ** (-jnp.arange(nf, dtype=F32) / nf)
        ang = jnp.concatenate([row[:, None] * inv, col[:, None] * inv], axis=-1)
        return jnp.cos(ang), jnp.sin(ang)

    cos64, sin64 = tab(64)
    cos32, sin32 = tab(32)
    c64 = jnp.concatenate([cos64] * 4, axis=-1)
    s64 = jnp.concatenate([-sin64, -sin64, sin64, sin64], axis=-1)
    one = jnp.ones((s_len, 32), F32)
    zero = jnp.zeros((s_len, 32), F32)
    c32 = jnp.concatenate([cos32, cos32, one, cos32, cos32, one], axis=-1)
    s32 = jnp.concatenate([-sin32, -sin32, zero, sin32, sin32, zero], axis=-1)

    def ext(t, fill):
        return jnp.concatenate([t, jnp.full((c_len, LANES), fill, F32)], axis=0)

    return ext(c64, 1.0), ext(s64, 0.0), ext(c32, 1.0), ext(s32, 0.0)


def _ada_kernel(x_ref, w_ref, b_ref, o_ref):
    x = x_ref[...]
    sx = x * jax.nn.sigmoid(x)
    o_ref[0] = jnp.dot(sx.astype(BF16), w_ref[0].astype(BF16), preferred_element_type=F32) + b_ref[0]


def _ada_call(rows, ada_w, ada_b):
    depth, d, n6 = ada_w.shape
    r = rows.shape[0]
    tn = 1536
    return pl.pallas_call(
        _ada_kernel,
        grid=(depth, n6 // tn),
        in_specs=[pl.BlockSpec((r, d), lambda l, j: (0, 0)),
                  pl.BlockSpec((1, d, tn), lambda l, j: (l, 0, j)),
                  pl.BlockSpec((1, 1, tn), lambda l, j: (l, 0, j))],
        out_specs=pl.BlockSpec((1, r, tn), lambda l, j: (l, 0, j)),
        out_shape=jax.ShapeDtypeStruct((depth, r, n6), F32),
        compiler_params=_cparams(2),
        name="ada_mod",
    )(rows, ada_w, ada_b.reshape(depth, 1, n6))


def _first_layer_x(x_ref, ctx_ref, nlat):
    return jnp.where(pl.program_id(1) < nlat, x_ref[0], ctx_ref[0])


def _prenorm(has_res, nlat, x_ref, y_ref, pmod_ref, mod_ref, g_ref, xo_ref):
    if has_res:
        x = x_ref[0] + pmod_ref[0, 0][5:6, :] * y_ref[0]
        xo_ref[0] = x
    else:
        x = _first_layer_x(x_ref, y_ref, nlat)
    m = mod_ref[0, 0]
    return _rms(x, g_ref[...], NORM_EPS) * (1.0 + m[1:2, :]) + m[0:1, :]


def _ab_proj_kernel(has_res, nlat, *refs):
    if has_res:
        x_ref, y_ref, pmod_ref = refs[:3]
        refs = refs[3:]
    else:
        x_ref, y_ref, pmod_ref = refs[0], refs[1], None
        refs = refs[2:]
    (mod_ref, g_ref, w1_ref, qn_ref, kvn_ref, wuq_ref, wukv_ref,
     c64_ref, s64_ref, c32_ref, s32_ref) = refs[:11]
    outs = refs[11:]
    if has_res:
        xo_ref, outs = outs[0], outs[1:]
    else:
        xo_ref = None
    sq_ref, sk_ref, sv_ref, mq_ref, mk_ref, mv_ref = outs

    h = _prenorm(has_res, nlat, x_ref, y_ref, pmod_ref, mod_ref, g_ref, xo_ref)
    p = jnp.dot(h.astype(BF16), w1_ref[...], preferred_element_type=F32)
    c64, s64, c32, s32 = c64_ref[...], s64_ref[...], c32_ref[...], s32_ref[...]
    for j in range(4):
        blk = _rope_block(p[:, j * 128:(j + 1) * 128], c64, s64)
        sq_ref[0, :, j * 128:(j + 1) * 128] = (blk * (HEAD_SCALE * LOG2E)).astype(BF16)
    for j in range(2):
        blk = _rope_block(p[:, 512 + j * 128:512 + (j + 1) * 128], c64, s64)
        sk_ref[0, :, j * 128:(j + 1) * 128] = blk.astype(BF16)
    sv_ref[0, 0] = p[:, 768:1024].T.astype(BF16)
    cq = p[:, 1024:1280]
    ckv = p[:, 1280:1408]
    kr = _rope_block(p[:, 1408:1536], c32, s32).astype(BF16)
    qm = jnp.dot(_rms(cq, qn_ref[...], NORM_EPS).astype(BF16), wuq_ref[...], preferred_element_type=F32)
    kv = jnp.dot(_rms(ckv, kvn_ref[...], NORM_EPS).astype(BF16), wukv_ref[...], preferred_element_type=F32)
    for j in range(4):
        mq_ref[0, :, j * 256:j * 256 + 128] = (qm[:, j * 256:j * 256 + 128] * (MLA_SCALE * LOG2E)).astype(BF16)
        rr = _rope_block(qm[:, j * 256 + 128:(j + 1) * 256], c32, s32)
        mq_ref[0, :, j * 256 + 128:(j + 1) * 256] = (rr * (MLA_SCALE * LOG2E)).astype(BF16)
        mk_ref[0, :, j * 256:j * 256 + 128] = kv[:, j * 128:(j + 1) * 128].astype(BF16)
        mk_ref[0, :, j * 256 + 128:(j + 1) * 256] = kr
    mv_ref[0, 0] = kv[:, 512:1024].T.astype(BF16)


def _diff_proj_kernel(has_res, nlat, *refs):
    if has_res:
        x_ref, y_ref, pmod_ref = refs[:3]
        refs = refs[3:]
    else:
        x_ref, y_ref, pmod_ref = refs[0], refs[1], None
        refs = refs[2:]
    mod_ref, g_ref, w_ref, c64_ref, s64_ref = refs[:5]
    outs = refs[5:]
    if has_res:
        xo_ref, outs = outs[0], outs[1:]
    else:
        xo_ref = None
    q_ref, k_ref, v_ref = outs
    h = _prenorm(has_res, nlat, x_ref, y_ref, pmod_ref, mod_ref, g_ref, xo_ref)
    p = jnp.dot(h.astype(BF16), w_ref[...], preferred_element_type=F32)
    c64, s64 = c64_ref[...], s64_ref[...]
    for j in range(8):
        blk = _rope_block(p[:, j * 128:(j + 1) * 128], c64, s64)
        q_ref[0, :, j * 128:(j + 1) * 128] = (blk * (HEAD_SCALE * LOG2E)).astype(BF16)
        blk = _rope_block(p[:, 1024 + j * 128:1024 + (j + 1) * 128], c64, s64)
        k_ref[0, :, j * 128:(j + 1) * 128] = blk.astype(BF16)
    v_ref[0, 0] = p[:, 2048:3072].T.astype(BF16)


def _row_spec(width):
    return pl.BlockSpec((1, TM, width), lambda b, i: (b, i, 0))


def _mod_spec(nlat):
    return pl.BlockSpec((1, 1, 8, D_MODEL), lambda b, i: (b, i // nlat, 0, 0))


def _full_spec(shape):
    nd = len(shape)
    return pl.BlockSpec(shape, lambda b, i: (0,) * nd)


def _tab_spec():
    return pl.BlockSpec((TM, LANES), lambda b, i: (i, 0))


def _split_specs(nlat, d):
    return [pl.BlockSpec((1, TM, d), lambda b, i: (b, jnp.minimum(i, nlat - 1), 0)),
            pl.BlockSpec((1, TM, d), lambda b, i: (b, 0, 0))]


def _proj_call(kind, x, res, mod, g, weights, tables, out_widths):
    has_res = res is not None
    if has_res:
        bsz, t, d = x.shape
    else:
        bsz, t, d = x[0].shape[0], x[0].shape[1] + x[1].shape[1], x[0].shape[2]
    nt = t // TM
    nlat = nt - 1
    if has_res:
        y, pmod = res
        ins = [x, y, pmod]
        specs = [_row_spec(d), _row_spec(d), _mod_spec(nlat)]
    else:
        ins = list(x)
        specs = _split_specs(nlat, d)
    ins += [mod, g.reshape(1, d)]
    specs += [_mod_spec(nlat), _full_spec((1, d))]
    for w in weights:
        ins.append(w)
        specs.append(_full_spec(w.shape))
    for tb in tables:
        ins.append(tb)
        specs.append(_tab_spec())
    out_shapes, out_specs = [], []
    if has_res:
        out_shapes.append(jax.ShapeDtypeStruct((bsz, t, d), F32))
        out_specs.append(_row_spec(d))
    for w in out_widths:
        if w < 0:
            out_shapes.append(jax.ShapeDtypeStruct((bsz, nt, -w, TM), BF16))
            out_specs.append(pl.BlockSpec((1, 1, -w, TM), lambda b, i: (b, i, 0, 0)))
        else:
            out_shapes.append(jax.ShapeDtypeStruct((bsz, t, w), BF16))
            out_specs.append(_row_spec(w))
    body = _ab_proj_kernel if kind == "ab" else _diff_proj_kernel
    outs = pl.pallas_call(
        functools.partial(body, has_res, nlat),
        grid=(bsz, nt),
        in_specs=specs,
        out_specs=out_specs,
        out_shape=out_shapes,
        compiler_params=_cparams(2),
        name=kind + "_proj",
    )(*ins)
    if has_res:
        return outs[0], outs[1:]
    return x, outs


def _pair_masks(mode, lane):
    if mode == "mla":
        in_a = (lane < 64) | ((lane >= 128) & (lane < 144)) | ((lane >= 192) & (lane < 208))
        in_b = ((lane >= 64) & (lane < 128)) | ((lane >= 144) & (lane < 160)) | ((lane >= 208) & (lane < 224))
    else:
        in_a = (lane < 32) | ((lane >= 64) & (lane < 96))
        in_b = ((lane >= 32) & (lane < 64)) | (lane >= 96)
    return in_a, in_b


def _qk(q, k):
    return lax.dot_general(q, k, (((1,), (1,)), ((), ())), preferred_element_type=F32)


def _dense_attn_kernel(mode, lambda_init, nlat, *refs):
    n_in = 8 if mode == "diff" else 3
    q_ref, k_ref, vt_ref = refs[:3]
    o_ref = refs[n_in]
    s_bufs = refs[n_in + 1:n_in + 3]
    p_bufs = refs[n_in + 3:n_in + 5]
    acc_buf, m_buf, l_buf, a_buf = refs[n_in + 5:]
    i = pl.program_id(2)
    qt = q_ref[0].T
    width, tq = qt.shape
    row = lax.broadcasted_iota(jnp.int32, (width, 1), 0)
    in_a, in_b = _pair_masks(mode, row)
    zero = jnp.zeros_like(qt)
    q2 = jnp.concatenate([jnp.where(in_a, qt, zero), jnp.where(in_b, qt, zero)], axis=1)
    chunks = [(nlat, 1)] + [(2 * c, 2) for c in range(nlat // 2)]
    n_lat_q = (nlat * TK) // tq

    def scores(chunk, s_buf):
        k0, n = chunk
        s_buf[0:n * TK, :] = jnp.dot(k_ref[0, k0 * TK:(k0 + n) * TK, :], q2, preferred_element_type=F32)

    def pv(chunk, p_buf):
        k0, n = chunk
        out = jnp.dot(vt_ref[0, k0], p_buf[0:TK, :], preferred_element_type=F32)
        for r in range(1, n):
            out = out + jnp.dot(vt_ref[0, k0 + r], p_buf[r * TK:(r + 1) * TK, :], preferred_element_type=F32)
        return out

    def softmax(chunk, s_buf, p_buf, first):
        n = chunk[1]
        s = s_buf[0:n * TK, :]
        mx = jnp.max(s, axis=0, keepdims=True)
        if first:
            mn = mx
        else:
            m = m_buf[...]
            mn = jnp.maximum(m, mx)
            a_buf[...] = jnp.exp2(m - mn)
        p = jnp.exp2(s - mn)
        ps = jnp.sum(p, axis=0, keepdims=True)
        l_buf[...] = ps if first else a_buf[...] * l_buf[...] + ps
        m_buf[...] = mn
        p_buf[0:n * TK, :] = p.astype(BF16)

    def accumulate(chunk, p_buf, first):
        if first:
            acc_buf[...] = pv(chunk, p_buf)
        else:
            acc_buf[...] = a_buf[...] * acc_buf[...] + pv(chunk, p_buf)

    def pipeline(chs):
        scores(chs[0], s_bufs[0])
        for c, ch in enumerate(chs):
            if c >= 2:
                accumulate(chs[c - 1], p_bufs[(c - 1) % 2], first=False)
            if c + 1 < len(chs):
                scores(chs[c + 1], s_bufs[(c + 1) % 2])
            softmax(ch, s_bufs[c % 2], p_bufs[c % 2], first=c == 0)
            if c == 1:
                accumulate(chs[0], p_bufs[0], first=True)
        last = len(chs) - 1
        if last == 0:
            accumulate(chs[0], p_bufs[0], first=True)
        else:
            accumulate(chs[last], p_bufs[last % 2], first=False)

    @pl.when(i < n_lat_q)
    def _():
        pipeline(chunks)

    @pl.when(i >= n_lat_q)
    def _():
        pipeline(chunks[:1])

    o2 = acc_buf[...] * (1.0 / l_buf[...])
    oa, ob = o2[:, :tq], o2[:, tq:]
    if mode == "diff":
        lq1_ref, lk1_ref, lq2_ref, lk2_ref, sub_ref = refs[3:8]
        lam = (jnp.exp(jnp.sum(lq1_ref[...] * lk1_ref[...], axis=1, keepdims=True))
               - jnp.exp(jnp.sum(lq2_ref[...] * lk2_ref[...], axis=1, keepdims=True)) + lambda_init)
        o = (oa - lam * ob).T
        o = _rms(o, sub_ref[...], DIFF_EPS) * (1.0 - lambda_init)
    else:
        vrow = lax.broadcasted_iota(jnp.int32, (LANES, 1), 0)
        o = jnp.where(vrow < 64, oa, ob).T
    o_ref[0] = o.astype(BF16)


def _dense_attn_call(mode, q, k, vt, extra=(), lambda_init=0.0):
    bsz, t, qtot = q.shape
    width = 256 if mode == "mla" else 128
    npairs = qtot // width
    nq = pl.cdiv(t, TQ)
    ins = [q, k, vt]
    specs = [pl.BlockSpec((1, TQ, width), lambda b, j, i: (b, i, j)),
             pl.BlockSpec((1, t, width), lambda b, j, i: (b, 0, j)),
             pl.BlockSpec((1, t // TK, LANES, TK), lambda b, j, i: (b, 0, j, 0))]
    for e in extra:
        ins.append(e)
        specs.append(pl.BlockSpec(e.shape, lambda b, j, i: (0, 0)))
    return pl.pallas_call(
        functools.partial(_dense_attn_kernel, mode, lambda_init, t // TK - 1),
        grid=(bsz, npairs, nq),
        in_specs=specs,
        out_specs=pl.BlockSpec((1, TQ, LANES), lambda b, j, i: (b, i, j)),
        out_shape=jax.ShapeDtypeStruct((bsz, t, npairs * LANES), BF16),
        scratch_shapes=[pltpu.VMEM((2 * TK, 2 * TQ), F32), pltpu.VMEM((2 * TK, 2 * TQ), F32),
                        pltpu.VMEM((2 * TK, 2 * TQ), BF16), pltpu.VMEM((2 * TK, 2 * TQ), BF16),
                        pltpu.VMEM((LANES, 2 * TQ), F32), pltpu.VMEM((1, 2 * TQ), F32),
                        pltpu.VMEM((1, 2 * TQ), F32), pltpu.VMEM((1, 2 * TQ), F32)],
        compiler_params=_cparams(3),
        name=mode + "_attn",
    )(*ins)


def _swa_kernel(nlat, sink_ref, q_ref, k_ref, vt_ref, o_ref):
    g = pl.program_id(1)
    i = pl.program_id(2)
    s_len = nlat * TM
    half = TM // 2
    q = q_ref[0]
    row = lax.broadcasted_iota(jnp.int32, (LANES, 1), 0)
    in_a, in_b = _pair_masks("pair", row)
    cols = []
    for pr in range(2):
        qt = q[:, pr * LANES:(pr + 1) * LANES].T
        zero = jnp.zeros_like(qt)
        cols += [jnp.where(in_a, qt, zero), jnp.where(in_b, qt, zero)]
    q4 = jnp.concatenate(cols, axis=1)

    start0 = pl.multiple_of(jnp.maximum(i * TM - half, 0), half)
    start1 = pl.multiple_of(i * TM, TM)
    start2 = pl.multiple_of(jnp.minimum((i + 1) * TM, s_len + half), half)
    k_cat = jnp.concatenate([k_ref[0, pl.ds(start0, half), :], k_ref[0, pl.ds(start1, TM), :],
                             k_ref[0, pl.ds(start2, half), :], k_ref[0, s_len:s_len + TM, :]], axis=0)
    b0 = jnp.maximum(i - 1, 0)
    b2 = jnp.minimum(i + 1, nlat)
    vt_cat = jnp.concatenate([vt_ref[0, b0][:, half:], vt_ref[0, i], vt_ref[0, b2][:, :half],
                              vt_ref[0, nlat]], axis=1)

    r = lax.broadcasted_iota(jnp.int32, (2 * TM, 1), 0)
    far = -4 * SWA_WINDOW
    lat = i < nlat
    pos0 = jnp.where(lat & (i >= 1), start0 + r, far)
    pos1 = jnp.where(lat, start1 + r - half, far)
    pos2 = jnp.where(lat & (i + 1 < nlat), start2 + r - half - TM, far)
    kpos = jnp.where(r < half, pos0, jnp.where(r < half + TM, pos1, pos2))
    qpos = i * TM + lax.broadcasted_iota(jnp.int32, (1, TM), 1)
    band = jnp.abs(qpos - kpos) <= SWA_WINDOW
    band4 = jnp.concatenate([band] * 4, axis=1)

    s = jnp.dot(k_cat, q4, preferred_element_type=F32)
    s_loc = jnp.where(band4, s[:2 * TM], NEG)
    s_ctx = s[2 * TM:]
    sink = jnp.concatenate([jnp.full((1, TM), sink_ref[4 * g + h], F32) for h in range(4)], axis=1) * LOG2E
    m = jnp.maximum(jnp.maximum(jnp.max(s_loc, axis=0, keepdims=True), jnp.max(s_ctx, axis=0, keepdims=True)), sink)
    p_loc = jnp.exp2(s_loc - m)
    p_ctx = jnp.exp2(s_ctx - m)
    l = jnp.sum(p_loc, axis=0, keepdims=True) + jnp.sum(p_ctx, axis=0, keepdims=True) + jnp.exp2(sink - m)
    p = jnp.concatenate([p_loc, p_ctx], axis=0).astype(BF16)
    o4 = jnp.dot(vt_cat, p, preferred_element_type=F32) * (1.0 / l)
    vrow = lax.broadcasted_iota(jnp.int32, (LANES, 1), 0)
    for pr in range(2):
        oa = o4[:, (2 * pr) * TM:(2 * pr + 1) * TM]
        ob = o4[:, (2 * pr + 1) * TM:(2 * pr + 2) * TM]
        o_ref[0, :, pr * LANES:(pr + 1) * LANES] = jnp.where(vrow < 64, oa, ob).T.astype(BF16)


def _swa_call(sink, q, k, vt):
    bsz, t, qtot = q.shape
    nkv = k.shape[-1] // LANES
    nt = t // TM
    grid_spec = pltpu.PrefetchScalarGridSpec(
        num_scalar_prefetch=1,
        grid=(bsz, nkv, nt),
        in_specs=[pl.BlockSpec((1, TM, 2 * LANES), lambda b, g, i, s: (b, i, g)),
                  pl.BlockSpec((1, t, LANES), lambda b, g, i, s: (b, 0, g)),
                  pl.BlockSpec((1, nt, LANES, TM), lambda b, g, i, s: (b, 0, g, 0))],
        out_specs=pl.BlockSpec((1, TM, 2 * LANES), lambda b, g, i, s: (b, i, g)),
    )
    return pl.pallas_call(
        functools.partial(_swa_kernel, nt - 1),
        grid_spec=grid_spec,
        out_shape=jax.ShapeDtypeStruct((bsz, t, qtot), BF16),
        compiler_params=_cparams(3),
        name="swa_attn",
    )(sink, q, k, vt)


def _out_proj_kernel(n_o, split, nlat, *refs):
    o_refs = refs[:n_o]
    w_refs = refs[n_o:2 * n_o]
    refs = refs[2 * n_o:]
    if split:
        x_in = _first_layer_x(refs[0], refs[1], nlat)
        refs = refs[2:]
    else:
        x_in = refs[0][0]
        refs = refs[1:]
    mod_ref, g_ref, wr_ref, br_ref, xo_ref, h_ref, lg_ref = refs
    acc = jnp.dot(o_refs[0][0], w_refs[0][...], preferred_element_type=F32)
    for n in range(1, n_o):
        acc = acc + jnp.dot(o_refs[n][0], w_refs[n][...], preferred_element_type=F32)
    m = mod_ref[0, 0]
    x = x_in + m[2:3, :] * acc
    xo_ref[0] = x
    h = _rms(x, g_ref[...], NORM_EPS) * (1.0 + m[4:5, :]) + m[3:4, :]
    hi = h.astype(BF16)
    lo = (h - hi.astype(F32)).astype(BF16)
    h_ref[0] = h
    both = jnp.dot(hi, wr_ref[...], preferred_element_type=F32)
    lg_ref[0] = (both[:, :LANES] + both[:, LANES:]
                 + jnp.dot(lo, wr_ref[:, :LANES], preferred_element_type=F32) + br_ref[...])


def _out_proj_call(os_, ws, x, mod, g, wr, br):
    split = isinstance(x, (tuple, list))
    if split:
        bsz, t, d = x[0].shape[0], x[0].shape[1] + x[1].shape[1], x[0].shape[2]
    else:
        bsz, t, d = x.shape
    nt = t // TM
    nlat = nt - 1
    n_o = len(os_)
    x_ins = list(x) if split else [x]
    x_specs = _split_specs(nlat, d) if split else [_row_spec(d)]
    ins = list(os_) + list(ws) + x_ins + [mod, g.reshape(1, d), wr, br]
    specs = ([_row_spec(o.shape[-1]) for o in os_] + [_full_spec(w.shape) for w in ws]
             + x_specs + [_mod_spec(nlat), _full_spec((1, d)), _full_spec(wr.shape), _full_spec(br.shape)])
    return pl.pallas_call(
        functools.partial(_out_proj_kernel, n_o, split, nlat),
        grid=(bsz, nt),
        in_specs=specs,
        out_specs=[_row_spec(d), _row_spec(d), _row_spec(LANES)],
        out_shape=[jax.ShapeDtypeStruct((bsz, t, d), F32),
                   jax.ShapeDtypeStruct((bsz, t, d), F32),
                   jax.ShapeDtypeStruct((bsz, t, LANES), F32)],
        compiler_params=_cparams(2),
        name="out_proj",
    )(*ins)


def _moe_kernel(be_ref, nu_ref, x_ref, w1_ref, w3_ref, w2_ref, o_ref, w1c, w3c, w2c):
    i = pl.program_id(0)
    e = be_ref[i]

    @pl.when((i == 0) | (e != be_ref[jnp.maximum(i - 1, 0)]))
    def _():
        w1c[...] = w1_ref[0].astype(BF16)
        w3c[...] = w3_ref[0].astype(BF16)
        w2c[...] = w2_ref[0].astype(BF16)

    @pl.when(i < nu_ref[0])
    def _():
        x = x_ref[...].astype(BF16)
        a = jnp.dot(x, w1c[...], preferred_element_type=F32)
        b = jnp.dot(x, w3c[...], preferred_element_type=F32)
        hmid = (a * jax.nn.sigmoid(a)) * b
        o_ref[...] = jnp.dot(hmid.astype(BF16), w2c[...], preferred_element_type=F32)

    @pl.when(i >= nu_ref[0])
    def _():
        o_ref[...] = jnp.zeros_like(o_ref)


def _moe_call(block_e, n_used, xb, w1, w3, w2):
    n_slots, d = xb.shape
    n_blocks = n_slots // TMOE
    ff = w1.shape[-1]
    grid_spec = pltpu.PrefetchScalarGridSpec(
        num_scalar_prefetch=2,
        grid=(n_blocks,),
        in_specs=[pl.BlockSpec((TMOE, d), lambda i, be, nu: (i, 0)),
                  pl.BlockSpec((1, d, ff), lambda i, be, nu: (be[i], 0, 0)),
                  pl.BlockSpec((1, d, ff), lambda i, be, nu: (be[i], 0, 0)),
                  pl.BlockSpec((1, ff, d), lambda i, be, nu: (be[i], 0, 0))],
        out_specs=pl.BlockSpec((TMOE, d), lambda i, be, nu: (i, 0)),
        scratch_shapes=[pltpu.VMEM((d, ff), BF16), pltpu.VMEM((d, ff), BF16), pltpu.VMEM((ff, d), BF16)],
    )
    return pl.pallas_call(
        _moe_kernel,
        grid_spec=grid_spec,
        out_shape=jax.ShapeDtypeStruct((n_slots, d), F32),
        compiler_params=_cparams(1),
        name="moe_experts",
    )(block_e, n_used, xb, w1, w3, w2)


def _route_and_dispatch(logits):
    n_tok = logits.shape[0]
    g_logits = logits[:, :N_GROUPS]
    g_prob = jax.nn.softmax(g_logits, axis=-1)
    grp = jnp.argmax(g_logits, axis=-1).astype(jnp.int32)
    p_grp = jnp.take_along_axis(g_prob, grp[:, None], axis=-1)
    e_logits = logits[:, N_GROUPS:N_GROUPS + N_EXPERTS].reshape(n_tok, N_GROUPS, EXPERTS_PER_GROUP)
    e_logits = jnp.take_along_axis(e_logits, grp[:, None, None], axis=1)[:, 0]
    top_p, top_i = lax.top_k(jax.nn.softmax(e_logits, axis=-1), TOP_K)
    gate = p_grp * top_p / jnp.sum(top_p, axis=-1, keepdims=True)
    eid = grp[:, None] * EXPERTS_PER_GROUP + top_i.astype(jnp.int32)

    onehot = jnp.sum((eid[:, :, None] == jnp.arange(N_EXPERTS, dtype=jnp.int32)).astype(jnp.int32), axis=1)
    cum = jnp.cumsum(onehot, axis=0) - onehot
    counts = jnp.sum(onehot, axis=0)
    rank = jnp.take_along_axis(cum, eid, axis=1)
    padded = (counts + TMOE - 1) // TMOE * TMOE
    pad_end = jnp.cumsum(padded)
    pad_start = pad_end - padded
    dest = pad_start[eid] + rank
    n_assign = n_tok * TOP_K
    n_blocks = (n_assign + N_EXPERTS * (TMOE - 1) + TMOE - 1) // TMOE
    n_slots = n_blocks * TMOE
    flat_tok = jnp.repeat(jnp.arange(n_tok, dtype=jnp.int32), TOP_K)
    slot_tok = jnp.zeros((n_slots,), jnp.int32).at[dest.reshape(n_assign)].set(flat_tok)
    block_start = jnp.arange(n_blocks, dtype=jnp.int32) * TMOE
    block_e = jnp.minimum(jnp.sum((block_start[:, None] >= pad_end[None, :]).astype(jnp.int32), axis=1),
                          N_EXPERTS - 1).astype(jnp.int32)
    n_used = (pad_end[-1:] // TMOE).astype(jnp.int32)
    return slot_tok, gate, block_e, n_used, dest


def _final_kernel(x_ref, y_ref, pmod_ref, g_ref, o_ref):
    x = x_ref[0] + pmod_ref[0, 0][5:6, :] * y_ref[0]
    o_ref[0] = _rms(x, g_ref[...], NORM_EPS)


def _final_call(x, y, pmod, g, s_len):
    bsz, t, d = x.shape
    return pl.pallas_call(
        _final_kernel,
        grid=(bsz, s_len // TM),
        in_specs=[_row_spec(d), _row_spec(d),
                  pl.BlockSpec((1, 1, 8, d), lambda b, i: (b, 0, 0, 0)),
                  _full_spec((1, d))],
        out_specs=_row_spec(d),
        out_shape=jax.ShapeDtypeStruct((bsz, s_len, d), F32),
        compiler_params=_cparams(2),
        name="final_norm",
    )(x, y, pmod, g.reshape(1, d))


def _take_cols(w, idx):
    wz = jnp.concatenate([w, jnp.zeros((w.shape[0], 1), w.dtype)], axis=1)
    return jnp.take(wz, jnp.asarray(idx, dtype=jnp.int32), axis=1).astype(BF16)


def _ab_layouts():
    zc = 1184
    cols = []
    for j in range(4):
        cols.append(j * 128 + _PAIR_PERM)
    for g in range(2):
        base = 512 + g * 64
        cols.append(base + np.concatenate([np.arange(0, 32), np.arange(0, 32), np.arange(32, 64), np.arange(32, 64)]))
    for g in range(2):
        base = 640 + g * 64
        cols.append(base + np.concatenate([np.arange(64), np.arange(64)]))
    cols.append(768 + np.arange(256))
    cols.append(1024 + np.arange(128))
    kr = 1152
    z32 = np.full((32,), zc)
    cols.append(np.concatenate([kr + np.arange(16), kr + np.arange(16), z32,
                                kr + 16 + np.arange(16), kr + 16 + np.arange(16), z32]))
    w1_idx = np.concatenate(cols)

    zq = 768
    uq = []
    z32q = np.full((32,), zq)
    for j in range(4):
        a, b = 2 * j * 96, (2 * j + 1) * 96
        uq.append(np.concatenate([a + np.arange(64), b + np.arange(64),
                                  a + 64 + np.arange(16), b + 64 + np.arange(16), z32q,
                                  a + 80 + np.arange(16), b + 80 + np.arange(16), z32q]))
    uq_idx = np.concatenate(uq)

    kn, mv = [], []
    for h in range(8):
        kn.append(h * 128 + np.arange(64))
        mv.append(h * 128 + 64 + np.arange(64))
    ukv_idx = np.concatenate(kn + mv)
    return w1_idx, uq_idx, ukv_idx


def _diff_layout():
    cols = []
    for part in range(2):
        for h in range(8):
            cols.append(part * 1024 + h * 128 + _PAIR_PERM)
    cols.append(2048 + np.arange(1024))
    return np.concatenate(cols)


def kernel(x, c, ctx, c_ctx, norm_mix, norm_ffn, ada_w, ada_b, ab_w_in, mla_q_norm, mla_w_uq, mla_kv_norm, mla_w_ukv, swa_sink, ab_w_out, diff_w_in, diff_lambda_q1, diff_lambda_k1, diff_lambda_q2, diff_lambda_k2, diff_subln, diff_w_out, router_group_w, router_group_b, router_expert_w, router_expert_b, expert_w1, expert_w3, expert_w2, final_norm):
    bsz, s_len, d = x.shape
    c_len = ctx.shape[1]
    depth = ada_w.shape[0]
    assert d == D_MODEL and c_len == TM and s_len % TQ == 0 and s_len % (2 * TK) == 0
    t = s_len + c_len

    xs = (x, ctx)
    tables = _rope_tables(s_len, c_len)
    c64, s64, c32, s32 = tables

    n_rows = (bsz + 1 + 7) // 8 * 8
    rows = jnp.concatenate([c, c_ctx[None, :], jnp.zeros((n_rows - bsz - 1, d), F32)], axis=0)
    mod_all = _ada_call(rows, ada_w, ada_b)
    mod_lat = mod_all[:, :bsz].reshape(depth, bsz, 1, 6, d)
    mod_ctx = jnp.broadcast_to(mod_all[:, bsz].reshape(depth, 1, 1, 6, d), (depth, bsz, 1, 6, d))
    mods = jnp.concatenate([mod_lat, mod_ctx], axis=2)
    mods = jnp.concatenate([mods, jnp.zeros((depth, bsz, 2, 2, d), F32)], axis=3)

    w1_idx, uq_idx, ukv_idx = _ab_layouts()
    diff_idx = _diff_layout()

    res = None
    for l in range(depth):
        j = l // 2
        mod = mods[l]
        if l % 2 == 0:
            weights = [_take_cols(ab_w_in[j], w1_idx), mla_q_norm[j].reshape(1, -1), mla_kv_norm[j].reshape(1, -1),
                       _take_cols(mla_w_uq[j], uq_idx), _take_cols(mla_w_ukv[j], ukv_idx)]
            xs, (sq, sk, sv, mq, mk, mv) = _proj_call("ab", xs, res, mod, norm_mix[l], weights,
                                                      [c64, s64, c32, s32], [512, 256, -256, 1024, 1024, -512])
            o_a = _swa_call(swa_sink[j], sq, sk, sv)
            o_b = _dense_attn_call("mla", mq, mk, mv)
            w_out = ab_w_out[j].astype(BF16)
            attn_outs, out_ws = [o_a, o_b], [w_out[:512], w_out[512:]]
        else:
            lambda_init = 0.8 - 0.6 * math.exp(-0.3 * l)
            weights = [_take_cols(diff_w_in[j], diff_idx)]
            xs, (dq, dk, dv) = _proj_call("diff", xs, res, mod, norm_mix[l], weights, [c64, s64], [1024, 1024, -1024])
            extra = [diff_lambda_q1[j].reshape(1, -1), diff_lambda_k1[j].reshape(1, -1),
                     diff_lambda_q2[j].reshape(1, -1), diff_lambda_k2[j].reshape(1, -1),
                     diff_subln[j].reshape(1, -1)]
            o_d = _dense_attn_call("diff", dq, dk, dv, extra=extra, lambda_init=lambda_init)
            attn_outs, out_ws = [o_d], [diff_w_out[j].astype(BF16)]

        wr = jnp.concatenate([router_group_w[l], router_expert_w[l],
                              jnp.zeros((d, LANES - N_GROUPS - N_EXPERTS), F32)], axis=1)
        wr_hi = wr.astype(BF16)
        wr = jnp.concatenate([wr_hi, (wr - wr_hi.astype(F32)).astype(BF16)], axis=1)
        br = jnp.concatenate([router_group_b[l], router_expert_b[l],
                              jnp.zeros((LANES - N_GROUPS - N_EXPERTS,), F32)]).reshape(1, LANES)
        xs, h2, logits = _out_proj_call(attn_outs, out_ws, xs, mod, norm_ffn[l], wr, br)

        n_tok = bsz * t
        slot_tok, gate, block_e, n_used, dest = _route_and_dispatch(logits.reshape(n_tok, LANES))
        xb = jnp.take(h2.reshape(n_tok, d), slot_tok, axis=0)
        yb = _moe_call(block_e, n_used, xb, expert_w1[l], expert_w3[l], expert_w2[l])
        y = (gate[:, 0:1] * jnp.take(yb, dest[:, 0], axis=0)
             + gate[:, 1:2] * jnp.take(yb, dest[:, 1], axis=0)).reshape(bsz, t, d)
        res = (y, mod)

    return _final_call(xs, res[0], res[1], final_norm, s_len)
```

```python
import functools
import math

import numpy as np
import jax
import jax.numpy as jnp
from jax import lax
from jax.experimental import pallas as pl
from jax.experimental.pallas import tpu as pltpu

F32 = jnp.float32
BF16 = jnp.bfloat16

D_MODEL = 1024
GRID_W = 64
ROPE_BASE = 10000.0
NORM_EPS = 1e-6
DIFF_EPS = 1e-5
NEG = -1e30

SWA_WINDOW = 128
MLA_SCALE = (64 + 32) ** -0.5
HEAD_SCALE = 64 ** -0.5
LOG2E = math.log2(math.e)

N_GROUPS = 4
EXPERTS_PER_GROUP = 8
N_EXPERTS = 32
TOP_K = 2
EXPERT_FF = 512

LANES = 128
TM = 256
TK = 256
TQ = 512
TMOE = 256
VMEM_LIMIT = 56 * 1024 * 1024


def _cparams(n_axes):
    return pltpu.CompilerParams(dimension_semantics=("arbitrary",) * n_axes,
                                vmem_limit_bytes=VMEM_LIMIT)


def _rms(x, g, eps):
    return x * lax.rsqrt(jnp.mean(x * x, axis=-1, keepdims=True) + eps) * g


def _rope_block(x, c, s):
    return x * c + pltpu.roll(x, 64, 1) * s


_PAIR_PERM = np.concatenate([np.arange(0, 32), np.arange(64, 96), np.arange(32, 64), np.arange(96, 128)])


def _rope_tables(s_len, c_len):
    rows = s_len // GRID_W
    row = jnp.repeat(jnp.arange(rows, dtype=F32), GRID_W)
    col = jnp.tile(jnp.arange(GRID_W, dtype=F32), rows)

    def tab(dim):
        nf = dim // 4
        inv = ROPE_BASE ** (-jnp.arange(nf, dtype=F32) / nf)
        ang = jnp.concatenate([row[:, None] * inv, col[:, None] * inv], axis=-1)
        return jnp.cos(ang), jnp.sin(ang)

    cos64, sin64 = tab(64)
    cos32, sin32 = tab(32)
    c64 = jnp.concatenate([cos64] * 4, axis=-1)
    s64 = jnp.concatenate([-sin64, -sin64, sin64, sin64], axis=-1)
    one = jnp.ones((s_len, 32), F32)
    zero = jnp.zeros((s_len, 32), F32)
    c32 = jnp.concatenate([cos32, cos32, one, cos32, cos32, one], axis=-1)
    s32 = jnp.concatenate([-sin32, -sin32, zero, sin32, sin32, zero], axis=-1)

    def ext(t, fill):
        return jnp.concatenate([t, jnp.full((c_len, LANES), fill, F32)], axis=0)

    return ext(c64, 1.0), ext(s64, 0.0), ext(c32, 1.0), ext(s32, 0.0)


def _ada_kernel(x_ref, w_ref, b_ref, o_ref):
    x = x_ref[...]
    sx = x * jax.nn.sigmoid(x)
    o_ref[0] = jnp.dot(sx.astype(BF16), w_ref[0].astype(BF16), preferred_element_type=F32) + b_ref[0]


def _ada_call(rows, ada_w, ada_b):
    depth, d, n6 = ada_w.shape
    r = rows.shape[0]
    tn = 1536
    return pl.pallas_call(
        _ada_kernel,
        grid=(depth, n6 // tn),
        in_specs=[pl.BlockSpec((r, d), lambda l, j: (0, 0)),
                  pl.BlockSpec((1, d, tn), lambda l, j: (l, 0, j)),
                  pl.BlockSpec((1, 1, tn), lambda l, j: (l, 0, j))],
        out_specs=pl.BlockSpec((1, r, tn), lambda l, j: (l, 0, j)),
        out_shape=jax.ShapeDtypeStruct((depth, r, n6), F32),
        compiler_params=_cparams(2),
        name="ada_mod",
    )(rows, ada_w, ada_b.reshape(depth, 1, n6))


def _first_layer_x(x_ref, ctx_ref, nlat):
    return jnp.where(pl.program_id(1) < nlat, x_ref[0], ctx_ref[0])


def _prenorm(has_res, nlat, x_ref, y_ref, pmod_ref, mod_ref, g_ref, xo_ref):
    if has_res:
        x = x_ref[0] + pmod_ref[0, 0][5:6, :] * y_ref[0]
        xo_ref[0] = x
    else:
        x = _first_layer_x(x_ref, y_ref, nlat)
    m = mod_ref[0, 0]
    return _rms(x, g_ref[...], NORM_EPS) * (1.0 + m[1:2, :]) + m[0:1, :]


def _ab_proj_kernel(has_res, nlat, *refs):
    if has_res:
        x_ref, y_ref, pmod_ref = refs[:3]
        refs = refs[3:]
    else:
        x_ref, y_ref, pmod_ref = refs[0], refs[1], None
        refs = refs[2:]
    (mod_ref, g_ref, w1_ref, qn_ref, kvn_ref, wuq_ref, wukv_ref,
     c64_ref, s64_ref, c32_ref, s32_ref) = refs[:11]
    outs = refs[11:]
    if has_res:
        xo_ref, outs = outs[0], outs[1:]
    else:
        xo_ref = None
    sq_ref, sk_ref, sv_ref, mq_ref, mk_ref, mv_ref = outs

    h = _prenorm(has_res, nlat, x_ref, y_ref, pmod_ref, mod_ref, g_ref, xo_ref)
    p = jnp.dot(h.astype(BF16), w1_ref[...], preferred_element_type=F32)
    c64, s64, c32, s32 = c64_ref[...], s64_ref[...], c32_ref[...], s32_ref[...]
    for j in range(4):
        blk = _rope_block(p[:, j * 128:(j + 1) * 128], c64, s64)
        sq_ref[0, :, j * 128:(j + 1) * 128] = (blk * (HEAD_SCALE * LOG2E)).astype(BF16)
    for j in range(2):
        blk = _rope_block(p[:, 512 + j * 128:512 + (j + 1) * 128], c64, s64)
        sk_ref[0, :, j * 128:(j + 1) * 128] = blk.astype(BF16)
    sv_ref[0, 0] = p[:, 768:1024].T.astype(BF16)
    cq = p[:, 1024:1280]
    ckv = p[:, 1280:1408]
    kr = _rope_block(p[:, 1408:1536], c32, s32).astype(BF16)
    qm = jnp.dot(_rms(cq, qn_ref[...], NORM_EPS).astype(BF16), wuq_ref[...], preferred_element_type=F32)
    kv = jnp.dot(_rms(ckv, kvn_ref[...], NORM_EPS).astype(BF16), wukv_ref[...], preferred_element_type=F32)
    for j in range(4):
        mq_ref[0, :, j * 256:j * 256 + 128] = (qm[:, j * 256:j * 256 + 128] * (MLA_SCALE * LOG2E)).astype(BF16)
        rr = _rope_block(qm[:, j * 256 + 128:(j + 1) * 256], c32, s32)
        mq_ref[0, :, j * 256 + 128:(j + 1) * 256] = (rr * (MLA_SCALE * LOG2E)).astype(BF16)
        mk_ref[0, :, j * 256:j * 256 + 128] = kv[:, j * 128:(j + 1) * 128].astype(BF16)
        mk_ref[0, :, j * 256 + 128:(j + 1) * 256] = kr
    mv_ref[0, 0] = kv[:, 512:1024].T.astype(BF16)


def _diff_proj_kernel(has_res, nlat, *refs):
    if has_res:
        x_ref, y_ref, pmod_ref = refs[:3]
        refs = refs[3:]
    else:
        x_ref, y_ref, pmod_ref = refs[0], refs[1], None
        refs = refs[2:]
    mod_ref, g_ref, w_ref, c64_ref, s64_ref = refs[:5]
    outs = refs[5:]
    if has_res:
        xo_ref, outs = outs[0], outs[1:]
    else:
        xo_ref = None
    q_ref, k_ref, v_ref = outs
    h = _prenorm(has_res, nlat, x_ref, y_ref, pmod_ref, mod_ref, g_ref, xo_ref)
    p = jnp.dot(h.astype(BF16), w_ref[...], preferred_element_type=F32)
    c64, s64 = c64_ref[...], s64_ref[...]
    for j in range(8):
        blk = _rope_block(p[:, j * 128:(j + 1) * 128], c64, s64)
        q_ref[0, :, j * 128:(j + 1) * 128] = (blk * (HEAD_SCALE * LOG2E)).astype(BF16)
        blk = _rope_block(p[:, 1024 + j * 128:1024 + (j + 1) * 128], c64, s64)
        k_ref[0, :, j * 128:(j + 1) * 128] = blk.astype(BF16)
    v_ref[0, 0] = p[:, 2048:3072].T.astype(BF16)


def _row_spec(width):
    return pl.BlockSpec((1, TM, width), lambda b, i: (b, i, 0))


def _mod_spec(nlat):
    return pl.BlockSpec((1, 1, 8, D_MODEL), lambda b, i: (b, i // nlat, 0, 0))


def _full_spec(shape):
    nd = len(shape)
    return pl.BlockSpec(shape, lambda b, i: (0,) * nd)


def _tab_spec():
    return pl.BlockSpec((TM, LANES), lambda b, i: (i, 0))


def _split_specs(nlat, d):
    return [pl.BlockSpec((1, TM, d), lambda b, i: (b, jnp.minimum(i, nlat - 1), 0)),
            pl.BlockSpec((1, TM, d), lambda b, i: (b, 0, 0))]


def _proj_call(kind, x, res, mod, g, weights, tables, out_widths):
    has_res = res is not None
    if has_res:
        bsz, t, d = x.shape
    else:
        bsz, t, d = x[0].shape[0], x[0].shape[1] + x[1].shape[1], x[0].shape[2]
    nt = t // TM
    nlat = nt - 1
    if has_res:
        y, pmod = res
        ins = [x, y, pmod]
        specs = [_row_spec(d), _row_spec(d), _mod_spec(nlat)]
    else:
        ins = list(x)
        specs = _split_specs(nlat, d)
    ins += [mod, g.reshape(1, d)]
    specs += [_mod_spec(nlat), _full_spec((1, d))]
    for w in weights:
        ins.append(w)
        specs.append(_full_spec(w.shape))
    for tb in tables:
        ins.append(tb)
        specs.append(_tab_spec())
    out_shapes, out_specs = [], []
    if has_res:
        out_shapes.append(jax.ShapeDtypeStruct((bsz, t, d), F32))
        out_specs.append(_row_spec(d))
    for w in out_widths:
        if w < 0:
            out_shapes.append(jax.ShapeDtypeStruct((bsz, nt, -w, TM), BF16))
            out_specs.append(pl.BlockSpec((1, 1, -w, TM), lambda b, i: (b, i, 0, 0)))
        else:
            out_shapes.append(jax.ShapeDtypeStruct((bsz, t, w), BF16))
            out_specs.append(_row_spec(w))
    body = _ab_proj_kernel if kind == "ab" else _diff_proj_kernel
    outs = pl.pallas_call(
        functools.partial(body, has_res, nlat),
        grid=(bsz, nt),
        in_specs=specs,
        out_specs=out_specs,
        out_shape=out_shapes,
        compiler_params=_cparams(2),
        name=kind + "_proj",
    )(*ins)
    if has_res:
        return outs[0], outs[1:]
    return x, outs


def _pair_masks(mode, lane):
    if mode == "mla":
        in_a = (lane < 64) | ((lane >= 128) & (lane < 144)) | ((lane >= 192) & (lane < 208))
        in_b = ((lane >= 64) & (lane < 128)) | ((lane >= 144) & (lane < 160)) | ((lane >= 208) & (lane < 224))
    else:
        in_a = (lane < 32) | ((lane >= 64) & (lane < 96))
        in_b = ((lane >= 32) & (lane < 64)) | (lane >= 96)
    return in_a, in_b


def _dense_attn_kernel(mode, lambda_init, nlat, *refs):
    n_in = 8 if mode == "diff" else 3
    q_ref, k_ref, vt_ref = refs[:3]
    o_ref = refs[n_in]
    s_bufs = refs[n_in + 1:n_in + 3]
    p_bufs = refs[n_in + 3:n_in + 5]
    acc_buf, m_buf, l_buf, a_buf = refs[n_in + 5:]
    i = pl.program_id(2)
    qt = q_ref[0].T
    width, tq = qt.shape
    row = lax.broadcasted_iota(jnp.int32, (width, 1), 0)
    in_a, in_b = _pair_masks(mode, row)
    zero = jnp.zeros_like(qt)
    q2 = jnp.concatenate([jnp.where(in_a, qt, zero), jnp.where(in_b, qt, zero)], axis=1)
    chunks = [(nlat, 1)] + [(2 * c, 2) for c in range(nlat // 2)]
    n_lat_q = (nlat * TK) // tq

    def scores(chunk, s_buf):
        k0, n = chunk
        s_buf[0:n * TK, :] = jnp.dot(k_ref[0, k0 * TK:(k0 + n) * TK, :], q2, preferred_element_type=F32)

    def pv(chunk, p_buf):
        k0, n = chunk
        out = jnp.dot(vt_ref[0, k0], p_buf[0:TK, :], preferred_element_type=F32)
        for r in range(1, n):
            out = out + jnp.dot(vt_ref[0, k0 + r], p_buf[r * TK:(r + 1) * TK, :], preferred_element_type=F32)
        return out

    def softmax(chunk, s_buf, p_buf, first):
        n = chunk[1]
        s = s_buf[0:n * TK, :]
        mx = jnp.max(s, axis=0, keepdims=True)
        if first:
            mn = mx
        else:
            m = m_buf[...]
            mn = jnp.maximum(m, mx)
            a_buf[...] = jnp.exp2(m - mn)
        p = jnp.exp2(s - mn)
        ps = jnp.sum(p, axis=0, keepdims=True)
        l_buf[...] = ps if first else a_buf[...] * l_buf[...] + ps
        m_buf[...] = mn
        p_buf[0:n * TK, :] = p.astype(BF16)

    def accumulate(chunk, p_buf, first):
        if first:
            acc_buf[...] = pv(chunk, p_buf)
        else:
            acc_buf[...] = a_buf[...] * acc_buf[...] + pv(chunk, p_buf)

    def pipeline(chs):
        scores(chs[0], s_bufs[0])
        for c, ch in enumerate(chs):
            if c >= 2:
                accumulate(chs[c - 1], p_bufs[(c - 1) % 2], first=False)
            if c + 1 < len(chs):
                scores(chs[c + 1], s_bufs[(c + 1) % 2])
            softmax(ch, s_bufs[c % 2], p_bufs[c % 2], first=c == 0)
            if c == 1:
                accumulate(chs[0], p_bufs[0], first=True)
        last = len(chs) - 1
        if last == 0:
            accumulate(chs[0], p_bufs[0], first=True)
        else:
            accumulate(chs[last], p_bufs[last % 2], first=False)

    @pl.when(i < n_lat_q)
    def _():
        pipeline(chunks)

    @pl.when(i >= n_lat_q)
    def _():
        pipeline(chunks[:1])

    o2 = acc_buf[...] * (1.0 / l_buf[...])
    oa, ob = o2[:, :tq], o2[:, tq:]
    if mode == "diff":
        lq1_ref, lk1_ref, lq2_ref, lk2_ref, sub_ref = refs[3:8]
        lam = (jnp.exp(jnp.sum(lq1_ref[...] * lk1_ref[...], axis=1, keepdims=True))
               - jnp.exp(jnp.sum(lq2_ref[...] * lk2_ref[...], axis=1, keepdims=True)) + lambda_init)
        o = (oa - lam * ob).T
        o = _rms(o, sub_ref[...], DIFF_EPS) * (1.0 - lambda_init)
    else:
        vrow = lax.broadcasted_iota(jnp.int32, (LANES, 1), 0)
        o = jnp.where(vrow < 64, oa, ob).T
    o_ref[0] = o.astype(BF16)


def _dense_attn_call(mode, q, k, vt, extra=(), lambda_init=0.0):
    bsz, t, qtot = q.shape
    width = 256 if mode == "mla" else 128
    npairs = qtot // width
    nq = pl.cdiv(t, TQ)
    ins = [q, k, vt]
    specs = [pl.BlockSpec((1, TQ, width), lambda b, j, i: (b, i, j)),
             pl.BlockSpec((1, t, width), lambda b, j, i: (b, 0, j)),
             pl.BlockSpec((1, t // TK, LANES, TK), lambda b, j, i: (b, 0, j, 0))]
    for e in extra:
        ins.append(e)
        specs.append(pl.BlockSpec(e.shape, lambda b, j, i: (0, 0)))
    return pl.pallas_call(
        functools.partial(_dense_attn_kernel, mode, lambda_init, t // TK - 1),
        grid=(bsz, npairs, nq),
        in_specs=specs,
        out_specs=pl.BlockSpec((1, TQ, LANES), lambda b, j, i: (b, i, j)),
        out_shape=jax.ShapeDtypeStruct((bsz, t, npairs * LANES), BF16),
        scratch_shapes=[pltpu.VMEM((2 * TK, 2 * TQ), F32), pltpu.VMEM((2 * TK, 2 * TQ), F32),
                        pltpu.VMEM((2 * TK, 2 * TQ), BF16), pltpu.VMEM((2 * TK, 2 * TQ), BF16),
                        pltpu.VMEM((LANES, 2 * TQ), F32), pltpu.VMEM((1, 2 * TQ), F32),
                        pltpu.VMEM((1, 2 * TQ), F32), pltpu.VMEM((1, 2 * TQ), F32)],
        compiler_params=_cparams(3),
        name=mode + "_attn",
    )(*ins)


def _swa_kernel(nlat, sink_ref, q_ref, k_ref, vt_ref, o_ref):
    g = pl.program_id(1)
    i = pl.program_id(2)
    s_len = nlat * TM
    half = TM // 2
    q = q_ref[0]
    row = lax.broadcasted_iota(jnp.int32, (LANES, 1), 0)
    in_a, in_b = _pair_masks("pair", row)
    cols = []
    for pr in range(2):
        qt = q[:, pr * LANES:(pr + 1) * LANES].T
        zero = jnp.zeros_like(qt)
        cols += [jnp.where(in_a, qt, zero), jnp.where(in_b, qt, zero)]
    q4 = jnp.concatenate(cols, axis=1)

    start0 = pl.multiple_of(jnp.maximum(i * TM - half, 0), half)
    start1 = pl.multiple_of(i * TM, TM)
    start2 = pl.multiple_of(jnp.minimum((i + 1) * TM, s_len + half), half)
    k_cat = jnp.concatenate([k_ref[0, pl.ds(start0, half), :], k_ref[0, pl.ds(start1, TM), :],
                             k_ref[0, pl.ds(start2, half), :], k_ref[0, s_len:s_len + TM, :]], axis=0)
    b0 = jnp.maximum(i - 1, 0)
    b2 = jnp.minimum(i + 1, nlat)
    vt_cat = jnp.concatenate([vt_ref[0, b0][:, half:], vt_ref[0, i], vt_ref[0, b2][:, :half],
                              vt_ref[0, nlat]], axis=1)

    r = lax.broadcasted_iota(jnp.int32, (2 * TM, 1), 0)
    far = -4 * SWA_WINDOW
    lat = i < nlat
    pos0 = jnp.where(lat & (i >= 1), start0 + r, far)
    pos1 = jnp.where(lat, start1 + r - half, far)
    pos2 = jnp.where(lat & (i + 1 < nlat), start2 + r - half - TM, far)
    kpos = jnp.where(r < half, pos0, jnp.where(r < half + TM, pos1, pos2))
    qpos = i * TM + lax.broadcasted_iota(jnp.int32, (1, TM), 1)
    band = jnp.abs(qpos - kpos) <= SWA_WINDOW
    band4 = jnp.concatenate([band] * 4, axis=1)

    s = jnp.dot(k_cat, q4, preferred_element_type=F32)
    s_loc = jnp.where(band4, s[:2 * TM], NEG)
    s_ctx = s[2 * TM:]
    sink = jnp.concatenate([jnp.full((1, TM), sink_ref[4 * g + h], F32) for h in range(4)], axis=1) * LOG2E
    m = jnp.maximum(jnp.maximum(jnp.max(s_loc, axis=0, keepdims=True), jnp.max(s_ctx, axis=0, keepdims=True)), sink)
    p_loc = jnp.exp2(s_loc - m)
    p_ctx = jnp.exp2(s_ctx - m)
    l = jnp.sum(p_loc, axis=0, keepdims=True) + jnp.sum(p_ctx, axis=0, keepdims=True) + jnp.exp2(sink - m)
    p = jnp.concatenate([p_loc, p_ctx], axis=0).astype(BF16)
    o4 = jnp.dot(vt_cat, p, preferred_element_type=F32) * (1.0 / l)
    vrow = lax.broadcasted_iota(jnp.int32, (LANES, 1), 0)
    for pr in range(2):
        oa = o4[:, (2 * pr) * TM:(2 * pr + 1) * TM]
        ob = o4[:, (2 * pr + 1) * TM:(2 * pr + 2) * TM]
        o_ref[0, :, pr * LANES:(pr + 1) * LANES] = jnp.where(vrow < 64, oa, ob).T.astype(BF16)


def _swa_call(sink, q, k, vt):
    bsz, t, qtot = q.shape
    nkv = k.shape[-1] // LANES
    nt = t // TM
    grid_spec = pltpu.PrefetchScalarGridSpec(
        num_scalar_prefetch=1,
        grid=(bsz, nkv, nt),
        in_specs=[pl.BlockSpec((1, TM, 2 * LANES), lambda b, g, i, s: (b, i, g)),
                  pl.BlockSpec((1, t, LANES), lambda b, g, i, s: (b, 0, g)),
                  pl.BlockSpec((1, nt, LANES, TM), lambda b, g, i, s: (b, 0, g, 0))],
        out_specs=pl.BlockSpec((1, TM, 2 * LANES), lambda b, g, i, s: (b, i, g)),
    )
    return pl.pallas_call(
        functools.partial(_swa_kernel, nt - 1),
        grid_spec=grid_spec,
        out_shape=jax.ShapeDtypeStruct((bsz, t, qtot), BF16),
        compiler_params=_cparams(3),
        name="swa_attn",
    )(sink, q, k, vt)


def _out_proj_kernel(n_o, split, nlat, *refs):
    o_refs = refs[:n_o]
    w_refs = refs[n_o:2 * n_o]
    refs = refs[2 * n_o:]
    if split:
        x_in = _first_layer_x(refs[0], refs[1], nlat)
        refs = refs[2:]
    else:
        x_in = refs[0][0]
        refs = refs[1:]
    mod_ref, g_ref, wr_ref, br_ref, xo_ref, h_ref, lg_ref = refs
    acc = jnp.dot(o_refs[0][0], w_refs[0][...], preferred_element_type=F32)
    for n in range(1, n_o):
        acc = acc + jnp.dot(o_refs[n][0], w_refs[n][...], preferred_element_type=F32)
    m = mod_ref[0, 0]
    x = x_in + m[2:3, :] * acc
    xo_ref[0] = x
    h = _rms(x, g_ref[...], NORM_EPS) * (1.0 + m[4:5, :]) + m[3:4, :]
    hi = h.astype(BF16)
    lo = (h - hi.astype(F32)).astype(BF16)
    h_ref[0] = hi
    both = jnp.dot(hi, wr_ref[...], preferred_element_type=F32)
    lg_ref[0] = (both[:, :LANES] + both[:, LANES:]
                 + jnp.dot(lo, wr_ref[:, :LANES], preferred_element_type=F32) + br_ref[...])


def _out_proj_call(os_, ws, x, mod, g, wr, br):
    split = isinstance(x, (tuple, list))
    if split:
        bsz, t, d = x[0].shape[0], x[0].shape[1] + x[1].shape[1], x[0].shape[2]
    else:
        bsz, t, d = x.shape
    nt = t // TM
    nlat = nt - 1
    n_o = len(os_)
    x_ins = list(x) if split else [x]
    x_specs = _split_specs(nlat, d) if split else [_row_spec(d)]
    ins = list(os_) + list(ws) + x_ins + [mod, g.reshape(1, d), wr, br]
    specs = ([_row_spec(o.shape[-1]) for o in os_] + [_full_spec(w.shape) for w in ws]
             + x_specs + [_mod_spec(nlat), _full_spec((1, d)), _full_spec(wr.shape), _full_spec(br.shape)])
    return pl.pallas_call(
        functools.partial(_out_proj_kernel, n_o, split, nlat),
        grid=(bsz, nt),
        in_specs=specs,
        out_specs=[_row_spec(d), _row_spec(d), _row_spec(LANES)],
        out_shape=[jax.ShapeDtypeStruct((bsz, t, d), F32),
                   jax.ShapeDtypeStruct((bsz, t, d), BF16),
                   jax.ShapeDtypeStruct((bsz, t, LANES), F32)],
        compiler_params=_cparams(2),
        name="out_proj",
    )(*ins)


def _moe_kernel(be_ref, nu_ref, x_ref, w1_ref, w3_ref, w2_ref, o_ref, w1c, w3c, w2c):
    i = pl.program_id(0)
    e = be_ref[i]

    @pl.when((i == 0) | (e != be_ref[jnp.maximum(i - 1, 0)]))
    def _():
        w1c[...] = w1_ref[0].astype(BF16)
        w3c[...] = w3_ref[0].astype(BF16)
        w2c[...] = w2_ref[0].astype(BF16)

    @pl.when(i < nu_ref[0])
    def _():
        x = x_ref[...]
        a = jnp.dot(x, w1c[...], preferred_element_type=F32)
        b = jnp.dot(x, w3c[...], preferred_element_type=F32)
        hmid = (a * jax.nn.sigmoid(a)) * b
        o_ref[...] = jnp.dot(hmid.astype(BF16), w2c[...], preferred_element_type=F32)

    @pl.when(i >= nu_ref[0])
    def _():
        o_ref[...] = jnp.zeros_like(o_ref)


def _moe_call(block_e, n_used, xb, w1, w3, w2):
    n_slots, d = xb.shape
    n_blocks = n_slots // TMOE
    ff = w1.shape[-1]
    grid_spec = pltpu.PrefetchScalarGridSpec(
        num_scalar_prefetch=2,
        grid=(n_blocks,),
        in_specs=[pl.BlockSpec((TMOE, d), lambda i, be, nu: (i, 0)),
                  pl.BlockSpec((1, d, ff), lambda i, be, nu: (be[i], 0, 0)),
                  pl.BlockSpec((1, d, ff), lambda i, be, nu: (be[i], 0, 0)),
                  pl.BlockSpec((1, ff, d), lambda i, be, nu: (be[i], 0, 0))],
        out_specs=pl.BlockSpec((TMOE, d), lambda i, be, nu: (i, 0)),
        scratch_shapes=[pltpu.VMEM((d, ff), BF16), pltpu.VMEM((d, ff), BF16), pltpu.VMEM((ff, d), BF16)],
    )
    return pl.pallas_call(
        _moe_kernel,
        grid_spec=grid_spec,
        out_shape=jax.ShapeDtypeStruct((n_slots, d), F32),
        compiler_params=_cparams(1),
        name="moe_experts",
    )(block_e, n_used, xb, w1, w3, w2)


def _router_kernel(lg_ref, tri_ref, rt_ref, cnt_ref, run):
    i = pl.program_id(0)

    @pl.when(i == 0)
    def _():
        run[...] = jnp.zeros_like(run)

    lg = lg_ref[...]
    lane = lax.broadcasted_iota(jnp.int32, lg.shape, 1)
    gmask = lane < N_GROUPS
    gl = jnp.where(gmask, lg, NEG)
    gmax = jnp.max(gl, axis=1, keepdims=True)
    grp = jnp.min(jnp.where(gl == gmax, lane, LANES), axis=1, keepdims=True)
    p_grp = 1.0 / jnp.sum(jnp.where(gmask, jnp.exp(lg - gmax), 0.0), axis=1, keepdims=True)
    first = N_GROUPS + EXPERTS_PER_GROUP * grp
    emask = (lane >= first) & (lane < first + EXPERTS_PER_GROUP)
    el = jnp.where(emask, lg, NEG)
    e1 = jnp.max(el, axis=1, keepdims=True)
    i1 = jnp.min(jnp.where(el == e1, lane, LANES), axis=1, keepdims=True)
    el2 = jnp.where(lane == i1, NEG, el)
    e2 = jnp.max(el2, axis=1, keepdims=True)
    i2 = jnp.min(jnp.where(el2 == e2, lane, LANES), axis=1, keepdims=True)
    tt = jnp.exp(e2 - e1)
    g1 = p_grp / (1.0 + tt)
    g2 = g1 * tt
    oh1 = lane == i1
    oh2 = lane == i2
    onehot = jnp.where(oh1 | oh2, 1.0, 0.0)
    rank_all = jnp.dot(tri_ref[...], onehot.astype(BF16), preferred_element_type=F32) + run[...]
    r1 = jnp.sum(jnp.where(oh1, rank_all, 0.0), axis=1, keepdims=True)
    r2 = jnp.sum(jnp.where(oh2, rank_all, 0.0), axis=1, keepdims=True)
    run[...] = run[...] + jnp.sum(onehot, axis=0, keepdims=True)
    cnt_ref[...] = run[...]
    vals = [(i1 - N_GROUPS).astype(F32), (i2 - N_GROUPS).astype(F32), r1, r2, g1, g2]
    packed = jnp.zeros(lg.shape, F32)
    for n, v in enumerate(vals):
        packed = jnp.where(lane == n, v, packed)
    rt_ref[...] = packed.T[0:8, :]


def _router_call(logits):
    n_tok = logits.shape[0]
    tri = (jnp.arange(TM)[:, None] > jnp.arange(TM)[None, :]).astype(BF16)
    return pl.pallas_call(
        _router_kernel,
        grid=(n_tok // TM,),
        in_specs=[pl.BlockSpec((TM, LANES), lambda i: (i, 0)),
                  pl.BlockSpec((TM, TM), lambda i: (0, 0))],
        out_specs=[pl.BlockSpec((8, TM), lambda i: (0, i)),
                   pl.BlockSpec((1, LANES), lambda i: (0, 0))],
        out_shape=[jax.ShapeDtypeStruct((8, n_tok), F32), jax.ShapeDtypeStruct((1, LANES), F32)],
        scratch_shapes=[pltpu.VMEM((1, LANES), F32)],
        compiler_params=_cparams(1),
        name="router",
    )(logits, tri)


def _dispatch_plan(rt, cnt):
    n_tok = rt.shape[1]
    eid = rt[0:2].astype(jnp.int32)
    rank = rt[2:4].astype(jnp.int32)
    gate = rt[4:6]
    counts = cnt[0, N_GROUPS:N_GROUPS + N_EXPERTS].astype(jnp.int32)
    padded = (counts + TMOE - 1) // TMOE * TMOE
    pad_end = jnp.cumsum(padded)
    pad_start = pad_end - padded
    dest = jnp.take(pad_start, eid) + rank
    n_assign = n_tok * TOP_K
    n_blocks = (n_assign + N_EXPERTS * (TMOE - 1) + TMOE - 1) // TMOE
    n_slots = n_blocks * TMOE
    tok = jnp.arange(n_tok, dtype=jnp.int32)
    slot_tok = jnp.zeros((n_slots,), jnp.int32).at[dest.reshape(n_assign)].set(jnp.concatenate([tok, tok]))
    block_start = jnp.arange(n_blocks, dtype=jnp.int32) * TMOE
    block_e = jnp.minimum(jnp.sum((block_start[:, None] >= pad_end[None, :]).astype(jnp.int32), axis=1),
                          N_EXPERTS - 1).astype(jnp.int32)
    n_used = (pad_end[-1:] // TMOE).astype(jnp.int32)
    return slot_tok, gate, block_e, n_used, dest


def _final_kernel(x_ref, y_ref, pmod_ref, g_ref, o_ref):
    x = x_ref[0] + pmod_ref[0, 0][5:6, :] * y_ref[0]
    o_ref[0] = _rms(x, g_ref[...], NORM_EPS)


def _final_call(x, y, pmod, g, s_len):
    bsz, t, d = x.shape
    return pl.pallas_call(
        _final_kernel,
        grid=(bsz, s_len // TM),
        in_specs=[_row_spec(d), _row_spec(d),
                  pl.BlockSpec((1, 1, 8, d), lambda b, i: (b, 0, 0, 0)),
                  _full_spec((1, d))],
        out_specs=_row_spec(d),
        out_shape=jax.ShapeDtypeStruct((bsz, s_len, d), F32),
        compiler_params=_cparams(2),
        name="final_norm",
    )(x, y, pmod, g.reshape(1, d))


def _take_cols(w, idx):
    wz = jnp.concatenate([w, jnp.zeros((w.shape[0], 1), w.dtype)], axis=1)
    return jnp.take(wz, jnp.asarray(idx, dtype=jnp.int32), axis=1).astype(BF16)


def _ab_layouts():
    zc = 1184
    cols = []
    for j in range(4):
        cols.append(j * 128 + _PAIR_PERM)
    for g in range(2):
        base = 512 + g * 64
        cols.append(base + np.concatenate([np.arange(0, 32), np.arange(0, 32), np.arange(32, 64), np.arange(32, 64)]))
    for g in range(2):
        base = 640 + g * 64
        cols.append(base + np.concatenate([np.arange(64), np.arange(64)]))
    cols.append(768 + np.arange(256))
    cols.append(1024 + np.arange(128))
    kr = 1152
    z32 = np.full((32,), zc)
    cols.append(np.concatenate([kr + np.arange(16), kr + np.arange(16), z32,
                                kr + 16 + np.arange(16), kr + 16 + np.arange(16), z32]))
    w1_idx = np.concatenate(cols)

    zq = 768
    uq = []
    z32q = np.full((32,), zq)
    for j in range(4):
        a, b = 2 * j * 96, (2 * j + 1) * 96
        uq.append(np.concatenate([a + np.arange(64), b + np.arange(64),
                                  a + 64 + np.arange(16), b + 64 + np.arange(16), z32q,
                                  a + 80 + np.arange(16), b + 80 + np.arange(16), z32q]))
    uq_idx = np.concatenate(uq)

    kn, mv = [], []
    for h in range(8):
        kn.append(h * 128 + np.arange(64))
        mv.append(h * 128 + 64 + np.arange(64))
    ukv_idx = np.concatenate(kn + mv)
    return w1_idx, uq_idx, ukv_idx


def _diff_layout():
    cols = []
    for part in range(2):
        for h in range(8):
            cols.append(part * 1024 + h * 128 + _PAIR_PERM)
    cols.append(2048 + np.arange(1024))
    return np.concatenate(cols)


def kernel(x, c, ctx, c_ctx, norm_mix, norm_ffn, ada_w, ada_b, ab_w_in, mla_q_norm, mla_w_uq, mla_kv_norm, mla_w_ukv, swa_sink, ab_w_out, diff_w_in, diff_lambda_q1, diff_lambda_k1, diff_lambda_q2, diff_lambda_k2, diff_subln, diff_w_out, router_group_w, router_group_b, router_expert_w, router_expert_b, expert_w1, expert_w3, expert_w2, final_norm):
    bsz, s_len, d = x.shape
    c_len = ctx.shape[1]
    depth = ada_w.shape[0]
    assert d == D_MODEL and c_len == TM and s_len % TQ == 0 and s_len % (2 * TK) == 0
    t = s_len + c_len

    xs = (x, ctx)
    tables = _rope_tables(s_len, c_len)
    c64, s64, c32, s32 = tables

    n_rows = (bsz + 1 + 7) // 8 * 8
    rows = jnp.concatenate([c, c_ctx[None, :], jnp.zeros((n_rows - bsz - 1, d), F32)], axis=0)
    mod_all = _ada_call(rows, ada_w, ada_b)
    mod_lat = mod_all[:, :bsz].reshape(depth, bsz, 1, 6, d)
    mod_ctx = jnp.broadcast_to(mod_all[:, bsz].reshape(depth, 1, 1, 6, d), (depth, bsz, 1, 6, d))
    mods = jnp.concatenate([mod_lat, mod_ctx], axis=2)
    mods = jnp.concatenate([mods, jnp.zeros((depth, bsz, 2, 2, d), F32)], axis=3)

    w1_idx, uq_idx, ukv_idx = _ab_layouts()
    diff_idx = _diff_layout()

    res = None
    for l in range(depth):
        j = l // 2
        mod = mods[l]
        if l % 2 == 0:
            weights = [_take_cols(ab_w_in[j], w1_idx), mla_q_norm[j].reshape(1, -1), mla_kv_norm[j].reshape(1, -1),
                       _take_cols(mla_w_uq[j], uq_idx), _take_cols(mla_w_ukv[j], ukv_idx)]
            xs, (sq, sk, sv, mq, mk, mv) = _proj_call("ab", xs, res, mod, norm_mix[l], weights,
                                                      [c64, s64, c32, s32], [512, 256, -256, 1024, 1024, -512])
            o_a = _swa_call(swa_sink[j], sq, sk, sv)
            o_b = _dense_attn_call("mla", mq, mk, mv)
            w_out = ab_w_out[j].astype(BF16)
            attn_outs, out_ws = [o_a, o_b], [w_out[:512], w_out[512:]]
        else:
            lambda_init = 0.8 - 0.6 * math.exp(-0.3 * l)
            weights = [_take_cols(diff_w_in[j], diff_idx)]
            xs, (dq, dk, dv) = _proj_call("diff", xs, res, mod, norm_mix[l], weights, [c64, s64], [1024, 1024, -1024])
            extra = [diff_lambda_q1[j].reshape(1, -1), diff_lambda_k1[j].reshape(1, -1),
                     diff_lambda_q2[j].reshape(1, -1), diff_lambda_k2[j].reshape(1, -1),
                     diff_subln[j].reshape(1, -1)]
            o_d = _dense_attn_call("diff", dq, dk, dv, extra=extra, lambda_init=lambda_init)
            attn_outs, out_ws = [o_d], [diff_w_out[j].astype(BF16)]

        wr = jnp.concatenate([router_group_w[l], router_expert_w[l],
                              jnp.zeros((d, LANES - N_GROUPS - N_EXPERTS), F32)], axis=1)
        wr_hi = wr.astype(BF16)
        wr = jnp.concatenate([wr_hi, (wr - wr_hi.astype(F32)).astype(BF16)], axis=1)
        br = jnp.concatenate([router_group_b[l], router_expert_b[l],
                              jnp.zeros((LANES - N_GROUPS - N_EXPERTS,), F32)]).reshape(1, LANES)
        xs, h2, logits = _out_proj_call(attn_outs, out_ws, xs, mod, norm_ffn[l], wr, br)

        n_tok = bsz * t
        rt, cnt = _router_call(logits.reshape(n_tok, LANES))
        slot_tok, gate, block_e, n_used, dest = _dispatch_plan(rt, cnt)
        xb = jnp.take(h2.reshape(n_tok, d), slot_tok, axis=0)
        yb = _moe_call(block_e, n_used, xb, expert_w1[l], expert_w3[l], expert_w2[l])
        y = (gate[0][:, None] * jnp.take(yb, dest[0], axis=0)
             + gate[1][:, None] * jnp.take(yb, dest[1], axis=0)).reshape(bsz, t, d)
        res = (y, mod)

    return _final_call(xs, res[0], res[1], final_norm, s_len)
```

```python
import functools
import math

import numpy as np
import jax
import jax.numpy as jnp
from jax import lax
from jax.experimental import pallas as pl
from jax.experimental.pallas import tpu as pltpu

F32 = jnp.float32
BF16 = jnp.bfloat16

D_MODEL = 1024
GRID_W = 64
ROPE_BASE = 10000.0
NORM_EPS = 1e-6
DIFF_EPS = 1e-5
NEG = -1e30

SWA_WINDOW = 128
MLA_SCALE = (64 + 32) ** -0.5
HEAD_SCALE = 64 ** -0.5
LOG2E = math.log2(math.e)

N_GROUPS = 4
EXPERTS_PER_GROUP = 8
N_EXPERTS = 32
TOP_K = 2
EXPERT_FF = 512

LANES = 128
TM = 256
TK = 256
TQ = 512
TMOE = 256
VMEM_LIMIT = 56 * 1024 * 1024


def _cparams(n_axes):
    return pltpu.CompilerParams(dimension_semantics=("arbitrary",) * n_axes,
                                vmem_limit_bytes=VMEM_LIMIT)


def _rms(x, g, eps):
    return x * lax.rsqrt(jnp.mean(x * x, axis=-1, keepdims=True) + eps) * g


def _rope_block(x, c, s):
    return x * c + pltpu.roll(x, 64, 1) * s


_PAIR_PERM = np.concatenate([np.arange(0, 32), np.arange(64, 96), np.arange(32, 64), np.arange(96, 128)])


def _rope_tables(s_len, c_len):
    rows = s_len // GRID_W
    row = jnp.repeat(jnp.arange(rows, dtype=F32), GRID_W)
    col = jnp.tile(jnp.arange(GRID_W, dtype=F32), rows)

    def tab(dim):
        nf = dim // 4
        inv = ROPE_BASE ** (-jnp.arange(nf, dtype=F32) / nf)
        ang = jnp.concatenate([row[:, None] * inv, col[:, None] * inv], axis=-1)
        return jnp.cos(ang), jnp.sin(ang)

    cos64, sin64 = tab(64)
    cos32, sin32 = tab(32)
    c64 = jnp.concatenate([cos64] * 4, axis=-1)
    s64 = jnp.concatenate([-sin64, -sin64, sin64, sin64], axis=-1)
    one = jnp.ones((s_len, 32), F32)
    zero = jnp.zeros((s_len, 32), F32)
    c32 = jnp.concatenate([cos32, cos32, one, cos32, cos32, one], axis=-1)
    s32 = jnp.concatenate([-sin32, -sin32, zero, sin32, sin32, zero], axis=-1)

    def ext(t, fill):
        return jnp.concatenate([t, jnp.full((c_len, LANES), fill, F32)], axis=0)

    return ext(c64, 1.0), ext(s64, 0.0), ext(c32, 1.0), ext(s32, 0.0)


def _ada_kernel(x_ref, w_ref, b_ref, o_ref):
    x = x_ref[...]
    sx = x * jax.nn.sigmoid(x)
    o_ref[0] = jnp.dot(sx.astype(BF16), w_ref[0].astype(BF16), preferred_element_type=F32) + b_ref[0]


def _ada_call(rows, ada_w, ada_b):
    depth, d, n6 = ada_w.shape
    r = rows.shape[0]
    tn = 1536
    return pl.pallas_call(
        _ada_kernel,
        grid=(depth, n6 // tn),
        in_specs=[pl.BlockSpec((r, d), lambda l, j: (0, 0)),
                  pl.BlockSpec((1, d, tn), lambda l, j: (l, 0, j)),
                  pl.BlockSpec((1, 1, tn), lambda l, j: (l, 0, j))],
        out_specs=pl.BlockSpec((1, r, tn), lambda l, j: (l, 0, j)),
        out_shape=jax.ShapeDtypeStruct((depth, r, n6), F32),
        compiler_params=_cparams(2),
        name="ada_mod",
    )(rows, ada_w, ada_b.reshape(depth, 1, n6))


def _first_layer_x(x_ref, ctx_ref, nlat):
    return jnp.where(pl.program_id(1) < nlat, x_ref[0], ctx_ref[0])


def _prenorm(has_res, nlat, x_ref, y_ref, pmod_ref, mod_ref, g_ref, xo_ref):
    if has_res:
        x = x_ref[0] + pmod_ref[0, 0][5:6, :] * y_ref[0]
        xo_ref[0] = x
    else:
        x = _first_layer_x(x_ref, y_ref, nlat)
    m = mod_ref[0, 0]
    return _rms(x, g_ref[...], NORM_EPS) * (1.0 + m[1:2, :]) + m[0:1, :]


def _ab_proj_kernel(has_res, nlat, *refs):
    if has_res:
        x_ref, y_ref, pmod_ref = refs[:3]
        refs = refs[3:]
    else:
        x_ref, y_ref, pmod_ref = refs[0], refs[1], None
        refs = refs[2:]
    (mod_ref, g_ref, w1_ref, qn_ref, kvn_ref, wuq_ref, wukv_ref,
     c64_ref, s64_ref, c32_ref, s32_ref) = refs[:11]
    outs = refs[11:]
    if has_res:
        xo_ref, outs = outs[0], outs[1:]
    else:
        xo_ref = None
    sq_ref, sk_ref, sv_ref, mq_ref, mk_ref, mv_ref = outs

    h = _prenorm(has_res, nlat, x_ref, y_ref, pmod_ref, mod_ref, g_ref, xo_ref)
    p = jnp.dot(h.astype(BF16), w1_ref[...], preferred_element_type=F32)
    c64, s64, c32, s32 = c64_ref[...], s64_ref[...], c32_ref[...], s32_ref[...]
    for j in range(4):
        blk = _rope_block(p[:, j * 128:(j + 1) * 128], c64, s64)
        sq_ref[0, :, j * 128:(j + 1) * 128] = (blk * (HEAD_SCALE * LOG2E)).astype(BF16)
    for j in range(2):
        blk = _rope_block(p[:, 512 + j * 128:512 + (j + 1) * 128], c64, s64)
        sk_ref[0, :, j * 128:(j + 1) * 128] = blk.astype(BF16)
    sv_ref[0, 0] = p[:, 768:1024].T.astype(BF16)
    cq = p[:, 1024:1280]
    ckv = p[:, 1280:1408]
    kr = _rope_block(p[:, 1408:1536], c32, s32).astype(BF16)
    qm = jnp.dot(_rms(cq, qn_ref[...], NORM_EPS).astype(BF16), wuq_ref[...], preferred_element_type=F32)
    kv = jnp.dot(_rms(ckv, kvn_ref[...], NORM_EPS).astype(BF16), wukv_ref[...], preferred_element_type=F32)
    for j in range(4):
        mq_ref[0, :, j * 256:j * 256 + 128] = (qm[:, j * 256:j * 256 + 128] * (MLA_SCALE * LOG2E)).astype(BF16)
        rr = _rope_block(qm[:, j * 256 + 128:(j + 1) * 256], c32, s32)
        mq_ref[0, :, j * 256 + 128:(j + 1) * 256] = (rr * (MLA_SCALE * LOG2E)).astype(BF16)
        mk_ref[0, :, j * 256:j * 256 + 128] = kv[:, j * 128:(j + 1) * 128].astype(BF16)
        mk_ref[0, :, j * 256 + 128:(j + 1) * 256] = kr
    mv_ref[0, 0] = kv[:, 512:1024].T.astype(BF16)


def _diff_proj_kernel(has_res, nlat, *refs):
    if has_res:
        x_ref, y_ref, pmod_ref = refs[:3]
        refs = refs[3:]
    else:
        x_ref, y_ref, pmod_ref = refs[0], refs[1], None
        refs = refs[2:]
    mod_ref, g_ref, w_ref, c64_ref, s64_ref = refs[:5]
    outs = refs[5:]
    if has_res:
        xo_ref, outs = outs[0], outs[1:]
    else:
        xo_ref = None
    q_ref, k_ref, v_ref = outs
    h = _prenorm(has_res, nlat, x_ref, y_ref, pmod_ref, mod_ref, g_ref, xo_ref)
    p = jnp.dot(h.astype(BF16), w_ref[...], preferred_element_type=F32)
    c64, s64 = c64_ref[...], s64_ref[...]
    for j in range(8):
        blk = _rope_block(p[:, j * 128:(j + 1) * 128], c64, s64)
        q_ref[0, :, j * 128:(j + 1) * 128] = (blk * (HEAD_SCALE * LOG2E)).astype(BF16)
        blk = _rope_block(p[:, 1024 + j * 128:1024 + (j + 1) * 128], c64, s64)
        k_ref[0, :, j * 128:(j + 1) * 128] = blk.astype(BF16)
    v_ref[0, 0] = p[:, 2048:3072].T.astype(BF16)


def _row_spec(width):
    return pl.BlockSpec((1, TM, width), lambda b, i: (b, i, 0))


def _mod_spec(nlat):
    return pl.BlockSpec((1, 1, 8, D_MODEL), lambda b, i: (b, i // nlat, 0, 0))


def _full_spec(shape):
    nd = len(shape)
    return pl.BlockSpec(shape, lambda b, i: (0,) * nd)


def _tab_spec():
    return pl.BlockSpec((TM, LANES), lambda b, i: (i, 0))


def _split_specs(nlat, d):
    return [pl.BlockSpec((1, TM, d), lambda b, i: (b, jnp.minimum(i, nlat - 1), 0)),
            pl.BlockSpec((1, TM, d), lambda b, i: (b, 0, 0))]


def _proj_call(kind, x, res, mod, g, weights, tables, out_widths):
    has_res = res is not None
    if has_res:
        bsz, t, d = x.shape
    else:
        bsz, t, d = x[0].shape[0], x[0].shape[1] + x[1].shape[1], x[0].shape[2]
    nt = t // TM
    nlat = nt - 1
    if has_res:
        y, pmod = res
        ins = [x, y, pmod]
        specs = [_row_spec(d), _row_spec(d), _mod_spec(nlat)]
    else:
        ins = list(x)
        specs = _split_specs(nlat, d)
    ins += [mod, g.reshape(1, d)]
    specs += [_mod_spec(nlat), _full_spec((1, d))]
    for w in weights:
        ins.append(w)
        specs.append(_full_spec(w.shape))
    for tb in tables:
        ins.append(tb)
        specs.append(_tab_spec())
    out_shapes, out_specs = [], []
    if has_res:
        out_shapes.append(jax.ShapeDtypeStruct((bsz, t, d), F32))
        out_specs.append(_row_spec(d))
    for w in out_widths:
        if w < 0:
            out_shapes.append(jax.ShapeDtypeStruct((bsz, nt, -w, TM), BF16))
            out_specs.append(pl.BlockSpec((1, 1, -w, TM), lambda b, i: (b, i, 0, 0)))
        else:
            out_shapes.append(jax.ShapeDtypeStruct((bsz, t, w), BF16))
            out_specs.append(_row_spec(w))
    body = _ab_proj_kernel if kind == "ab" else _diff_proj_kernel
    outs = pl.pallas_call(
        functools.partial(body, has_res, nlat),
        grid=(bsz, nt),
        in_specs=specs,
        out_specs=out_specs,
        out_shape=out_shapes,
        compiler_params=_cparams(2),
        name=kind + "_proj",
    )(*ins)
    if has_res:
        return outs[0], outs[1:]
    return x, outs


def _pair_masks(mode, lane):
    if mode == "mla":
        in_a = (lane < 64) | ((lane >= 128) & (lane < 144)) | ((lane >= 192) & (lane < 208))
        in_b = ((lane >= 64) & (lane < 128)) | ((lane >= 144) & (lane < 160)) | ((lane >= 208) & (lane < 224))
    else:
        in_a = (lane < 32) | ((lane >= 64) & (lane < 96))
        in_b = ((lane >= 32) & (lane < 64)) | (lane >= 96)
    return in_a, in_b


def _dense_attn_kernel(mode, lambda_init, nlat, *refs):
    n_in = 8 if mode == "diff" else 3
    q_ref, k_ref, vt_ref = refs[:3]
    o_ref = refs[n_in]
    s_bufs = refs[n_in + 1:n_in + 3]
    p_bufs = refs[n_in + 3:n_in + 5]
    acc_buf, m_buf, l_buf, a_buf = refs[n_in + 5:]
    i = pl.program_id(2)
    qt = q_ref[0].T
    width, tq = qt.shape
    row = lax.broadcasted_iota(jnp.int32, (width, 1), 0)
    in_a, in_b = _pair_masks(mode, row)
    zero = jnp.zeros_like(qt)
    q2 = jnp.concatenate([jnp.where(in_a, qt, zero), jnp.where(in_b, qt, zero)], axis=1)
    chunks = [(nlat, 1)] + [(2 * c, 2) for c in range(nlat // 2)]
    n_lat_q = (nlat * TK) // tq

    def scores(chunk, s_buf):
        k0, n = chunk
        s_buf[0:n * TK, :] = jnp.dot(k_ref[0, k0 * TK:(k0 + n) * TK, :], q2, preferred_element_type=F32)

    def pv(chunk, p_buf):
        k0, n = chunk
        out = jnp.dot(vt_ref[0, k0], p_buf[0:TK, :], preferred_element_type=F32)
        for r in range(1, n):
            out = out + jnp.dot(vt_ref[0, k0 + r], p_buf[r * TK:(r + 1) * TK, :], preferred_element_type=F32)
        return out

    def softmax(chunk, s_buf, p_buf, first):
        n = chunk[1]
        s = s_buf[0:n * TK, :]
        mx = jnp.max(s, axis=0, keepdims=True)
        if first:
            mn = mx
        else:
            m = m_buf[...]
            mn = jnp.maximum(m, mx)
            a_buf[...] = jnp.exp2(m - mn)
        p = jnp.exp2(s - mn)
        ps = jnp.sum(p, axis=0, keepdims=True)
        l_buf[...] = ps if first else a_buf[...] * l_buf[...] + ps
        m_buf[...] = mn
        p_buf[0:n * TK, :] = p.astype(BF16)

    def accumulate(chunk, p_buf, first):
        if first:
            acc_buf[...] = pv(chunk, p_buf)
        else:
            acc_buf[...] = a_buf[...] * acc_buf[...] + pv(chunk, p_buf)

    def pipeline(chs):
        scores(chs[0], s_bufs[0])
        for c, ch in enumerate(chs):
            if c >= 2:
                accumulate(chs[c - 1], p_bufs[(c - 1) % 2], first=False)
            if c + 1 < len(chs):
                scores(chs[c + 1], s_bufs[(c + 1) % 2])
            softmax(ch, s_bufs[c % 2], p_bufs[c % 2], first=c == 0)
            if c == 1:
                accumulate(chs[0], p_bufs[0], first=True)
        last = len(chs) - 1
        if last == 0:
            accumulate(chs[0], p_bufs[0], first=True)
        else:
            accumulate(chs[last], p_bufs[last % 2], first=False)

    @pl.when(i < n_lat_q)
    def _():
        pipeline(chunks)

    @pl.when(i >= n_lat_q)
    def _():
        pipeline(chunks[:1])

    o2 = acc_buf[...] * (1.0 / l_buf[...])
    oa, ob = o2[:, :tq], o2[:, tq:]
    if mode == "diff":
        lq1_ref, lk1_ref, lq2_ref, lk2_ref, sub_ref = refs[3:8]
        lam = (jnp.exp(jnp.sum(lq1_ref[...] * lk1_ref[...], axis=1, keepdims=True))
               - jnp.exp(jnp.sum(lq2_ref[...] * lk2_ref[...], axis=1, keepdims=True)) + lambda_init)
        o = (oa - lam * ob).T
        o = _rms(o, sub_ref[...], DIFF_EPS) * (1.0 - lambda_init)
    else:
        vrow = lax.broadcasted_iota(jnp.int32, (LANES, 1), 0)
        o = jnp.where(vrow < 64, oa, ob).T
    o_ref[0] = o.astype(BF16)


def _dense_attn_call(mode, q, k, vt, extra=(), lambda_init=0.0):
    bsz, t, qtot = q.shape
    width = 256 if mode == "mla" else 128
    npairs = qtot // width
    nq = pl.cdiv(t, TQ)
    ins = [q, k, vt]
    specs = [pl.BlockSpec((1, TQ, width), lambda b, j, i: (b, i, j)),
             pl.BlockSpec((1, t, width), lambda b, j, i: (b, 0, j)),
             pl.BlockSpec((1, t // TK, LANES, TK), lambda b, j, i: (b, 0, j, 0))]
    for e in extra:
        ins.append(e)
        specs.append(pl.BlockSpec(e.shape, lambda b, j, i: (0, 0)))
    return pl.pallas_call(
        functools.partial(_dense_attn_kernel, mode, lambda_init, t // TK - 1),
        grid=(bsz, npairs, nq),
        in_specs=specs,
        out_specs=pl.BlockSpec((1, TQ, LANES), lambda b, j, i: (b, i, j)),
        out_shape=jax.ShapeDtypeStruct((bsz, t, npairs * LANES), BF16),
        scratch_shapes=[pltpu.VMEM((2 * TK, 2 * TQ), F32), pltpu.VMEM((2 * TK, 2 * TQ), F32),
                        pltpu.VMEM((2 * TK, 2 * TQ), BF16), pltpu.VMEM((2 * TK, 2 * TQ), BF16),
                        pltpu.VMEM((LANES, 2 * TQ), F32), pltpu.VMEM((1, 2 * TQ), F32),
                        pltpu.VMEM((1, 2 * TQ), F32), pltpu.VMEM((1, 2 * TQ), F32)],
        compiler_params=_cparams(3),
        name=mode + "_attn",
    )(*ins)


def _swa_kernel(nlat, sink_ref, q_ref, k_ref, vt_ref, o_ref):
    g = pl.program_id(1)
    i = pl.program_id(2)
    s_len = nlat * TM
    half = TM // 2
    q = q_ref[0]
    row = lax.broadcasted_iota(jnp.int32, (LANES, 1), 0)
    in_a, in_b = _pair_masks("pair", row)
    cols = []
    for pr in range(2):
        qt = q[:, pr * LANES:(pr + 1) * LANES].T
        zero = jnp.zeros_like(qt)
        cols += [jnp.where(in_a, qt, zero), jnp.where(in_b, qt, zero)]
    q4 = jnp.concatenate(cols, axis=1)

    start0 = pl.multiple_of(jnp.maximum(i * TM - half, 0), half)
    start1 = pl.multiple_of(i * TM, TM)
    start2 = pl.multiple_of(jnp.minimum((i + 1) * TM, s_len + half), half)
    k_cat = jnp.concatenate([k_ref[0, pl.ds(start0, half), :], k_ref[0, pl.ds(start1, TM), :],
                             k_ref[0, pl.ds(start2, half), :], k_ref[0, s_len:s_len + TM, :]], axis=0)
    b0 = jnp.maximum(i - 1, 0)
    b2 = jnp.minimum(i + 1, nlat)
    vt_cat = jnp.concatenate([vt_ref[0, b0][:, half:], vt_ref[0, i], vt_ref[0, b2][:, :half],
                              vt_ref[0, nlat]], axis=1)

    r = lax.broadcasted_iota(jnp.int32, (2 * TM, 1), 0)
    far = -4 * SWA_WINDOW
    lat = i < nlat
    pos0 = jnp.where(lat & (i >= 1), start0 + r, far)
    pos1 = jnp.where(lat, start1 + r - half, far)
    pos2 = jnp.where(lat & (i + 1 < nlat), start2 + r - half - TM, far)
    kpos = jnp.where(r < half, pos0, jnp.where(r < half + TM, pos1, pos2))
    qpos = i * TM + lax.broadcasted_iota(jnp.int32, (1, TM), 1)
    band = jnp.abs(qpos - kpos) <= SWA_WINDOW
    band4 = jnp.concatenate([band] * 4, axis=1)

    s = jnp.dot(k_cat, q4, preferred_element_type=F32)
    s_loc = jnp.where(band4, s[:2 * TM], NEG)
    s_ctx = s[2 * TM:]
    sink = jnp.concatenate([jnp.full((1, TM), sink_ref[4 * g + h], F32) for h in range(4)], axis=1) * LOG2E
    m = jnp.maximum(jnp.maximum(jnp.max(s_loc, axis=0, keepdims=True), jnp.max(s_ctx, axis=0, keepdims=True)), sink)
    p_loc = jnp.exp2(s_loc - m)
    p_ctx = jnp.exp2(s_ctx - m)
    l = jnp.sum(p_loc, axis=0, keepdims=True) + jnp.sum(p_ctx, axis=0, keepdims=True) + jnp.exp2(sink - m)
    p = jnp.concatenate([p_loc, p_ctx], axis=0).astype(BF16)
    o4 = jnp.dot(vt_cat, p, preferred_element_type=F32) * (1.0 / l)
    vrow = lax.broadcasted_iota(jnp.int32, (LANES, 1), 0)
    for pr in range(2):
        oa = o4[:, (2 * pr) * TM:(2 * pr + 1) * TM]
        ob = o4[:, (2 * pr + 1) * TM:(2 * pr + 2) * TM]
        o_ref[0, :, pr * LANES:(pr + 1) * LANES] = jnp.where(vrow < 64, oa, ob).T.astype(BF16)


def _swa_call(sink, q, k, vt):
    bsz, t, qtot = q.shape
    nkv = k.shape[-1] // LANES
    nt = t // TM
    grid_spec = pltpu.PrefetchScalarGridSpec(
        num_scalar_prefetch=1,
        grid=(bsz, nkv, nt),
        in_specs=[pl.BlockSpec((1, TM, 2 * LANES), lambda b, g, i, s: (b, i, g)),
                  pl.BlockSpec((1, t, LANES), lambda b, g, i, s: (b, 0, g)),
                  pl.BlockSpec((1, nt, LANES, TM), lambda b, g, i, s: (b, 0, g, 0))],
        out_specs=pl.BlockSpec((1, TM, 2 * LANES), lambda b, g, i, s: (b, i, g)),
    )
    return pl.pallas_call(
        functools.partial(_swa_kernel, nt - 1),
        grid_spec=grid_spec,
        out_shape=jax.ShapeDtypeStruct((bsz, t, qtot), BF16),
        compiler_params=_cparams(3),
        name="swa_attn",
    )(sink, q, k, vt)


def _out_proj_kernel(n_o, split, nlat, *refs):
    o_refs = refs[:n_o]
    w_refs = refs[n_o:2 * n_o]
    refs = refs[2 * n_o:]
    if split:
        x_in = _first_layer_x(refs[0], refs[1], nlat)
        refs = refs[2:]
    else:
        x_in = refs[0][0]
        refs = refs[1:]
    mod_ref, g_ref, wr_ref, br_ref, xo_ref, h_ref, lg_ref = refs
    acc = jnp.dot(o_refs[0][0], w_refs[0][...], preferred_element_type=F32)
    for n in range(1, n_o):
        acc = acc + jnp.dot(o_refs[n][0], w_refs[n][...], preferred_element_type=F32)
    m = mod_ref[0, 0]
    x = x_in + m[2:3, :] * acc
    xo_ref[0] = x
    h = _rms(x, g_ref[...], NORM_EPS) * (1.0 + m[4:5, :]) + m[3:4, :]
    hi = h.astype(BF16)
    lo = (h - hi.astype(F32)).astype(BF16)
    h_ref[0] = hi
    both = jnp.dot(hi, wr_ref[...], preferred_element_type=F32)
    lg_ref[0] = (both[:, :LANES] + both[:, LANES:]
                 + jnp.dot(lo, wr_ref[:, :LANES], preferred_element_type=F32) + br_ref[...])


def _out_proj_call(os_, ws, x, mod, g, wr, br):
    split = isinstance(x, (tuple, list))
    if split:
        bsz, t, d = x[0].shape[0], x[0].shape[1] + x[1].shape[1], x[0].shape[2]
    else:
        bsz, t, d = x.shape
    nt = t // TM
    nlat = nt - 1
    n_o = len(os_)
    x_ins = list(x) if split else [x]
    x_specs = _split_specs(nlat, d) if split else [_row_spec(d)]
    ins = list(os_) + list(ws) + x_ins + [mod, g.reshape(1, d), wr, br]
    specs = ([_row_spec(o.shape[-1]) for o in os_] + [_full_spec(w.shape) for w in ws]
             + x_specs + [_mod_spec(nlat), _full_spec((1, d)), _full_spec(wr.shape), _full_spec(br.shape)])
    return pl.pallas_call(
        functools.partial(_out_proj_kernel, n_o, split, nlat),
        grid=(bsz, nt),
        in_specs=specs,
        out_specs=[_row_spec(d), _row_spec(d), _row_spec(LANES)],
        out_shape=[jax.ShapeDtypeStruct((bsz, t, d), F32),
                   jax.ShapeDtypeStruct((bsz, t, d), BF16),
                   jax.ShapeDtypeStruct((bsz, t, LANES), F32)],
        compiler_params=_cparams(2),
        name="out_proj",
    )(*ins)


def _moe_kernel(be_ref, nu_ref, x_ref, w1_ref, w3_ref, w2_ref, o_ref, w1c, w3c, w2c):
    i = pl.program_id(0)
    e = be_ref[i]

    @pl.when((i == 0) | (e != be_ref[jnp.maximum(i - 1, 0)]))
    def _():
        w1c[...] = w1_ref[0, 0].astype(BF16)
        w3c[...] = w3_ref[0, 0].astype(BF16)
        w2c[...] = w2_ref[0, 0].astype(BF16)

    @pl.when(i < nu_ref[0])
    def _():
        x = x_ref[...]
        a = jnp.dot(x, w1c[...], preferred_element_type=F32)
        b = jnp.dot(x, w3c[...], preferred_element_type=F32)
        hmid = (a * jax.nn.sigmoid(a)) * b
        o_ref[...] = jnp.dot(hmid.astype(BF16), w2c[...], preferred_element_type=F32)

    @pl.when(i >= nu_ref[0])
    def _():
        o_ref[...] = jnp.zeros_like(o_ref)


def _moe_call(layer, block_e, n_used, xb, w1, w3, w2):
    n_slots, d = xb.shape
    n_blocks = n_slots // TMOE
    ff = w1.shape[-1]
    grid_spec = pltpu.PrefetchScalarGridSpec(
        num_scalar_prefetch=2,
        grid=(n_blocks,),
        in_specs=[pl.BlockSpec((TMOE, d), lambda i, be, nu: (i, 0)),
                  pl.BlockSpec((1, 1, d, ff), lambda i, be, nu: (layer, be[i], 0, 0)),
                  pl.BlockSpec((1, 1, d, ff), lambda i, be, nu: (layer, be[i], 0, 0)),
                  pl.BlockSpec((1, 1, ff, d), lambda i, be, nu: (layer, be[i], 0, 0))],
        out_specs=pl.BlockSpec((TMOE, d), lambda i, be, nu: (i, 0)),
        scratch_shapes=[pltpu.VMEM((d, ff), BF16), pltpu.VMEM((d, ff), BF16), pltpu.VMEM((ff, d), BF16)],
    )
    return pl.pallas_call(
        _moe_kernel,
        grid_spec=grid_spec,
        out_shape=jax.ShapeDtypeStruct((n_slots, d), F32),
        compiler_params=_cparams(1),
        name="moe_experts",
    )(block_e, n_used, xb, w1, w3, w2)


def _router_kernel(lg_ref, tri_ref, rt_ref, cnt_ref, run):
    i = pl.program_id(0)

    @pl.when(i == 0)
    def _():
        run[...] = jnp.zeros_like(run)

    lg = lg_ref[...]
    lane = lax.broadcasted_iota(jnp.int32, lg.shape, 1)
    gmask = lane < N_GROUPS
    gl = jnp.where(gmask, lg, NEG)
    gmax = jnp.max(gl, axis=1, keepdims=True)
    grp = jnp.min(jnp.where(gl == gmax, lane, LANES), axis=1, keepdims=True)
    p_grp = 1.0 / jnp.sum(jnp.where(gmask, jnp.exp(lg - gmax), 0.0), axis=1, keepdims=True)
    first = N_GROUPS + EXPERTS_PER_GROUP * grp
    emask = (lane >= first) & (lane < first + EXPERTS_PER_GROUP)
    el = jnp.where(emask, lg, NEG)
    e1 = jnp.max(el, axis=1, keepdims=True)
    i1 = jnp.min(jnp.where(el == e1, lane, LANES), axis=1, keepdims=True)
    el2 = jnp.where(lane == i1, NEG, el)
    e2 = jnp.max(el2, axis=1, keepdims=True)
    i2 = jnp.min(jnp.where(el2 == e2, lane, LANES), axis=1, keepdims=True)
    tt = jnp.exp(e2 - e1)
    g1 = p_grp / (1.0 + tt)
    g2 = g1 * tt
    oh1 = lane == i1
    oh2 = lane == i2
    onehot = jnp.where(oh1 | oh2, 1.0, 0.0)
    rank_all = jnp.dot(tri_ref[...], onehot.astype(BF16), preferred_element_type=F32) + run[...]
    r1 = jnp.sum(jnp.where(oh1, rank_all, 0.0), axis=1, keepdims=True)
    r2 = jnp.sum(jnp.where(oh2, rank_all, 0.0), axis=1, keepdims=True)
    run[...] = run[...] + jnp.sum(onehot, axis=0, keepdims=True)
    cnt_ref[...] = run[...]
    vals = [(i1 - N_GROUPS).astype(F32), (i2 - N_GROUPS).astype(F32), r1, r2, g1, g2]
    packed = jnp.zeros(lg.shape, F32)
    for n, v in enumerate(vals):
        packed = jnp.where(lane == n, v, packed)
    rt_ref[...] = packed.T[0:8, :]


def _router_call(logits):
    n_tok = logits.shape[0]
    tri = (jnp.arange(TM)[:, None] > jnp.arange(TM)[None, :]).astype(BF16)
    return pl.pallas_call(
        _router_kernel,
        grid=(n_tok // TM,),
        in_specs=[pl.BlockSpec((TM, LANES), lambda i: (i, 0)),
                  pl.BlockSpec((TM, TM), lambda i: (0, 0))],
        out_specs=[pl.BlockSpec((8, TM), lambda i: (0, i)),
                   pl.BlockSpec((1, LANES), lambda i: (0, 0))],
        out_shape=[jax.ShapeDtypeStruct((8, n_tok), F32), jax.ShapeDtypeStruct((1, LANES), F32)],
        scratch_shapes=[pltpu.VMEM((1, LANES), F32)],
        compiler_params=_cparams(1),
        name="router",
    )(logits, tri)


def _dispatch_plan(rt, cnt):
    n_tok = rt.shape[1]
    eid = rt[0:2].astype(jnp.int32)
    rank = rt[2:4].astype(jnp.int32)
    gate = rt[4:6]
    counts = cnt[0, N_GROUPS:N_GROUPS + N_EXPERTS].astype(jnp.int32)
    padded = (counts + TMOE - 1) // TMOE * TMOE
    pad_end = jnp.cumsum(padded)
    pad_start = pad_end - padded
    experts = jnp.arange(N_EXPERTS, dtype=jnp.int32)[:, None, None]
    dest = jnp.sum(jnp.where(eid[None] == experts, pad_start[:, None, None], 0), axis=0) + rank
    n_assign = n_tok * TOP_K
    n_blocks = (n_assign + N_EXPERTS * (TMOE - 1) + TMOE - 1) // TMOE
    n_slots = n_blocks * TMOE
    tok = jnp.arange(n_tok, dtype=jnp.int32)
    slot_tok = jnp.zeros((n_slots,), jnp.int32).at[dest.reshape(n_assign)].set(jnp.concatenate([tok, tok]))
    block_start = jnp.arange(n_blocks, dtype=jnp.int32) * TMOE
    block_e = jnp.minimum(jnp.sum((block_start[:, None] >= pad_end[None, :]).astype(jnp.int32), axis=1),
                          N_EXPERTS - 1).astype(jnp.int32)
    n_used = (pad_end[-1:] // TMOE).astype(jnp.int32)
    return slot_tok, gate, block_e, n_used, dest


def _final_kernel(x_ref, y_ref, pmod_ref, g_ref, o_ref):
    x = x_ref[0] + pmod_ref[0, 0][5:6, :] * y_ref[0]
    o_ref[0] = _rms(x, g_ref[...], NORM_EPS)


def _final_call(x, y, pmod, g, s_len):
    bsz, t, d = x.shape
    return pl.pallas_call(
        _final_kernel,
        grid=(bsz, s_len // TM),
        in_specs=[_row_spec(d), _row_spec(d),
                  pl.BlockSpec((1, 1, 8, d), lambda b, i: (b, 0, 0, 0)),
                  _full_spec((1, d))],
        out_specs=_row_spec(d),
        out_shape=jax.ShapeDtypeStruct((bsz, s_len, d), F32),
        compiler_params=_cparams(2),
        name="final_norm",
    )(x, y, pmod, g.reshape(1, d))


def _take_cols(w, idx):
    wz = jnp.concatenate([w, jnp.zeros((w.shape[0], 1), w.dtype)], axis=1)
    return jnp.take(wz, jnp.asarray(idx, dtype=jnp.int32), axis=1).astype(BF16)


def _ab_layouts():
    zc = 1184
    cols = []
    for j in range(4):
        cols.append(j * 128 + _PAIR_PERM)
    for g in range(2):
        base = 512 + g * 64
        cols.append(base + np.concatenate([np.arange(0, 32), np.arange(0, 32), np.arange(32, 64), np.arange(32, 64)]))
    for g in range(2):
        base = 640 + g * 64
        cols.append(base + np.concatenate([np.arange(64), np.arange(64)]))
    cols.append(768 + np.arange(256))
    cols.append(1024 + np.arange(128))
    kr = 1152
    z32 = np.full((32,), zc)
    cols.append(np.concatenate([kr + np.arange(16), kr + np.arange(16), z32,
                                kr + 16 + np.arange(16), kr + 16 + np.arange(16), z32]))
    w1_idx = np.concatenate(cols)

    zq = 768
    uq = []
    z32q = np.full((32,), zq)
    for j in range(4):
        a, b = 2 * j * 96, (2 * j + 1) * 96
        uq.append(np.concatenate([a + np.arange(64), b + np.arange(64),
                                  a + 64 + np.arange(16), b + 64 + np.arange(16), z32q,
                                  a + 80 + np.arange(16), b + 80 + np.arange(16), z32q]))
    uq_idx = np.concatenate(uq)

    kn, mv = [], []
    for h in range(8):
        kn.append(h * 128 + np.arange(64))
        mv.append(h * 128 + 64 + np.arange(64))
    ukv_idx = np.concatenate(kn + mv)
    return w1_idx, uq_idx, ukv_idx


def _diff_layout():
    cols = []
    for part in range(2):
        for h in range(8):
            cols.append(part * 1024 + h * 128 + _PAIR_PERM)
    cols.append(2048 + np.arange(1024))
    return np.concatenate(cols)


def kernel(x, c, ctx, c_ctx, norm_mix, norm_ffn, ada_w, ada_b, ab_w_in, mla_q_norm, mla_w_uq, mla_kv_norm, mla_w_ukv, swa_sink, ab_w_out, diff_w_in, diff_lambda_q1, diff_lambda_k1, diff_lambda_q2, diff_lambda_k2, diff_subln, diff_w_out, router_group_w, router_group_b, router_expert_w, router_expert_b, expert_w1, expert_w3, expert_w2, final_norm):
    bsz, s_len, d = x.shape
    c_len = ctx.shape[1]
    depth = ada_w.shape[0]
    assert d == D_MODEL and c_len == TM and s_len % TQ == 0 and s_len % (2 * TK) == 0
    t = s_len + c_len

    xs = (x, ctx)
    tables = _rope_tables(s_len, c_len)
    c64, s64, c32, s32 = tables

    n_rows = (bsz + 1 + 7) // 8 * 8
    rows = jnp.concatenate([c, c_ctx[None, :], jnp.zeros((n_rows - bsz - 1, d), F32)], axis=0)
    mod_all = _ada_call(rows, ada_w, ada_b)
    mod_lat = mod_all[:, :bsz].reshape(depth, bsz, 1, 6, d)
    mod_ctx = jnp.broadcast_to(mod_all[:, bsz].reshape(depth, 1, 1, 6, d), (depth, bsz, 1, 6, d))
    mods = jnp.concatenate([mod_lat, mod_ctx], axis=2)
    mods = jnp.concatenate([mods, jnp.zeros((depth, bsz, 2, 2, d), F32)], axis=3)

    w1_idx, uq_idx, ukv_idx = _ab_layouts()
    diff_idx = _diff_layout()

    res = None
    for l in range(depth):
        j = l // 2
        mod = mods[l]
        if l % 2 == 0:
            weights = [_take_cols(ab_w_in[j], w1_idx), mla_q_norm[j].reshape(1, -1), mla_kv_norm[j].reshape(1, -1),
                       _take_cols(mla_w_uq[j], uq_idx), _take_cols(mla_w_ukv[j], ukv_idx)]
            xs, (sq, sk, sv, mq, mk, mv) = _proj_call("ab", xs, res, mod, norm_mix[l], weights,
                                                      [c64, s64, c32, s32], [512, 256, -256, 1024, 1024, -512])
            o_a = _swa_call(swa_sink[j], sq, sk, sv)
            o_b = _dense_attn_call("mla", mq, mk, mv)
            w_out = ab_w_out[j].astype(BF16)
            attn_outs, out_ws = [o_a, o_b], [w_out[:512], w_out[512:]]
        else:
            lambda_init = 0.8 - 0.6 * math.exp(-0.3 * l)
            weights = [_take_cols(diff_w_in[j], diff_idx)]
            xs, (dq, dk, dv) = _proj_call("diff", xs, res, mod, norm_mix[l], weights, [c64, s64], [1024, 1024, -1024])
            extra = [diff_lambda_q1[j].reshape(1, -1), diff_lambda_k1[j].reshape(1, -1),
                     diff_lambda_q2[j].reshape(1, -1), diff_lambda_k2[j].reshape(1, -1),
                     diff_subln[j].reshape(1, -1)]
            o_d = _dense_attn_call("diff", dq, dk, dv, extra=extra, lambda_init=lambda_init)
            attn_outs, out_ws = [o_d], [diff_w_out[j].astype(BF16)]

        wr = jnp.concatenate([router_group_w[l], router_expert_w[l],
                              jnp.zeros((d, LANES - N_GROUPS - N_EXPERTS), F32)], axis=1)
        wr_hi = wr.astype(BF16)
        wr = jnp.concatenate([wr_hi, (wr - wr_hi.astype(F32)).astype(BF16)], axis=1)
        br = jnp.concatenate([router_group_b[l], router_expert_b[l],
                              jnp.zeros((LANES - N_GROUPS - N_EXPERTS,), F32)]).reshape(1, LANES)
        xs, h2, logits = _out_proj_call(attn_outs, out_ws, xs, mod, norm_ffn[l], wr, br)

        n_tok = bsz * t
        rt, cnt = _router_call(logits.reshape(n_tok, LANES))
        slot_tok, gate, block_e, n_used, dest = _dispatch_plan(rt, cnt)
        xb = jnp.take(h2.reshape(n_tok, d), slot_tok, axis=0)
        yb = _moe_call(l, block_e, n_used, xb, expert_w1, expert_w3, expert_w2)
        y = (gate[0][:, None] * jnp.take(yb, dest[0], axis=0)
             + gate[1][:, None] * jnp.take(yb, dest[1], axis=0)).reshape(bsz, t, d)
        res = (y, mod)

    return _final_call(xs, res[0], res[1], final_norm, s_len)
```

```python
import functools
import math

import numpy as np
import jax
import jax.numpy as jnp
from jax import lax
from jax.experimental import pallas as pl
from jax.experimental.pallas import tpu as pltpu

F32 = jnp.float32
BF16 = jnp.bfloat16

D_MODEL = 1024
GRID_W = 64
ROPE_BASE = 10000.0
NORM_EPS = 1e-6
DIFF_EPS = 1e-5
NEG = -1e30

SWA_WINDOW = 128
MLA_SCALE = (64 + 32) ** -0.5
HEAD_SCALE = 64 ** -0.5
LOG2E = math.log2(math.e)

N_GROUPS = 4
EXPERTS_PER_GROUP = 8
N_EXPERTS = 32
TOP_K = 2
EXPERT_FF = 512

LANES = 128
TM = 256
TK = 256
TQ = 1024
TMOE = 256
VMEM_LIMIT = 56 * 1024 * 1024


def _cparams(n_axes):
    return pltpu.CompilerParams(dimension_semantics=("arbitrary",) * n_axes,
                                vmem_limit_bytes=VMEM_LIMIT)


def _rms(x, g, eps):
    return x * lax.rsqrt(jnp.mean(x * x, axis=-1, keepdims=True) + eps) * g


def _rope_block(x, c, s):
    return x * c + pltpu.roll(x, 64, 1) * s


_PAIR_PERM = np.concatenate([np.arange(0, 32), np.arange(64, 96), np.arange(32, 64), np.arange(96, 128)])


def _rope_tables(s_len, c_len):
    rows = s_len // GRID_W
    row = jnp.repeat(jnp.arange(rows, dtype=F32), GRID_W)
    col = jnp.tile(jnp.arange(GRID_W, dtype=F32), rows)

    def tab(dim):
        nf = dim // 4
        inv = ROPE_BASE ** (-jnp.arange(nf, dtype=F32) / nf)
        ang = jnp.concatenate([row[:, None] * inv, col[:, None] * inv], axis=-1)
        return jnp.cos(ang), jnp.sin(ang)

    cos64, sin64 = tab(64)
    cos32, sin32 = tab(32)
    c64 = jnp.concatenate([cos64] * 4, axis=-1)
    s64 = jnp.concatenate([-sin64, -sin64, sin64, sin64], axis=-1)
    one = jnp.ones((s_len, 32), F32)
    zero = jnp.zeros((s_len, 32), F32)
    c32 = jnp.concatenate([cos32, cos32, one, cos32, cos32, one], axis=-1)
    s32 = jnp.concatenate([-sin32, -sin32, zero, sin32, sin32, zero], axis=-1)

    def ext(t, fill):
        return jnp.concatenate([t, jnp.full((c_len, LANES), fill, F32)], axis=0)

    return ext(c64, 1.0), ext(s64, 0.0), ext(c32, 1.0), ext(s32, 0.0)


def _ada_kernel(x_ref, w_ref, b_ref, o_ref):
    x = x_ref[...]
    sx = x * jax.nn.sigmoid(x)
    o_ref[0] = jnp.dot(sx.astype(BF16), w_ref[0].astype(BF16), preferred_element_type=F32) + b_ref[0]


def _ada_call(rows, ada_w, ada_b):
    depth, d, n6 = ada_w.shape
    r = rows.shape[0]
    tn = 1536
    return pl.pallas_call(
        _ada_kernel,
        grid=(depth, n6 // tn),
        in_specs=[pl.BlockSpec((r, d), lambda l, j: (0, 0)),
                  pl.BlockSpec((1, d, tn), lambda l, j: (l, 0, j)),
                  pl.BlockSpec((1, 1, tn), lambda l, j: (l, 0, j))],
        out_specs=pl.BlockSpec((1, r, tn), lambda l, j: (l, 0, j)),
        out_shape=jax.ShapeDtypeStruct((depth, r, n6), F32),
        compiler_params=_cparams(2),
        name="ada_mod",
    )(rows, ada_w, ada_b.reshape(depth, 1, n6))


def _first_layer_x(x_ref, ctx_ref, nlat):
    return jnp.where(pl.program_id(1) < nlat, x_ref[0], ctx_ref[0])


def _moe_residual(x_ref, ya_ref, yb_ref, gate_ref, pmod_ref):
    gt = gate_ref[0]
    y = gt[:, 0:1] * ya_ref[0] + gt[:, 1:2] * yb_ref[0]
    return x_ref[0] + pmod_ref[0, 0][5:6, :] * y


def _prenorm(has_res, nlat, x_refs, mod_ref, g_ref, xo_ref):
    if has_res:
        x = _moe_residual(*x_refs)
        xo_ref[0] = x
    else:
        x = _first_layer_x(x_refs[0], x_refs[1], nlat)
    m = mod_ref[0, 0]
    return _rms(x, g_ref[...], NORM_EPS) * (1.0 + m[1:2, :]) + m[0:1, :]


def _ab_proj_kernel(has_res, nlat, *refs):
    n_x = 5 if has_res else 2
    x_refs, refs = refs[:n_x], refs[n_x:]
    (mod_ref, g_ref, w1_ref, qn_ref, kvn_ref, wuq_ref, wukv_ref,
     c64_ref, s64_ref, c32_ref, s32_ref) = refs[:11]
    outs = refs[11:]
    if has_res:
        xo_ref, outs = outs[0], outs[1:]
    else:
        xo_ref = None
    sq_ref, sk_ref, sv_ref, mq_ref, mk_ref, mv_ref = outs

    h = _prenorm(has_res, nlat, x_refs, mod_ref, g_ref, xo_ref)
    p = jnp.dot(h.astype(BF16), w1_ref[...], preferred_element_type=F32)
    c64, s64, c32, s32 = c64_ref[...], s64_ref[...], c32_ref[...], s32_ref[...]
    for j in range(4):
        blk = _rope_block(p[:, j * 128:(j + 1) * 128], c64, s64)
        sq_ref[0, :, j * 128:(j + 1) * 128] = (blk * (HEAD_SCALE * LOG2E)).astype(BF16)
    for j in range(2):
        blk = _rope_block(p[:, 512 + j * 128:512 + (j + 1) * 128], c64, s64)
        sk_ref[0, :, j * 128:(j + 1) * 128] = blk.astype(BF16)
    sv_ref[0, 0] = p[:, 768:1024].T.astype(BF16)
    cq = p[:, 1024:1280]
    ckv = p[:, 1280:1408]
    kr = _rope_block(p[:, 1408:1536], c32, s32).astype(BF16)
    qm = jnp.dot(_rms(cq, qn_ref[...], NORM_EPS).astype(BF16), wuq_ref[...], preferred_element_type=F32)
    kv = jnp.dot(_rms(ckv, kvn_ref[...], NORM_EPS).astype(BF16), wukv_ref[...], preferred_element_type=F32)
    for j in range(4):
        mq_ref[0, :, j * 256:j * 256 + 128] = (qm[:, j * 256:j * 256 + 128] * (MLA_SCALE * LOG2E)).astype(BF16)
        rr = _rope_block(qm[:, j * 256 + 128:(j + 1) * 256], c32, s32)
        mq_ref[0, :, j * 256 + 128:(j + 1) * 256] = (rr * (MLA_SCALE * LOG2E)).astype(BF16)
        mk_ref[0, :, j * 256:j * 256 + 128] = kv[:, j * 128:(j + 1) * 128].astype(BF16)
        mk_ref[0, :, j * 256 + 128:(j + 1) * 256] = kr
    mv_ref[0, 0] = kv[:, 512:1024].T.astype(BF16)


def _diff_proj_kernel(has_res, nlat, *refs):
    n_x = 5 if has_res else 2
    x_refs, refs = refs[:n_x], refs[n_x:]
    mod_ref, g_ref, w_ref, c64_ref, s64_ref = refs[:5]
    outs = refs[5:]
    if has_res:
        xo_ref, outs = outs[0], outs[1:]
    else:
        xo_ref = None
    q_ref, k_ref, v_ref = outs
    h = _prenorm(has_res, nlat, x_refs, mod_ref, g_ref, xo_ref)
    p = jnp.dot(h.astype(BF16), w_ref[...], preferred_element_type=F32)
    c64, s64 = c64_ref[...], s64_ref[...]
    for j in range(8):
        blk = _rope_block(p[:, j * 128:(j + 1) * 128], c64, s64)
        q_ref[0, :, j * 128:(j + 1) * 128] = (blk * (HEAD_SCALE * LOG2E)).astype(BF16)
        blk = _rope_block(p[:, 1024 + j * 128:1024 + (j + 1) * 128], c64, s64)
        k_ref[0, :, j * 128:(j + 1) * 128] = blk.astype(BF16)
    v_ref[0, 0] = p[:, 2048:3072].T.astype(BF16)


def _row_spec(width):
    return pl.BlockSpec((1, TM, width), lambda b, i: (b, i, 0))


def _mod_spec(nlat):
    return pl.BlockSpec((1, 1, 8, D_MODEL), lambda b, i: (b, i // nlat, 0, 0))


def _full_spec(shape):
    nd = len(shape)
    return pl.BlockSpec(shape, lambda b, i: (0,) * nd)


def _tab_spec():
    return pl.BlockSpec((TM, LANES), lambda b, i: (i, 0))


def _split_specs(nlat, d):
    return [pl.BlockSpec((1, TM, d), lambda b, i: (b, jnp.minimum(i, nlat - 1), 0)),
            pl.BlockSpec((1, TM, d), lambda b, i: (b, 0, 0))]


def _proj_call(kind, x, res, mod, g, weights, tables, out_widths):
    has_res = res is not None
    if has_res:
        bsz, t, d = x.shape
    else:
        bsz, t, d = x[0].shape[0], x[0].shape[1] + x[1].shape[1], x[0].shape[2]
    nt = t // TM
    nlat = nt - 1
    if has_res:
        ya, yb, gates, pmod = res
        ins = [x, ya, yb, gates, pmod]
        specs = [_row_spec(d), _row_spec(d), _row_spec(d), _row_spec(TOP_K), _mod_spec(nlat)]
    else:
        ins = list(x)
        specs = _split_specs(nlat, d)
    ins += [mod, g.reshape(1, d)]
    specs += [_mod_spec(nlat), _full_spec((1, d))]
    for w in weights:
        ins.append(w)
        specs.append(_full_spec(w.shape))
    for tb in tables:
        ins.append(tb)
        specs.append(_tab_spec())
    out_shapes, out_specs = [], []
    if has_res:
        out_shapes.append(jax.ShapeDtypeStruct((bsz, t, d), F32))
        out_specs.append(_row_spec(d))
    for w in out_widths:
        if w < 0:
            out_shapes.append(jax.ShapeDtypeStruct((bsz, nt, -w, TM), BF16))
            out_specs.append(pl.BlockSpec((1, 1, -w, TM), lambda b, i: (b, i, 0, 0)))
        else:
            out_shapes.append(jax.ShapeDtypeStruct((bsz, t, w), BF16))
            out_specs.append(_row_spec(w))
    body = _ab_proj_kernel if kind == "ab" else _diff_proj_kernel
    outs = pl.pallas_call(
        functools.partial(body, has_res, nlat),
        grid=(bsz, nt),
        in_specs=specs,
        out_specs=out_specs,
        out_shape=out_shapes,
        compiler_params=_cparams(2),
        name=kind + "_proj",
    )(*ins)
    if has_res:
        return outs[0], outs[1:]
    return x, outs


def _pair_masks(mode, lane):
    if mode == "mla":
        in_a = (lane < 64) | ((lane >= 128) & (lane < 144)) | ((lane >= 192) & (lane < 208))
        in_b = ((lane >= 64) & (lane < 128)) | ((lane >= 144) & (lane < 160)) | ((lane >= 208) & (lane < 224))
    else:
        in_a = (lane < 32) | ((lane >= 64) & (lane < 96))
        in_b = ((lane >= 32) & (lane < 64)) | (lane >= 96)
    return in_a, in_b


def _dense_attn_kernel(mode, lambda_init, nlat, *refs):
    n_in = 8 if mode == "diff" else 3
    q_ref, k_ref, vt_ref = refs[:3]
    o_ref = refs[n_in]
    s_bufs = refs[n_in + 1:n_in + 3]
    p_bufs = refs[n_in + 3:n_in + 5]
    acc_buf, m_buf, l_buf, a_buf = refs[n_in + 5:]
    i = pl.program_id(2)
    qt = q_ref[0].T
    width, tq = qt.shape
    row = lax.broadcasted_iota(jnp.int32, (width, 1), 0)
    in_a, in_b = _pair_masks(mode, row)
    zero = jnp.zeros_like(qt)
    q2 = jnp.concatenate([jnp.where(in_a, qt, zero), jnp.where(in_b, qt, zero)], axis=1)
    chunks = [(nlat, 1)] + [(2 * c, 2) for c in range(nlat // 2)]
    n_lat_q = (nlat * TK) // tq

    def scores(chunk, s_buf):
        k0, n = chunk
        s_buf[0:n * TK, :] = jnp.dot(k_ref[0, k0 * TK:(k0 + n) * TK, :], q2, preferred_element_type=F32)

    def pv(chunk, p_buf):
        k0, n = chunk
        out = jnp.dot(vt_ref[0, k0], p_buf[0:TK, :], preferred_element_type=F32)
        for r in range(1, n):
            out = out + jnp.dot(vt_ref[0, k0 + r], p_buf[r * TK:(r + 1) * TK, :], preferred_element_type=F32)
        return out

    def softmax(chunk, s_buf, p_buf, first):
        n = chunk[1]
        s = s_buf[0:n * TK, :]
        mx = jnp.max(s, axis=0, keepdims=True)
        if first:
            mn = mx
        else:
            m = m_buf[...]
            mn = jnp.maximum(m, mx)
            a_buf[...] = jnp.exp2(m - mn)
        p = jnp.exp2(s - mn)
        ps = jnp.sum(p, axis=0, keepdims=True)
        l_buf[...] = ps if first else a_buf[...] * l_buf[...] + ps
        m_buf[...] = mn
        p_buf[0:n * TK, :] = p.astype(BF16)

    def accumulate(chunk, p_buf, first):
        if first:
            acc_buf[...] = pv(chunk, p_buf)
        else:
            acc_buf[...] = a_buf[...] * acc_buf[...] + pv(chunk, p_buf)

    def pipeline(chs):
        scores(chs[0], s_bufs[0])
        for c, ch in enumerate(chs):
            if c >= 2:
                accumulate(chs[c - 1], p_bufs[(c - 1) % 2], first=False)
            if c + 1 < len(chs):
                scores(chs[c + 1], s_bufs[(c + 1) % 2])
            softmax(ch, s_bufs[c % 2], p_bufs[c % 2], first=c == 0)
            if c == 1:
                accumulate(chs[0], p_bufs[0], first=True)
        last = len(chs) - 1
        if last == 0:
            accumulate(chs[0], p_bufs[0], first=True)
        else:
            accumulate(chs[last], p_bufs[last % 2], first=False)

    @pl.when(i < n_lat_q)
    def _():
        pipeline(chunks)

    @pl.when(i >= n_lat_q)
    def _():
        pipeline(chunks[:1])

    o2 = acc_buf[...] * (1.0 / l_buf[...])
    oa, ob = o2[:, :tq], o2[:, tq:]
    if mode == "diff":
        lq1_ref, lk1_ref, lq2_ref, lk2_ref, sub_ref = refs[3:8]
        lam = (jnp.exp(jnp.sum(lq1_ref[...] * lk1_ref[...], axis=1, keepdims=True))
               - jnp.exp(jnp.sum(lq2_ref[...] * lk2_ref[...], axis=1, keepdims=True)) + lambda_init)
        o = (oa - lam * ob).T
        o = _rms(o, sub_ref[...], DIFF_EPS) * (1.0 - lambda_init)
    else:
        vrow = lax.broadcasted_iota(jnp.int32, (LANES, 1), 0)
        o = jnp.where(vrow < 64, oa, ob).T
    o_ref[0] = o.astype(BF16)


def _dense_attn_call(mode, q, k, vt, extra=(), lambda_init=0.0):
    bsz, t, qtot = q.shape
    width = 256 if mode == "mla" else 128
    npairs = qtot // width
    nq = pl.cdiv(t, TQ)
    ins = [q, k, vt]
    specs = [pl.BlockSpec((1, TQ, width), lambda b, j, i: (b, i, j)),
             pl.BlockSpec((1, t, width), lambda b, j, i: (b, 0, j)),
             pl.BlockSpec((1, t // TK, LANES, TK), lambda b, j, i: (b, 0, j, 0))]
    for e in extra:
        ins.append(e)
        specs.append(pl.BlockSpec(e.shape, lambda b, j, i: (0, 0)))
    return pl.pallas_call(
        functools.partial(_dense_attn_kernel, mode, lambda_init, t // TK - 1),
        grid=(bsz, npairs, nq),
        in_specs=specs,
        out_specs=pl.BlockSpec((1, TQ, LANES), lambda b, j, i: (b, i, j)),
        out_shape=jax.ShapeDtypeStruct((bsz, t, npairs * LANES), BF16),
        scratch_shapes=[pltpu.VMEM((2 * TK, 2 * TQ), F32), pltpu.VMEM((2 * TK, 2 * TQ), F32),
                        pltpu.VMEM((2 * TK, 2 * TQ), BF16), pltpu.VMEM((2 * TK, 2 * TQ), BF16),
                        pltpu.VMEM((LANES, 2 * TQ), F32), pltpu.VMEM((1, 2 * TQ), F32),
                        pltpu.VMEM((1, 2 * TQ), F32), pltpu.VMEM((1, 2 * TQ), F32)],
        compiler_params=_cparams(3),
        name=mode + "_attn",
    )(*ins)


def _swa_kernel(nlat, sink_ref, q_ref, k_ref, vt_ref, o_ref):
    g = pl.program_id(1)
    i = pl.program_id(2)
    s_len = nlat * TM
    half = TM // 2
    q = q_ref[0]
    row = lax.broadcasted_iota(jnp.int32, (LANES, 1), 0)
    in_a, in_b = _pair_masks("pair", row)
    cols = []
    for pr in range(2):
        qt = q[:, pr * LANES:(pr + 1) * LANES].T
        zero = jnp.zeros_like(qt)
        cols += [jnp.where(in_a, qt, zero), jnp.where(in_b, qt, zero)]
    q4 = jnp.concatenate(cols, axis=1)

    start0 = pl.multiple_of(jnp.maximum(i * TM - half, 0), half)
    start1 = pl.multiple_of(i * TM, TM)
    start2 = pl.multiple_of(jnp.minimum((i + 1) * TM, s_len + half), half)
    k_cat = jnp.concatenate([k_ref[0, pl.ds(start0, half), :], k_ref[0, pl.ds(start1, TM), :],
                             k_ref[0, pl.ds(start2, half), :], k_ref[0, s_len:s_len + TM, :]], axis=0)
    b0 = jnp.maximum(i - 1, 0)
    b2 = jnp.minimum(i + 1, nlat)
    vt_cat = jnp.concatenate([vt_ref[0, b0][:, half:], vt_ref[0, i], vt_ref[0, b2][:, :half],
                              vt_ref[0, nlat]], axis=1)

    r = lax.broadcasted_iota(jnp.int32, (2 * TM, 1), 0)
    far = -4 * SWA_WINDOW
    lat = i < nlat
    pos0 = jnp.where(lat & (i >= 1), start0 + r, far)
    pos1 = jnp.where(lat, start1 + r - half, far)
    pos2 = jnp.where(lat & (i + 1 < nlat), start2 + r - half - TM, far)
    kpos = jnp.where(r < half, pos0, jnp.where(r < half + TM, pos1, pos2))
    qpos = i * TM + lax.broadcasted_iota(jnp.int32, (1, TM), 1)
    band = jnp.abs(qpos - kpos) <= SWA_WINDOW
    band4 = jnp.concatenate([band] * 4, axis=1)

    s = jnp.dot(k_cat, q4, preferred_element_type=F32)
    s_loc = jnp.where(band4, s[:2 * TM], NEG)
    s_ctx = s[2 * TM:]
    sink = jnp.concatenate([jnp.full((1, TM), sink_ref[4 * g + h], F32) for h in range(4)], axis=1) * LOG2E
    m = jnp.maximum(jnp.maximum(jnp.max(s_loc, axis=0, keepdims=True), jnp.max(s_ctx, axis=0, keepdims=True)), sink)
    p_loc = jnp.exp2(s_loc - m)
    p_ctx = jnp.exp2(s_ctx - m)
    l = jnp.sum(p_loc, axis=0, keepdims=True) + jnp.sum(p_ctx, axis=0, keepdims=True) + jnp.exp2(sink - m)
    p = jnp.concatenate([p_loc, p_ctx], axis=0).astype(BF16)
    o4 = jnp.dot(vt_cat, p, preferred_element_type=F32) * (1.0 / l)
    vrow = lax.broadcasted_iota(jnp.int32, (LANES, 1), 0)
    for pr in range(2):
        oa = o4[:, (2 * pr) * TM:(2 * pr + 1) * TM]
        ob = o4[:, (2 * pr + 1) * TM:(2 * pr + 2) * TM]
        o_ref[0, :, pr * LANES:(pr + 1) * LANES] = jnp.where(vrow < 64, oa, ob).T.astype(BF16)


def _swa_call(sink, q, k, vt):
    bsz, t, qtot = q.shape
    nkv = k.shape[-1] // LANES
    nt = t // TM
    grid_spec = pltpu.PrefetchScalarGridSpec(
        num_scalar_prefetch=1,
        grid=(bsz, nkv, nt),
        in_specs=[pl.BlockSpec((1, TM, 2 * LANES), lambda b, g, i, s: (b, i, g)),
                  pl.BlockSpec((1, t, LANES), lambda b, g, i, s: (b, 0, g)),
                  pl.BlockSpec((1, nt, LANES, TM), lambda b, g, i, s: (b, 0, g, 0))],
        out_specs=pl.BlockSpec((1, TM, 2 * LANES), lambda b, g, i, s: (b, i, g)),
    )
    return pl.pallas_call(
        functools.partial(_swa_kernel, nt - 1),
        grid_spec=grid_spec,
        out_shape=jax.ShapeDtypeStruct((bsz, t, qtot), BF16),
        compiler_params=_cparams(3),
        name="swa_attn",
    )(sink, q, k, vt)


def _out_proj_kernel(n_o, split, nlat, *refs):
    o_refs = refs[:n_o]
    w_refs = refs[n_o:2 * n_o]
    refs = refs[2 * n_o:]
    if split:
        x_in = _first_layer_x(refs[0], refs[1], nlat)
        refs = refs[2:]
    else:
        x_in = refs[0][0]
        refs = refs[1:]
    mod_ref, g_ref, wr_ref, br_ref, xo_ref, h_ref, lg_ref = refs
    acc = jnp.dot(o_refs[0][0], w_refs[0][...], preferred_element_type=F32)
    for n in range(1, n_o):
        acc = acc + jnp.dot(o_refs[n][0], w_refs[n][...], preferred_element_type=F32)
    m = mod_ref[0, 0]
    x = x_in + m[2:3, :] * acc
    xo_ref[0] = x
    h = _rms(x, g_ref[...], NORM_EPS) * (1.0 + m[4:5, :]) + m[3:4, :]
    hi = h.astype(BF16)
    lo = (h - hi.astype(F32)).astype(BF16)
    h_ref[0] = hi
    both = jnp.dot(hi, wr_ref[...], preferred_element_type=F32)
    lg_ref[0] = (both[:, :LANES] + both[:, LANES:]
                 + jnp.dot(lo, wr_ref[:, :LANES], preferred_element_type=F32) + br_ref[...])


def _out_proj_call(os_, ws, x, mod, g, wr, br):
    split = isinstance(x, (tuple, list))
    if split:
        bsz, t, d = x[0].shape[0], x[0].shape[1] + x[1].shape[1], x[0].shape[2]
    else:
        bsz, t, d = x.shape
    nt = t // TM
    nlat = nt - 1
    n_o = len(os_)
    x_ins = list(x) if split else [x]
    x_specs = _split_specs(nlat, d) if split else [_row_spec(d)]
    ins = list(os_) + list(ws) + x_ins + [mod, g.reshape(1, d), wr, br]
    specs = ([_row_spec(o.shape[-1]) for o in os_] + [_full_spec(w.shape) for w in ws]
             + x_specs + [_mod_spec(nlat), _full_spec((1, d)), _full_spec(wr.shape), _full_spec(br.shape)])
    return pl.pallas_call(
        functools.partial(_out_proj_kernel, n_o, split, nlat),
        grid=(bsz, nt),
        in_specs=specs,
        out_specs=[_row_spec(d), _row_spec(d), _row_spec(LANES)],
        out_shape=[jax.ShapeDtypeStruct((bsz, t, d), F32),
                   jax.ShapeDtypeStruct((bsz, t, d), BF16),
                   jax.ShapeDtypeStruct((bsz, t, LANES), F32)],
        compiler_params=_cparams(2),
        name="out_proj",
    )(*ins)


def _moe_kernel(be_ref, nu_ref, x_ref, w1_ref, w3_ref, w2_ref, o_ref, w1c, w3c, w2c):
    i = pl.program_id(0)
    e = be_ref[i]

    @pl.when((i == 0) | (e != be_ref[jnp.maximum(i - 1, 0)]))
    def _():
        w1c[...] = w1_ref[0, 0].astype(BF16)
        w3c[...] = w3_ref[0, 0].astype(BF16)
        w2c[...] = w2_ref[0, 0].astype(BF16)

    @pl.when(i < nu_ref[0])
    def _():
        x = x_ref[...]
        a = jnp.dot(x, w1c[...], preferred_element_type=F32)
        b = jnp.dot(x, w3c[...], preferred_element_type=F32)
        hmid = (a * jax.nn.sigmoid(a)) * b
        o_ref[...] = jnp.dot(hmid.astype(BF16), w2c[...], preferred_element_type=F32)

    @pl.when(i >= nu_ref[0])
    def _():
        o_ref[...] = jnp.zeros_like(o_ref)


def _moe_call(layer, block_e, n_used, xb, w1, w3, w2):
    n_slots, d = xb.shape
    n_blocks = n_slots // TMOE
    ff = w1.shape[-1]
    grid_spec = pltpu.PrefetchScalarGridSpec(
        num_scalar_prefetch=2,
        grid=(n_blocks,),
        in_specs=[pl.BlockSpec((TMOE, d), lambda i, be, nu: (i, 0)),
                  pl.BlockSpec((1, 1, d, ff), lambda i, be, nu: (layer, be[i], 0, 0)),
                  pl.BlockSpec((1, 1, d, ff), lambda i, be, nu: (layer, be[i], 0, 0)),
                  pl.BlockSpec((1, 1, ff, d), lambda i, be, nu: (layer, be[i], 0, 0))],
        out_specs=pl.BlockSpec((TMOE, d), lambda i, be, nu: (i, 0)),
        scratch_shapes=[pltpu.VMEM((d, ff), BF16), pltpu.VMEM((d, ff), BF16), pltpu.VMEM((ff, d), BF16)],
    )
    return pl.pallas_call(
        _moe_kernel,
        grid_spec=grid_spec,
        out_shape=jax.ShapeDtypeStruct((n_slots, d), F32),
        compiler_params=_cparams(1),
        name="moe_experts",
    )(block_e, n_used, xb, w1, w3, w2)


def _router_kernel(lg_ref, tri_ref, rt_ref, cnt_ref, run):
    i = pl.program_id(0)

    @pl.when(i == 0)
    def _():
        run[...] = jnp.zeros_like(run)

    lg = lg_ref[...]
    lane = lax.broadcasted_iota(jnp.int32, lg.shape, 1)
    gmask = lane < N_GROUPS
    gl = jnp.where(gmask, lg, NEG)
    gmax = jnp.max(gl, axis=1, keepdims=True)
    grp = jnp.min(jnp.where(gl == gmax, lane, LANES), axis=1, keepdims=True)
    p_grp = 1.0 / jnp.sum(jnp.where(gmask, jnp.exp(lg - gmax), 0.0), axis=1, keepdims=True)
    first = N_GROUPS + EXPERTS_PER_GROUP * grp
    emask = (lane >= first) & (lane < first + EXPERTS_PER_GROUP)
    el = jnp.where(emask, lg, NEG)
    e1 = jnp.max(el, axis=1, keepdims=True)
    i1 = jnp.min(jnp.where(el == e1, lane, LANES), axis=1, keepdims=True)
    el2 = jnp.where(lane == i1, NEG, el)
    e2 = jnp.max(el2, axis=1, keepdims=True)
    i2 = jnp.min(jnp.where(el2 == e2, lane, LANES), axis=1, keepdims=True)
    tt = jnp.exp(e2 - e1)
    g1 = p_grp / (1.0 + tt)
    g2 = g1 * tt
    oh1 = lane == i1
    oh2 = lane == i2
    onehot = jnp.where(oh1 | oh2, 1.0, 0.0)
    rank_all = jnp.dot(tri_ref[...], onehot.astype(BF16), preferred_element_type=F32) + run[...]
    r1 = jnp.sum(jnp.where(oh1, rank_all, 0.0), axis=1, keepdims=True)
    r2 = jnp.sum(jnp.where(oh2, rank_all, 0.0), axis=1, keepdims=True)
    run[...] = run[...] + jnp.sum(onehot, axis=0, keepdims=True)
    cnt_ref[...] = run[...]
    vals = [(i1 - N_GROUPS).astype(F32), (i2 - N_GROUPS).astype(F32), r1, r2, g1, g2]
    packed = jnp.zeros(lg.shape, F32)
    for n, v in enumerate(vals):
        packed = jnp.where(lane == n, v, packed)
    rt_ref[...] = packed.T[0:8, :]


def _router_call(logits):
    n_tok = logits.shape[0]
    tri = (jnp.arange(TM)[:, None] > jnp.arange(TM)[None, :]).astype(BF16)
    return pl.pallas_call(
        _router_kernel,
        grid=(n_tok // TM,),
        in_specs=[pl.BlockSpec((TM, LANES), lambda i: (i, 0)),
                  pl.BlockSpec((TM, TM), lambda i: (0, 0))],
        out_specs=[pl.BlockSpec((8, TM), lambda i: (0, i)),
                   pl.BlockSpec((1, LANES), lambda i: (0, 0))],
        out_shape=[jax.ShapeDtypeStruct((8, n_tok), F32), jax.ShapeDtypeStruct((1, LANES), F32)],
        scratch_shapes=[pltpu.VMEM((1, LANES), F32)],
        compiler_params=_cparams(1),
        name="router",
    )(logits, tri)


def _dispatch_plan(rt, cnt):
    n_tok = rt.shape[1]
    eid = rt[0:2].astype(jnp.int32)
    rank = rt[2:4].astype(jnp.int32)
    gate = rt[4:6]
    counts = cnt[0, N_GROUPS:N_GROUPS + N_EXPERTS].astype(jnp.int32)
    padded = (counts + TMOE - 1) // TMOE * TMOE
    pad_end = jnp.cumsum(padded)
    pad_start = pad_end - padded
    experts = jnp.arange(N_EXPERTS, dtype=jnp.int32)[:, None, None]
    dest = jnp.sum(jnp.where(eid[None] == experts, pad_start[:, None, None], 0), axis=0) + rank
    n_assign = n_tok * TOP_K
    n_blocks = (n_assign + N_EXPERTS * (TMOE - 1) + TMOE - 1) // TMOE
    n_slots = n_blocks * TMOE
    tok = jnp.arange(n_tok, dtype=jnp.int32)
    slot_tok = jnp.zeros((n_slots,), jnp.int32).at[dest.reshape(n_assign)].set(jnp.concatenate([tok, tok]))
    block_start = jnp.arange(n_blocks, dtype=jnp.int32) * TMOE
    block_e = jnp.minimum(jnp.sum((block_start[:, None] >= pad_end[None, :]).astype(jnp.int32), axis=1),
                          N_EXPERTS - 1).astype(jnp.int32)
    n_used = (pad_end[-1:] // TMOE).astype(jnp.int32)
    return slot_tok, gate, block_e, n_used, dest


def _final_kernel(x_ref, ya_ref, yb_ref, gate_ref, pmod_ref, g_ref, o_ref):
    x = _moe_residual(x_ref, ya_ref, yb_ref, gate_ref, pmod_ref)
    o_ref[0] = _rms(x, g_ref[...], NORM_EPS)


def _final_call(x, res, g, s_len):
    bsz, t, d = x.shape
    ya, yb, gates, pmod = res
    return pl.pallas_call(
        _final_kernel,
        grid=(bsz, s_len // TM),
        in_specs=[_row_spec(d), _row_spec(d), _row_spec(d), _row_spec(TOP_K),
                  pl.BlockSpec((1, 1, 8, d), lambda b, i: (b, 0, 0, 0)),
                  _full_spec((1, d))],
        out_specs=_row_spec(d),
        out_shape=jax.ShapeDtypeStruct((bsz, s_len, d), F32),
        compiler_params=_cparams(2),
        name="final_norm",
    )(x, ya, yb, gates, pmod, g.reshape(1, d))


def _take_cols(w, idx):
    wz = jnp.concatenate([w, jnp.zeros((w.shape[0], 1), w.dtype)], axis=1)
    return jnp.take(wz, jnp.asarray(idx, dtype=jnp.int32), axis=1).astype(BF16)


def _ab_layouts():
    zc = 1184
    cols = []
    for j in range(4):
        cols.append(j * 128 + _PAIR_PERM)
    for g in range(2):
        base = 512 + g * 64
        cols.append(base + np.concatenate([np.arange(0, 32), np.arange(0, 32), np.arange(32, 64), np.arange(32, 64)]))
    for g in range(2):
        base = 640 + g * 64
        cols.append(base + np.concatenate([np.arange(64), np.arange(64)]))
    cols.append(768 + np.arange(256))
    cols.append(1024 + np.arange(128))
    kr = 1152
    z32 = np.full((32,), zc)
    cols.append(np.concatenate([kr + np.arange(16), kr + np.arange(16), z32,
                                kr + 16 + np.arange(16), kr + 16 + np.arange(16), z32]))
    w1_idx = np.concatenate(cols)

    zq = 768
    uq = []
    z32q = np.full((32,), zq)
    for j in range(4):
        a, b = 2 * j * 96, (2 * j + 1) * 96
        uq.append(np.concatenate([a + np.arange(64), b + np.arange(64),
                                  a + 64 + np.arange(16), b + 64 + np.arange(16), z32q,
                                  a + 80 + np.arange(16), b + 80 + np.arange(16), z32q]))
    uq_idx = np.concatenate(uq)

    kn, mv = [], []
    for h in range(8):
        kn.append(h * 128 + np.arange(64))
        mv.append(h * 128 + 64 + np.arange(64))
    ukv_idx = np.concatenate(kn + mv)
    return w1_idx, uq_idx, ukv_idx


def _diff_layout():
    cols = []
    for part in range(2):
        for h in range(8):
            cols.append(part * 1024 + h * 128 + _PAIR_PERM)
    cols.append(2048 + np.arange(1024))
    return np.concatenate(cols)


def kernel(x, c, ctx, c_ctx, norm_mix, norm_ffn, ada_w, ada_b, ab_w_in, mla_q_norm, mla_w_uq, mla_kv_norm, mla_w_ukv, swa_sink, ab_w_out, diff_w_in, diff_lambda_q1, diff_lambda_k1, diff_lambda_q2, diff_lambda_k2, diff_subln, diff_w_out, router_group_w, router_group_b, router_expert_w, router_expert_b, expert_w1, expert_w3, expert_w2, final_norm):
    bsz, s_len, d = x.shape
    c_len = ctx.shape[1]
    depth = ada_w.shape[0]
    assert d == D_MODEL and c_len == TM and s_len % TQ == 0 and s_len % (2 * TK) == 0
    t = s_len + c_len

    xs = (x, ctx)
    tables = _rope_tables(s_len, c_len)
    c64, s64, c32, s32 = tables

    n_rows = (bsz + 1 + 7) // 8 * 8
    rows = jnp.concatenate([c, c_ctx[None, :], jnp.zeros((n_rows - bsz - 1, d), F32)], axis=0)
    mod_all = _ada_call(rows, ada_w, ada_b)
    mod_lat = mod_all[:, :bsz].reshape(depth, bsz, 1, 6, d)
    mod_ctx = jnp.broadcast_to(mod_all[:, bsz].reshape(depth, 1, 1, 6, d), (depth, bsz, 1, 6, d))
    mods = jnp.concatenate([mod_lat, mod_ctx], axis=2)
    mods = jnp.concatenate([mods, jnp.zeros((depth, bsz, 2, 2, d), F32)], axis=3)

    w1_idx, uq_idx, ukv_idx = _ab_layouts()
    diff_idx = _diff_layout()

    res = None
    for l in range(depth):
        j = l // 2
        mod = mods[l]
        if l % 2 == 0:
            weights = [_take_cols(ab_w_in[j], w1_idx), mla_q_norm[j].reshape(1, -1), mla_kv_norm[j].reshape(1, -1),
                       _take_cols(mla_w_uq[j], uq_idx), _take_cols(mla_w_ukv[j], ukv_idx)]
            xs, (sq, sk, sv, mq, mk, mv) = _proj_call("ab", xs, res, mod, norm_mix[l], weights,
                                                      [c64, s64, c32, s32], [512, 256, -256, 1024, 1024, -512])
            o_a = _swa_call(swa_sink[j], sq, sk, sv)
            o_b = _dense_attn_call("mla", mq, mk, mv)
            w_out = ab_w_out[j].astype(BF16)
            attn_outs, out_ws = [o_a, o_b], [w_out[:512], w_out[512:]]
        else:
            lambda_init = 0.8 - 0.6 * math.exp(-0.3 * l)
            weights = [_take_cols(diff_w_in[j], diff_idx)]
            xs, (dq, dk, dv) = _proj_call("diff", xs, res, mod, norm_mix[l], weights, [c64, s64], [1024, 1024, -1024])
            extra = [diff_lambda_q1[j].reshape(1, -1), diff_lambda_k1[j].reshape(1, -1),
                     diff_lambda_q2[j].reshape(1, -1), diff_lambda_k2[j].reshape(1, -1),
                     diff_subln[j].reshape(1, -1)]
            o_d = _dense_attn_call("diff", dq, dk, dv, extra=extra, lambda_init=lambda_init)
            attn_outs, out_ws = [o_d], [diff_w_out[j].astype(BF16)]

        wr = jnp.concatenate([router_group_w[l], router_expert_w[l],
                              jnp.zeros((d, LANES - N_GROUPS - N_EXPERTS), F32)], axis=1)
        wr_hi = wr.astype(BF16)
        wr = jnp.concatenate([wr_hi, (wr - wr_hi.astype(F32)).astype(BF16)], axis=1)
        br = jnp.concatenate([router_group_b[l], router_expert_b[l],
                              jnp.zeros((LANES - N_GROUPS - N_EXPERTS,), F32)]).reshape(1, LANES)
        xs, h2, logits = _out_proj_call(attn_outs, out_ws, xs, mod, norm_ffn[l], wr, br)

        n_tok = bsz * t
        rt, cnt = _router_call(logits.reshape(n_tok, LANES))
        slot_tok, gate, block_e, n_used, dest = _dispatch_plan(rt, cnt)
        xb = jnp.take(h2.reshape(n_tok, d), slot_tok, axis=0)
        yb = _moe_call(l, block_e, n_used, xb, expert_w1, expert_w3, expert_w2)
        res = (jnp.take(yb, dest[0], axis=0).reshape(bsz, t, d), jnp.take(yb, dest[1], axis=0).reshape(bsz, t, d),
               gate.T.reshape(bsz, t, TOP_K), mod)

    return _final_call(xs, res, final_norm, s_len)
```

```python
import functools
import math

import numpy as np
import jax
import jax.numpy as jnp
from jax import lax
from jax.experimental import pallas as pl
from jax.experimental.pallas import tpu as pltpu

F32 = jnp.float32
BF16 = jnp.bfloat16

D_MODEL = 1024
GRID_W = 64
ROPE_BASE = 10000.0
NORM_EPS = 1e-6
DIFF_EPS = 1e-5
NEG = -1e30

SWA_WINDOW = 128
MLA_SCALE = (64 + 32) ** -0.5
HEAD_SCALE = 64 ** -0.5
LOG2E = math.log2(math.e)

N_GROUPS = 4
EXPERTS_PER_GROUP = 8
N_EXPERTS = 32
TOP_K = 2
EXPERT_FF = 512

LANES = 128
TM = 256
TK = 256
TQ = 1024
TMOE = 256
VMEM_LIMIT = 56 * 1024 * 1024


def _cparams(n_axes):
    return pltpu.CompilerParams(dimension_semantics=("arbitrary",) * n_axes,
                                vmem_limit_bytes=VMEM_LIMIT)


def _rms(x, g, eps):
    return x * lax.rsqrt(jnp.mean(x * x, axis=-1, keepdims=True) + eps) * g


def _rope_block(x, c, s):
    return x * c + pltpu.roll(x, 64, 1) * s


_PAIR_PERM = np.concatenate([np.arange(0, 32), np.arange(64, 96), np.arange(32, 64), np.arange(96, 128)])


def _rope_tables(s_len, c_len):
    rows = s_len // GRID_W
    row = jnp.repeat(jnp.arange(rows, dtype=F32), GRID_W)
    col = jnp.tile(jnp.arange(GRID_W, dtype=F32), rows)

    def tab(dim):
        nf = dim // 4
        inv = ROPE_BASE ** (-jnp.arange(nf, dtype=F32) / nf)
        ang = jnp.concatenate([row[:, None] * inv, col[:, None] * inv], axis=-1)
        return jnp.cos(ang), jnp.sin(ang)

    cos64, sin64 = tab(64)
    cos32, sin32 = tab(32)
    c64 = jnp.concatenate([cos64] * 4, axis=-1)
    s64 = jnp.concatenate([-sin64, -sin64, sin64, sin64], axis=-1)
    one = jnp.ones((s_len, 32), F32)
    zero = jnp.zeros((s_len, 32), F32)
    c32 = jnp.concatenate([cos32, cos32, one, cos32, cos32, one], axis=-1)
    s32 = jnp.concatenate([-sin32, -sin32, zero, sin32, sin32, zero], axis=-1)

    def ext(t, fill):
        return jnp.concatenate([t, jnp.full((c_len, LANES), fill, F32)], axis=0)

    return ext(c64, 1.0), ext(s64, 0.0), ext(c32, 1.0), ext(s32, 0.0)


def _ada_kernel(x_ref, w_ref, b_ref, o_ref):
    x = x_ref[...]
    sx = x * jax.nn.sigmoid(x)
    o_ref[0] = jnp.dot(sx.astype(BF16), w_ref[0].astype(BF16), preferred_element_type=F32) + b_ref[0]


def _ada_call(rows, ada_w, ada_b):
    depth, d, n6 = ada_w.shape
    r = rows.shape[0]
    tn = 1536
    return pl.pallas_call(
        _ada_kernel,
        grid=(depth, n6 // tn),
        in_specs=[pl.BlockSpec((r, d), lambda l, j: (0, 0)),
                  pl.BlockSpec((1, d, tn), lambda l, j: (l, 0, j)),
                  pl.BlockSpec((1, 1, tn), lambda l, j: (l, 0, j))],
        out_specs=pl.BlockSpec((1, r, tn), lambda l, j: (l, 0, j)),
        out_shape=jax.ShapeDtypeStruct((depth, r, n6), F32),
        compiler_params=_cparams(2),
        name="ada_mod",
    )(rows, ada_w, ada_b.reshape(depth, 1, n6))


def _first_layer_x(x_ref, ctx_ref, nlat):
    return jnp.where(pl.program_id(1) < nlat, x_ref[0], ctx_ref[0])


def _moe_residual(x_ref, ya_ref, yb_ref, ga_ref, gb_ref, pmod_ref):
    reps = x_ref.shape[-1] // LANES
    ga = jnp.concatenate([ga_ref[0]] * reps, axis=1)
    gb = jnp.concatenate([gb_ref[0]] * reps, axis=1)
    y = ga * ya_ref[0] + gb * yb_ref[0]
    return x_ref[0] + pmod_ref[0, 0][5:6, :] * y


def _prenorm(has_res, nlat, x_refs, mod_ref, g_ref, xo_ref):
    if has_res:
        x = _moe_residual(*x_refs)
        xo_ref[0] = x
    else:
        x = _first_layer_x(x_refs[0], x_refs[1], nlat)
    m = mod_ref[0, 0]
    return _rms(x, g_ref[...], NORM_EPS) * (1.0 + m[1:2, :]) + m[0:1, :]


def _ab_proj_kernel(has_res, nlat, *refs):
    n_x = 6 if has_res else 2
    x_refs, refs = refs[:n_x], refs[n_x:]
    (mod_ref, g_ref, w1_ref, qn_ref, kvn_ref, wuq_ref, wukv_ref,
     c64_ref, s64_ref, c32_ref, s32_ref) = refs[:11]
    outs = refs[11:]
    if has_res:
        xo_ref, outs = outs[0], outs[1:]
    else:
        xo_ref = None
    sq_ref, sk_ref, sv_ref, mq_ref, mk_ref, mv_ref = outs

    h = _prenorm(has_res, nlat, x_refs, mod_ref, g_ref, xo_ref)
    p = jnp.dot(h.astype(BF16), w1_ref[...], preferred_element_type=F32)
    c64, s64, c32, s32 = c64_ref[...], s64_ref[...], c32_ref[...], s32_ref[...]
    for j in range(4):
        blk = _rope_block(p[:, j * 128:(j + 1) * 128], c64, s64)
        sq_ref[0, :, j * 128:(j + 1) * 128] = (blk * (HEAD_SCALE * LOG2E)).astype(BF16)
    for j in range(2):
        blk = _rope_block(p[:, 512 + j * 128:512 + (j + 1) * 128], c64, s64)
        sk_ref[0, :, j * 128:(j + 1) * 128] = blk.astype(BF16)
    sv_ref[0, 0] = p[:, 768:1024].T.astype(BF16)
    cq = p[:, 1024:1280]
    ckv = p[:, 1280:1408]
    kr = _rope_block(p[:, 1408:1536], c32, s32).astype(BF16)
    qm = jnp.dot(_rms(cq, qn_ref[...], NORM_EPS).astype(BF16), wuq_ref[...], preferred_element_type=F32)
    kv = jnp.dot(_rms(ckv, kvn_ref[...], NORM_EPS).astype(BF16), wukv_ref[...], preferred_element_type=F32)
    for j in range(4):
        mq_ref[0, :, j * 256:j * 256 + 128] = (qm[:, j * 256:j * 256 + 128] * (MLA_SCALE * LOG2E)).astype(BF16)
        rr = _rope_block(qm[:, j * 256 + 128:(j + 1) * 256], c32, s32)
        mq_ref[0, :, j * 256 + 128:(j + 1) * 256] = (rr * (MLA_SCALE * LOG2E)).astype(BF16)
        mk_ref[0, :, j * 256:j * 256 + 128] = kv[:, j * 128:(j + 1) * 128].astype(BF16)
        mk_ref[0, :, j * 256 + 128:(j + 1) * 256] = kr
    mv_ref[0, 0] = kv[:, 512:1024].T.astype(BF16)


def _diff_proj_kernel(has_res, nlat, *refs):
    n_x = 6 if has_res else 2
    x_refs, refs = refs[:n_x], refs[n_x:]
    mod_ref, g_ref, w_ref, c64_ref, s64_ref = refs[:5]
    outs = refs[5:]
    if has_res:
        xo_ref, outs = outs[0], outs[1:]
    else:
        xo_ref = None
    q_ref, k_ref, v_ref = outs
    h = _prenorm(has_res, nlat, x_refs, mod_ref, g_ref, xo_ref)
    p = jnp.dot(h.astype(BF16), w_ref[...], preferred_element_type=F32)
    c64, s64 = c64_ref[...], s64_ref[...]
    for j in range(8):
        blk = _rope_block(p[:, j * 128:(j + 1) * 128], c64, s64)
        q_ref[0, :, j * 128:(j + 1) * 128] = (blk * (HEAD_SCALE * LOG2E)).astype(BF16)
        blk = _rope_block(p[:, 1024 + j * 128:1024 + (j + 1) * 128], c64, s64)
        k_ref[0, :, j * 128:(j + 1) * 128] = blk.astype(BF16)
    v_ref[0, 0] = p[:, 2048:3072].T.astype(BF16)


def _row_spec(width):
    return pl.BlockSpec((1, TM, width), lambda b, i: (b, i, 0))


def _mod_spec(nlat):
    return pl.BlockSpec((1, 1, 8, D_MODEL), lambda b, i: (b, i // nlat, 0, 0))


def _full_spec(shape):
    nd = len(shape)
    return pl.BlockSpec(shape, lambda b, i: (0,) * nd)


def _tab_spec():
    return pl.BlockSpec((TM, LANES), lambda b, i: (i, 0))


def _split_specs(nlat, d):
    return [pl.BlockSpec((1, TM, d), lambda b, i: (b, jnp.minimum(i, nlat - 1), 0)),
            pl.BlockSpec((1, TM, d), lambda b, i: (b, 0, 0))]


def _proj_call(kind, x, res, mod, g, weights, tables, out_widths):
    has_res = res is not None
    if has_res:
        bsz, t, d = x.shape
    else:
        bsz, t, d = x[0].shape[0], x[0].shape[1] + x[1].shape[1], x[0].shape[2]
    nt = t // TM
    nlat = nt - 1
    if has_res:
        ya, yb, ga, gb, pmod = res
        ins = [x, ya, yb, ga, gb, pmod]
        specs = [_row_spec(d), _row_spec(d), _row_spec(d), _row_spec(LANES), _row_spec(LANES), _mod_spec(nlat)]
    else:
        ins = list(x)
        specs = _split_specs(nlat, d)
    ins += [mod, g.reshape(1, d)]
    specs += [_mod_spec(nlat), _full_spec((1, d))]
    for w in weights:
        ins.append(w)
        specs.append(_full_spec(w.shape))
    for tb in tables:
        ins.append(tb)
        specs.append(_tab_spec())
    out_shapes, out_specs = [], []
    if has_res:
        out_shapes.append(jax.ShapeDtypeStruct((bsz, t, d), F32))
        out_specs.append(_row_spec(d))
    for w in out_widths:
        if w < 0:
            out_shapes.append(jax.ShapeDtypeStruct((bsz, nt, -w, TM), BF16))
            out_specs.append(pl.BlockSpec((1, 1, -w, TM), lambda b, i: (b, i, 0, 0)))
        else:
            out_shapes.append(jax.ShapeDtypeStruct((bsz, t, w), BF16))
            out_specs.append(_row_spec(w))
    body = _ab_proj_kernel if kind == "ab" else _diff_proj_kernel
    outs = pl.pallas_call(
        functools.partial(body, has_res, nlat),
        grid=(bsz, nt),
        in_specs=specs,
        out_specs=out_specs,
        out_shape=out_shapes,
        compiler_params=_cparams(2),
        name=kind + "_proj",
    )(*ins)
    if has_res:
        return outs[0], outs[1:]
    return x, outs


def _pair_masks(mode, lane):
    if mode == "mla":
        in_a = (lane < 64) | ((lane >= 128) & (lane < 144)) | ((lane >= 192) & (lane < 208))
        in_b = ((lane >= 64) & (lane < 128)) | ((lane >= 144) & (lane < 160)) | ((lane >= 208) & (lane < 224))
    else:
        in_a = (lane < 32) | ((lane >= 64) & (lane < 96))
        in_b = ((lane >= 32) & (lane < 64)) | (lane >= 96)
    return in_a, in_b


def _dense_attn_kernel(mode, lambda_init, nlat, *refs):
    n_in = 8 if mode == "diff" else 3
    q_ref, k_ref, vt_ref = refs[:3]
    o_ref = refs[n_in]
    s_bufs = refs[n_in + 1:n_in + 3]
    p_bufs = refs[n_in + 3:n_in + 5]
    acc_buf, m_buf, l_buf, a_buf = refs[n_in + 5:]
    i = pl.program_id(2)
    qt = q_ref[0].T
    width, tq = qt.shape
    row = lax.broadcasted_iota(jnp.int32, (width, 1), 0)
    in_a, in_b = _pair_masks(mode, row)
    zero = jnp.zeros_like(qt)
    q2 = jnp.concatenate([jnp.where(in_a, qt, zero), jnp.where(in_b, qt, zero)], axis=1)
    chunks = [(nlat, 1)] + [(2 * c, 2) for c in range(nlat // 2)]
    n_lat_q = (nlat * TK) // tq

    def scores(chunk, s_buf):
        k0, n = chunk
        s_buf[0:n * TK, :] = jnp.dot(k_ref[0, k0 * TK:(k0 + n) * TK, :], q2, preferred_element_type=F32)

    def pv(chunk, p_buf):
        k0, n = chunk
        out = jnp.dot(vt_ref[0, k0], p_buf[0:TK, :], preferred_element_type=F32)
        for r in range(1, n):
            out = out + jnp.dot(vt_ref[0, k0 + r], p_buf[r * TK:(r + 1) * TK, :], preferred_element_type=F32)
        return out

    def softmax(chunk, s_buf, p_buf, first):
        n = chunk[1]
        s = s_buf[0:n * TK, :]
        mx = jnp.max(s, axis=0, keepdims=True)
        if first:
            mn = mx
        else:
            m = m_buf[...]
            mn = jnp.maximum(m, mx)
            a_buf[...] = jnp.exp2(m - mn)
        p = jnp.exp2(s - mn)
        ps = jnp.sum(p, axis=0, keepdims=True)
        l_buf[...] = ps if first else a_buf[...] * l_buf[...] + ps
        m_buf[...] = mn
        p_buf[0:n * TK, :] = p.astype(BF16)

    def accumulate(chunk, p_buf, first):
        if first:
            acc_buf[...] = pv(chunk, p_buf)
        else:
            acc_buf[...] = a_buf[...] * acc_buf[...] + pv(chunk, p_buf)

    def pipeline(chs):
        scores(chs[0], s_bufs[0])
        for c, ch in enumerate(chs):
            if c >= 2:
                accumulate(chs[c - 1], p_bufs[(c - 1) % 2], first=False)
            if c + 1 < len(chs):
                scores(chs[c + 1], s_bufs[(c + 1) % 2])
            softmax(ch, s_bufs[c % 2], p_bufs[c % 2], first=c == 0)
            if c == 1:
                accumulate(chs[0], p_bufs[0], first=True)
        last = len(chs) - 1
        if last == 0:
            accumulate(chs[0], p_bufs[0], first=True)
        else:
            accumulate(chs[last], p_bufs[last % 2], first=False)

    @pl.when(i < n_lat_q)
    def _():
        pipeline(chunks)

    @pl.when(i >= n_lat_q)
    def _():
        pipeline(chunks[:1])

    o2 = acc_buf[...] * (1.0 / l_buf[...])
    oa, ob = o2[:, :tq], o2[:, tq:]
    if mode == "diff":
        lq1_ref, lk1_ref, lq2_ref, lk2_ref, sub_ref = refs[3:8]
        lam = (jnp.exp(jnp.sum(lq1_ref[...] * lk1_ref[...], axis=1, keepdims=True))
               - jnp.exp(jnp.sum(lq2_ref[...] * lk2_ref[...], axis=1, keepdims=True)) + lambda_init)
        o = (oa - lam * ob).T
        o = _rms(o, sub_ref[...], DIFF_EPS) * (1.0 - lambda_init)
    else:
        vrow = lax.broadcasted_iota(jnp.int32, (LANES, 1), 0)
        o = jnp.where(vrow < 64, oa, ob).T
    o_ref[0] = o.astype(BF16)


def _dense_attn_call(mode, q, k, vt, extra=(), lambda_init=0.0):
    bsz, t, qtot = q.shape
    width = 256 if mode == "mla" else 128
    npairs = qtot // width
    nq = pl.cdiv(t, TQ)
    ins = [q, k, vt]
    specs = [pl.BlockSpec((1, TQ, width), lambda b, j, i: (b, i, j)),
             pl.BlockSpec((1, t, width), lambda b, j, i: (b, 0, j)),
             pl.BlockSpec((1, t // TK, LANES, TK), lambda b, j, i: (b, 0, j, 0))]
    for e in extra:
        ins.append(e)
        specs.append(pl.BlockSpec(e.shape, lambda b, j, i: (0, 0)))
    return pl.pallas_call(
        functools.partial(_dense_attn_kernel, mode, lambda_init, t // TK - 1),
        grid=(bsz, npairs, nq),
        in_specs=specs,
        out_specs=pl.BlockSpec((1, TQ, LANES), lambda b, j, i: (b, i, j)),
        out_shape=jax.ShapeDtypeStruct((bsz, t, npairs * LANES), BF16),
        scratch_shapes=[pltpu.VMEM((2 * TK, 2 * TQ), F32), pltpu.VMEM((2 * TK, 2 * TQ), F32),
                        pltpu.VMEM((2 * TK, 2 * TQ), BF16), pltpu.VMEM((2 * TK, 2 * TQ), BF16),
                        pltpu.VMEM((LANES, 2 * TQ), F32), pltpu.VMEM((1, 2 * TQ), F32),
                        pltpu.VMEM((1, 2 * TQ), F32), pltpu.VMEM((1, 2 * TQ), F32)],
        compiler_params=_cparams(3),
        name=mode + "_attn",
    )(*ins)


def _swa_kernel(nlat, sink_ref, q_ref, k_ref, vt_ref, o_ref):
    g = pl.program_id(1)
    i = pl.program_id(2)
    s_len = nlat * TM
    half = TM // 2
    q = q_ref[0]
    row = lax.broadcasted_iota(jnp.int32, (LANES, 1), 0)
    in_a, in_b = _pair_masks("pair", row)
    cols = []
    for pr in range(2):
        qt = q[:, pr * LANES:(pr + 1) * LANES].T
        zero = jnp.zeros_like(qt)
        cols += [jnp.where(in_a, qt, zero), jnp.where(in_b, qt, zero)]
    q4 = jnp.concatenate(cols, axis=1)

    start0 = pl.multiple_of(jnp.maximum(i * TM - half, 0), half)
    start1 = pl.multiple_of(i * TM, TM)
    start2 = pl.multiple_of(jnp.minimum((i + 1) * TM, s_len + half), half)
    k_cat = jnp.concatenate([k_ref[0, pl.ds(start0, half), :], k_ref[0, pl.ds(start1, TM), :],
                             k_ref[0, pl.ds(start2, half), :], k_ref[0, s_len:s_len + TM, :]], axis=0)
    b0 = jnp.maximum(i - 1, 0)
    b2 = jnp.minimum(i + 1, nlat)
    vt_cat = jnp.concatenate([vt_ref[0, b0][:, half:], vt_ref[0, i], vt_ref[0, b2][:, :half],
                              vt_ref[0, nlat]], axis=1)

    r = lax.broadcasted_iota(jnp.int32, (2 * TM, 1), 0)
    far = -4 * SWA_WINDOW
    lat = i < nlat
    pos0 = jnp.where(lat & (i >= 1), start0 + r, far)
    pos1 = jnp.where(lat, start1 + r - half, far)
    pos2 = jnp.where(lat & (i + 1 < nlat), start2 + r - half - TM, far)
    kpos = jnp.where(r < half, pos0, jnp.where(r < half + TM, pos1, pos2))
    qpos = i * TM + lax.broadcasted_iota(jnp.int32, (1, TM), 1)
    band = jnp.abs(qpos - kpos) <= SWA_WINDOW
    band4 = jnp.concatenate([band] * 4, axis=1)

    s = jnp.dot(k_cat, q4, preferred_element_type=F32)
    s_loc = jnp.where(band4, s[:2 * TM], NEG)
    s_ctx = s[2 * TM:]
    sink = jnp.concatenate([jnp.full((1, TM), sink_ref[4 * g + h], F32) for h in range(4)], axis=1) * LOG2E
    m = jnp.maximum(jnp.maximum(jnp.max(s_loc, axis=0, keepdims=True), jnp.max(s_ctx, axis=0, keepdims=True)), sink)
    p_loc = jnp.exp2(s_loc - m)
    p_ctx = jnp.exp2(s_ctx - m)
    l = jnp.sum(p_loc, axis=0, keepdims=True) + jnp.sum(p_ctx, axis=0, keepdims=True) + jnp.exp2(sink - m)
    p = jnp.concatenate([p_loc, p_ctx], axis=0).astype(BF16)
    o4 = jnp.dot(vt_cat, p, preferred_element_type=F32) * (1.0 / l)
    vrow = lax.broadcasted_iota(jnp.int32, (LANES, 1), 0)
    for pr in range(2):
        oa = o4[:, (2 * pr) * TM:(2 * pr + 1) * TM]
        ob = o4[:, (2 * pr + 1) * TM:(2 * pr + 2) * TM]
        o_ref[0, :, pr * LANES:(pr + 1) * LANES] = jnp.where(vrow < 64, oa, ob).T.astype(BF16)


def _swa_call(sink, q, k, vt):
    bsz, t, qtot = q.shape
    nkv = k.shape[-1] // LANES
    nt = t // TM
    grid_spec = pltpu.PrefetchScalarGridSpec(
        num_scalar_prefetch=1,
        grid=(bsz, nkv, nt),
        in_specs=[pl.BlockSpec((1, TM, 2 * LANES), lambda b, g, i, s: (b, i, g)),
                  pl.BlockSpec((1, t, LANES), lambda b, g, i, s: (b, 0, g)),
                  pl.BlockSpec((1, nt, LANES, TM), lambda b, g, i, s: (b, 0, g, 0))],
        out_specs=pl.BlockSpec((1, TM, 2 * LANES), lambda b, g, i, s: (b, i, g)),
    )
    return pl.pallas_call(
        functools.partial(_swa_kernel, nt - 1),
        grid_spec=grid_spec,
        out_shape=jax.ShapeDtypeStruct((bsz, t, qtot), BF16),
        compiler_params=_cparams(3),
        name="swa_attn",
    )(sink, q, k, vt)


def _out_proj_kernel(n_o, split, nlat, *refs):
    o_refs = refs[:n_o]
    w_refs = refs[n_o:2 * n_o]
    refs = refs[2 * n_o:]
    if split:
        x_in = _first_layer_x(refs[0], refs[1], nlat)
        refs = refs[2:]
    else:
        x_in = refs[0][0]
        refs = refs[1:]
    mod_ref, g_ref, wr_ref, br_ref, xo_ref, h_ref, lg_ref = refs
    acc = jnp.dot(o_refs[0][0], w_refs[0][...], preferred_element_type=F32)
    for n in range(1, n_o):
        acc = acc + jnp.dot(o_refs[n][0], w_refs[n][...], preferred_element_type=F32)
    m = mod_ref[0, 0]
    x = x_in + m[2:3, :] * acc
    xo_ref[0] = x
    h = _rms(x, g_ref[...], NORM_EPS) * (1.0 + m[4:5, :]) + m[3:4, :]
    hi = h.astype(BF16)
    lo = (h - hi.astype(F32)).astype(BF16)
    h_ref[0] = hi
    both = jnp.dot(hi, wr_ref[...], preferred_element_type=F32)
    lg_ref[0] = (both[:, :LANES] + both[:, LANES:]
                 + jnp.dot(lo, wr_ref[:, :LANES], preferred_element_type=F32) + br_ref[...])


def _out_proj_call(os_, ws, x, mod, g, wr, br):
    split = isinstance(x, (tuple, list))
    if split:
        bsz, t, d = x[0].shape[0], x[0].shape[1] + x[1].shape[1], x[0].shape[2]
    else:
        bsz, t, d = x.shape
    nt = t // TM
    nlat = nt - 1
    n_o = len(os_)
    x_ins = list(x) if split else [x]
    x_specs = _split_specs(nlat, d) if split else [_row_spec(d)]
    ins = list(os_) + list(ws) + x_ins + [mod, g.reshape(1, d), wr, br]
    specs = ([_row_spec(o.shape[-1]) for o in os_] + [_full_spec(w.shape) for w in ws]
             + x_specs + [_mod_spec(nlat), _full_spec((1, d)), _full_spec(wr.shape), _full_spec(br.shape)])
    return pl.pallas_call(
        functools.partial(_out_proj_kernel, n_o, split, nlat),
        grid=(bsz, nt),
        in_specs=specs,
        out_specs=[_row_spec(d), _row_spec(d), _row_spec(LANES)],
        out_shape=[jax.ShapeDtypeStruct((bsz, t, d), F32),
                   jax.ShapeDtypeStruct((bsz, t, d), BF16),
                   jax.ShapeDtypeStruct((bsz, t, LANES), F32)],
        compiler_params=_cparams(2),
        name="out_proj",
    )(*ins)


def _moe_kernel(be_ref, nu_ref, x_ref, w1_ref, w3_ref, w2_ref, o_ref, w1c, w3c, w2c):
    i = pl.program_id(0)
    e = be_ref[i]

    @pl.when((i == 0) | (e != be_ref[jnp.maximum(i - 1, 0)]))
    def _():
        w1c[...] = w1_ref[0, 0].astype(BF16)
        w3c[...] = w3_ref[0, 0].astype(BF16)
        w2c[...] = w2_ref[0, 0].astype(BF16)

    @pl.when(i < nu_ref[0])
    def _():
        x = x_ref[...]
        a = jnp.dot(x, w1c[...], preferred_element_type=F32)
        b = jnp.dot(x, w3c[...], preferred_element_type=F32)
        hmid = (a * jax.nn.sigmoid(a)) * b
        o_ref[...] = jnp.dot(hmid.astype(BF16), w2c[...], preferred_element_type=F32)

    @pl.when(i >= nu_ref[0])
    def _():
        o_ref[...] = jnp.zeros_like(o_ref)


def _moe_call(layer, block_e, n_used, xb, w1, w3, w2):
    n_slots, d = xb.shape
    n_blocks = n_slots // TMOE
    ff = w1.shape[-1]
    grid_spec = pltpu.PrefetchScalarGridSpec(
        num_scalar_prefetch=2,
        grid=(n_blocks,),
        in_specs=[pl.BlockSpec((TMOE, d), lambda i, be, nu: (i, 0)),
                  pl.BlockSpec((1, 1, d, ff), lambda i, be, nu: (layer, be[i], 0, 0)),
                  pl.BlockSpec((1, 1, d, ff), lambda i, be, nu: (layer, be[i], 0, 0)),
                  pl.BlockSpec((1, 1, ff, d), lambda i, be, nu: (layer, be[i], 0, 0))],
        out_specs=pl.BlockSpec((TMOE, d), lambda i, be, nu: (i, 0)),
        scratch_shapes=[pltpu.VMEM((d, ff), BF16), pltpu.VMEM((d, ff), BF16), pltpu.VMEM((ff, d), BF16)],
    )
    return pl.pallas_call(
        _moe_kernel,
        grid_spec=grid_spec,
        out_shape=jax.ShapeDtypeStruct((n_slots, d), F32),
        compiler_params=_cparams(1),
        name="moe_experts",
    )(block_e, n_used, xb, w1, w3, w2)


def _router_kernel(lg_ref, tri_ref, rt_ref, cnt_ref, ga_ref, gb_ref, run):
    i = pl.program_id(0)

    @pl.when(i == 0)
    def _():
        run[...] = jnp.zeros_like(run)

    lg = lg_ref[...]
    lane = lax.broadcasted_iota(jnp.int32, lg.shape, 1)
    gmask = lane < N_GROUPS
    gl = jnp.where(gmask, lg, NEG)
    gmax = jnp.max(gl, axis=1, keepdims=True)
    grp = jnp.min(jnp.where(gl == gmax, lane, LANES), axis=1, keepdims=True)
    p_grp = 1.0 / jnp.sum(jnp.where(gmask, jnp.exp(lg - gmax), 0.0), axis=1, keepdims=True)
    first = N_GROUPS + EXPERTS_PER_GROUP * grp
    emask = (lane >= first) & (lane < first + EXPERTS_PER_GROUP)
    el = jnp.where(emask, lg, NEG)
    e1 = jnp.max(el, axis=1, keepdims=True)
    i1 = jnp.min(jnp.where(el == e1, lane, LANES), axis=1, keepdims=True)
    el2 = jnp.where(lane == i1, NEG, el)
    e2 = jnp.max(el2, axis=1, keepdims=True)
    i2 = jnp.min(jnp.where(el2 == e2, lane, LANES), axis=1, keepdims=True)
    tt = jnp.exp(e2 - e1)
    g1 = p_grp / (1.0 + tt)
    g2 = g1 * tt
    oh1 = lane == i1
    oh2 = lane == i2
    onehot = jnp.where(oh1 | oh2, 1.0, 0.0)
    rank_all = jnp.dot(tri_ref[...], onehot.astype(BF16), preferred_element_type=F32) + run[...]
    r1 = jnp.sum(jnp.where(oh1, rank_all, 0.0), axis=1, keepdims=True)
    r2 = jnp.sum(jnp.where(oh2, rank_all, 0.0), axis=1, keepdims=True)
    run[...] = run[...] + jnp.sum(onehot, axis=0, keepdims=True)
    cnt_ref[...] = run[...]
    vals = [(i1 - N_GROUPS).astype(F32), (i2 - N_GROUPS).astype(F32), r1, r2, g1, g2]
    packed = jnp.zeros(lg.shape, F32)
    for n, v in enumerate(vals):
        packed = jnp.where(lane == n, v, packed)
    rt_ref[...] = packed.T[0:8, :]
    ga_ref[...] = jnp.broadcast_to(g1, lg.shape)
    gb_ref[...] = jnp.broadcast_to(g2, lg.shape)


def _router_call(logits):
    n_tok = logits.shape[0]
    tri = (jnp.arange(TM)[:, None] > jnp.arange(TM)[None, :]).astype(BF16)
    return pl.pallas_call(
        _router_kernel,
        grid=(n_tok // TM,),
        in_specs=[pl.BlockSpec((TM, LANES), lambda i: (i, 0)),
                  pl.BlockSpec((TM, TM), lambda i: (0, 0))],
        out_specs=[pl.BlockSpec((8, TM), lambda i: (0, i)),
                   pl.BlockSpec((1, LANES), lambda i: (0, 0)),
                   pl.BlockSpec((TM, LANES), lambda i: (i, 0)),
                   pl.BlockSpec((TM, LANES), lambda i: (i, 0))],
        out_shape=[jax.ShapeDtypeStruct((8, n_tok), F32), jax.ShapeDtypeStruct((1, LANES), F32),
                   jax.ShapeDtypeStruct((n_tok, LANES), F32), jax.ShapeDtypeStruct((n_tok, LANES), F32)],
        scratch_shapes=[pltpu.VMEM((1, LANES), F32)],
        compiler_params=_cparams(1),
        name="router",
    )(logits, tri)


def _dispatch_plan(rt, cnt):
    n_tok = rt.shape[1]
    eid = rt[0:2].astype(jnp.int32)
    rank = rt[2:4].astype(jnp.int32)
    counts = cnt[0, N_GROUPS:N_GROUPS + N_EXPERTS].astype(jnp.int32)
    padded = (counts + TMOE - 1) // TMOE * TMOE
    pad_end = jnp.cumsum(padded)
    pad_start = pad_end - padded
    experts = jnp.arange(N_EXPERTS, dtype=jnp.int32)[:, None, None]
    dest = jnp.sum(jnp.where(eid[None] == experts, pad_start[:, None, None], 0), axis=0) + rank
    n_assign = n_tok * TOP_K
    n_blocks = (n_assign + N_EXPERTS * (TMOE - 1) + TMOE - 1) // TMOE
    n_slots = n_blocks * TMOE
    tok = jnp.arange(n_tok, dtype=jnp.int32)
    slot_tok = jnp.zeros((n_slots,), jnp.int32).at[dest.reshape(n_assign)].set(jnp.concatenate([tok, tok]))
    block_start = jnp.arange(n_blocks, dtype=jnp.int32) * TMOE
    block_e = jnp.minimum(jnp.sum((block_start[:, None] >= pad_end[None, :]).astype(jnp.int32), axis=1),
                          N_EXPERTS - 1).astype(jnp.int32)
    n_used = (pad_end[-1:] // TMOE).astype(jnp.int32)
    return slot_tok, block_e, n_used, dest


def _final_kernel(x_ref, ya_ref, yb_ref, ga_ref, gb_ref, pmod_ref, g_ref, o_ref):
    x = _moe_residual(x_ref, ya_ref, yb_ref, ga_ref, gb_ref, pmod_ref)
    o_ref[0] = _rms(x, g_ref[...], NORM_EPS)


def _final_call(x, res, g, s_len):
    bsz, t, d = x.shape
    ya, yb, ga, gb, pmod = res
    return pl.pallas_call(
        _final_kernel,
        grid=(bsz, s_len // TM),
        in_specs=[_row_spec(d), _row_spec(d), _row_spec(d), _row_spec(LANES), _row_spec(LANES),
                  pl.BlockSpec((1, 1, 8, d), lambda b, i: (b, 0, 0, 0)),
                  _full_spec((1, d))],
        out_specs=_row_spec(d),
        out_shape=jax.ShapeDtypeStruct((bsz, s_len, d), F32),
        compiler_params=_cparams(2),
        name="final_norm",
    )(x, ya, yb, ga, gb, pmod, g.reshape(1, d))


def _take_cols(w, idx):
    wz = jnp.concatenate([w, jnp.zeros((w.shape[0], 1), w.dtype)], axis=1)
    return jnp.take(wz, jnp.asarray(idx, dtype=jnp.int32), axis=1).astype(BF16)


def _ab_layouts():
    zc = 1184
    cols = []
    for j in range(4):
        cols.append(j * 128 + _PAIR_PERM)
    for g in range(2):
        base = 512 + g * 64
        cols.append(base + np.concatenate([np.arange(0, 32), np.arange(0, 32), np.arange(32, 64), np.arange(32, 64)]))
    for g in range(2):
        base = 640 + g * 64
        cols.append(base + np.concatenate([np.arange(64), np.arange(64)]))
    cols.append(768 + np.arange(256))
    cols.append(1024 + np.arange(128))
    kr = 1152
    z32 = np.full((32,), zc)
    cols.append(np.concatenate([kr + np.arange(16), kr + np.arange(16), z32,
                                kr + 16 + np.arange(16), kr + 16 + np.arange(16), z32]))
    w1_idx = np.concatenate(cols)

    zq = 768
    uq = []
    z32q = np.full((32,), zq)
    for j in range(4):
        a, b = 2 * j * 96, (2 * j + 1) * 96
        uq.append(np.concatenate([a + np.arange(64), b + np.arange(64),
                                  a + 64 + np.arange(16), b + 64 + np.arange(16), z32q,
                                  a + 80 + np.arange(16), b + 80 + np.arange(16), z32q]))
    uq_idx = np.concatenate(uq)

    kn, mv = [], []
    for h in range(8):
        kn.append(h * 128 + np.arange(64))
        mv.append(h * 128 + 64 + np.arange(64))
    ukv_idx = np.concatenate(kn + mv)
    return w1_idx, uq_idx, ukv_idx


def _diff_layout():
    cols = []
    for part in range(2):
        for h in range(8):
            cols.append(part * 1024 + h * 128 + _PAIR_PERM)
    cols.append(2048 + np.arange(1024))
    return np.concatenate(cols)


def kernel(x, c, ctx, c_ctx, norm_mix, norm_ffn, ada_w, ada_b, ab_w_in, mla_q_norm, mla_w_uq, mla_kv_norm, mla_w_ukv, swa_sink, ab_w_out, diff_w_in, diff_lambda_q1, diff_lambda_k1, diff_lambda_q2, diff_lambda_k2, diff_subln, diff_w_out, router_group_w, router_group_b, router_expert_w, router_expert_b, expert_w1, expert_w3, expert_w2, final_norm):
    bsz, s_len, d = x.shape
    c_len = ctx.shape[1]
    depth = ada_w.shape[0]
    assert d == D_MODEL and c_len == TM and s_len % TQ == 0 and s_len % (2 * TK) == 0
    t = s_len + c_len

    xs = (x, ctx)
    tables = _rope_tables(s_len, c_len)
    c64, s64, c32, s32 = tables

    n_rows = (bsz + 1 + 7) // 8 * 8
    rows = jnp.concatenate([c, c_ctx[None, :], jnp.zeros((n_rows - bsz - 1, d), F32)], axis=0)
    mod_all = _ada_call(rows, ada_w, ada_b)
    mod_lat = mod_all[:, :bsz].reshape(depth, bsz, 1, 6, d)
    mod_ctx = jnp.broadcast_to(mod_all[:, bsz].reshape(depth, 1, 1, 6, d), (depth, bsz, 1, 6, d))
    mods = jnp.concatenate([mod_lat, mod_ctx], axis=2)
    mods = jnp.concatenate([mods, jnp.zeros((depth, bsz, 2, 2, d), F32)], axis=3)

    w1_idx, uq_idx, ukv_idx = _ab_layouts()
    diff_idx = _diff_layout()

    res = None
    for l in range(depth):
        j = l // 2
        mod = mods[l]
        if l % 2 == 0:
            weights = [_take_cols(ab_w_in[j], w1_idx), mla_q_norm[j].reshape(1, -1), mla_kv_norm[j].reshape(1, -1),
                       _take_cols(mla_w_uq[j], uq_idx), _take_cols(mla_w_ukv[j], ukv_idx)]
            xs, (sq, sk, sv, mq, mk, mv) = _proj_call("ab", xs, res, mod, norm_mix[l], weights,
                                                      [c64, s64, c32, s32], [512, 256, -256, 1024, 1024, -512])
            o_a = _swa_call(swa_sink[j], sq, sk, sv)
            o_b = _dense_attn_call("mla", mq, mk, mv)
            w_out = ab_w_out[j].astype(BF16)
            attn_outs, out_ws = [o_a, o_b], [w_out[:512], w_out[512:]]
        else:
            lambda_init = 0.8 - 0.6 * math.exp(-0.3 * l)
            weights = [_take_cols(diff_w_in[j], diff_idx)]
            xs, (dq, dk, dv) = _proj_call("diff", xs, res, mod, norm_mix[l], weights, [c64, s64], [1024, 1024, -1024])
            extra = [diff_lambda_q1[j].reshape(1, -1), diff_lambda_k1[j].reshape(1, -1),
                     diff_lambda_q2[j].reshape(1, -1), diff_lambda_k2[j].reshape(1, -1),
                     diff_subln[j].reshape(1, -1)]
            o_d = _dense_attn_call("diff", dq, dk, dv, extra=extra, lambda_init=lambda_init)
            attn_outs, out_ws = [o_d], [diff_w_out[j].astype(BF16)]

        wr = jnp.concatenate([router_group_w[l], router_expert_w[l],
                              jnp.zeros((d, LANES - N_GROUPS - N_EXPERTS), F32)], axis=1)
        wr_hi = wr.astype(BF16)
        wr = jnp.concatenate([wr_hi, (wr - wr_hi.astype(F32)).astype(BF16)], axis=1)
        br = jnp.concatenate([router_group_b[l], router_expert_b[l],
                              jnp.zeros((LANES - N_GROUPS - N_EXPERTS,), F32)]).reshape(1, LANES)
        xs, h2, logits = _out_proj_call(attn_outs, out_ws, xs, mod, norm_ffn[l], wr, br)

        n_tok = bsz * t
        rt, cnt, ga, gb = _router_call(logits.reshape(n_tok, LANES))
        slot_tok, block_e, n_used, dest = _dispatch_plan(rt, cnt)
        xb = jnp.take(h2.reshape(n_tok, d), slot_tok, axis=0)
        yb = _moe_call(l, block_e, n_used, xb, expert_w1, expert_w3, expert_w2)
        res = (jnp.take(yb, dest[0], axis=0).reshape(bsz, t, d), jnp.take(yb, dest[1], axis=0).reshape(bsz, t, d),
               ga.reshape(bsz, t, LANES), gb.reshape(bsz, t, LANES), mod)

    return _final_call(xs, res, final_norm, s_len)
```

```python
import functools
import math

import numpy as np
import jax
import jax.numpy as jnp
from jax import lax
from jax.experimental import pallas as pl
from jax.experimental.pallas import tpu as pltpu

F32 = jnp.float32
BF16 = jnp.bfloat16

D_MODEL = 1024
GRID_W = 64
ROPE_BASE = 10000.0
NORM_EPS = 1e-6
DIFF_EPS = 1e-5
NEG = -1e30

SWA_WINDOW = 128
MLA_SCALE = (64 + 32) ** -0.5
HEAD_SCALE = 64 ** -0.5
LOG2E = math.log2(math.e)

N_GROUPS = 4
EXPERTS_PER_GROUP = 8
N_EXPERTS = 32
TOP_K = 2
EXPERT_FF = 512

LANES = 128
TM = 256
TK = 256
TQ = 1024
TMOE = 256
VMEM_LIMIT = 56 * 1024 * 1024


def _cparams(n_axes):
    return pltpu.CompilerParams(dimension_semantics=("arbitrary",) * n_axes,
                                vmem_limit_bytes=VMEM_LIMIT)


def _rms(x, g, eps):
    return x * lax.rsqrt(jnp.mean(x * x, axis=-1, keepdims=True) + eps) * g


def _rope_block(x, c, s):
    return x * c + pltpu.roll(x, 64, 1) * s


_PAIR_PERM = np.concatenate([np.arange(0, 32), np.arange(64, 96), np.arange(32, 64), np.arange(96, 128)])


def _rope_tables(s_len, c_len):
    rows = s_len // GRID_W
    row = jnp.repeat(jnp.arange(rows, dtype=F32), GRID_W)
    col = jnp.tile(jnp.arange(GRID_W, dtype=F32), rows)

    def tab(dim):
        nf = dim // 4
        inv = ROPE_BASE ** (-jnp.arange(nf, dtype=F32) / nf)
        ang = jnp.concatenate([row[:, None] * inv, col[:, None] * inv], axis=-1)
        return jnp.cos(ang), jnp.sin(ang)

    cos64, sin64 = tab(64)
    cos32, sin32 = tab(32)
    c64 = jnp.concatenate([cos64] * 4, axis=-1)
    s64 = jnp.concatenate([-sin64, -sin64, sin64, sin64], axis=-1)
    one = jnp.ones((s_len, 32), F32)
    zero = jnp.zeros((s_len, 32), F32)
    c32 = jnp.concatenate([cos32, cos32, one, cos32, cos32, one], axis=-1)
    s32 = jnp.concatenate([-sin32, -sin32, zero, sin32, sin32, zero], axis=-1)

    def ext(t, fill):
        return jnp.concatenate([t, jnp.full((c_len, LANES), fill, F32)], axis=0)

    return ext(c64, 1.0), ext(s64, 0.0), ext(c32, 1.0), ext(s32, 0.0)


def _ada_kernel(x_ref, w_ref, b_ref, o_ref):
    x = x_ref[...]
    sx = x * jax.nn.sigmoid(x)
    o_ref[0] = jnp.dot(sx.astype(BF16), w_ref[0].astype(BF16), preferred_element_type=F32) + b_ref[0]


def _ada_call(rows, ada_w, ada_b):
    depth, d, n6 = ada_w.shape
    r = rows.shape[0]
    tn = 1536
    return pl.pallas_call(
        _ada_kernel,
        grid=(depth, n6 // tn),
        in_specs=[pl.BlockSpec((r, d), lambda l, j: (0, 0)),
                  pl.BlockSpec((1, d, tn), lambda l, j: (l, 0, j)),
                  pl.BlockSpec((1, 1, tn), lambda l, j: (l, 0, j))],
        out_specs=pl.BlockSpec((1, r, tn), lambda l, j: (l, 0, j)),
        out_shape=jax.ShapeDtypeStruct((depth, r, n6), F32),
        compiler_params=_cparams(2),
        name="ada_mod",
    )(rows, ada_w, ada_b.reshape(depth, 1, n6))


def _first_layer_x(x_ref, ctx_ref, nlat):
    return jnp.where(pl.program_id(1) < nlat, x_ref[0], ctx_ref[0])


def _moe_residual(x_ref, ya_ref, yb_ref, ga_ref, gb_ref, pmod_ref):
    reps = x_ref.shape[-1] // LANES
    ga = jnp.concatenate([ga_ref[0]] * reps, axis=1)
    gb = jnp.concatenate([gb_ref[0]] * reps, axis=1)
    y = ga * ya_ref[0] + gb * yb_ref[0]
    return x_ref[0] + pmod_ref[0, 0][5:6, :] * y


def _prenorm(has_res, nlat, x_refs, mod_ref, g_ref, xo_ref):
    if has_res:
        x = _moe_residual(*x_refs)
        xo_ref[0] = x
    else:
        x = _first_layer_x(x_refs[0], x_refs[1], nlat)
    m = mod_ref[0, 0]
    return _rms(x, g_ref[...], NORM_EPS) * (1.0 + m[1:2, :]) + m[0:1, :]


def _ab_proj_kernel(has_res, nlat, *refs):
    n_x = 6 if has_res else 2
    x_refs, refs = refs[:n_x], refs[n_x:]
    (mod_ref, g_ref, w1_ref, qn_ref, kvn_ref, wuq_ref, wukv_ref,
     c64_ref, s64_ref, c32_ref, s32_ref) = refs[:11]
    outs = refs[11:]
    if has_res:
        xo_ref, outs = outs[0], outs[1:]
    else:
        xo_ref = None
    sq_ref, sk_ref, sv_ref, mq_ref, mk_ref, mv_ref = outs

    h = _prenorm(has_res, nlat, x_refs, mod_ref, g_ref, xo_ref)
    p = jnp.dot(h.astype(BF16), w1_ref[...], preferred_element_type=F32)
    c64, s64, c32, s32 = c64_ref[...], s64_ref[...], c32_ref[...], s32_ref[...]
    for j in range(4):
        blk = _rope_block(p[:, j * 128:(j + 1) * 128], c64, s64)
        sq_ref[0, :, j * 128:(j + 1) * 128] = (blk * (HEAD_SCALE * LOG2E)).astype(BF16)
    for j in range(2):
        blk = _rope_block(p[:, 512 + j * 128:512 + (j + 1) * 128], c64, s64)
        sk_ref[0, :, j * 128:(j + 1) * 128] = blk.astype(BF16)
    sv_ref[0, 0] = p[:, 768:1024].T.astype(BF16)
    cq = p[:, 1024:1280]
    ckv = p[:, 1280:1408]
    kr = _rope_block(p[:, 1408:1536], c32, s32).astype(BF16)
    qm = jnp.dot(_rms(cq, qn_ref[...], NORM_EPS).astype(BF16), wuq_ref[...], preferred_element_type=F32)
    kv = jnp.dot(_rms(ckv, kvn_ref[...], NORM_EPS).astype(BF16), wukv_ref[...], preferred_element_type=F32)
    for j in range(4):
        mq_ref[0, :, j * 256:j * 256 + 128] = (qm[:, j * 256:j * 256 + 128] * (MLA_SCALE * LOG2E)).astype(BF16)
        rr = _rope_block(qm[:, j * 256 + 128:(j + 1) * 256], c32, s32)
        mq_ref[0, :, j * 256 + 128:(j + 1) * 256] = (rr * (MLA_SCALE * LOG2E)).astype(BF16)
        mk_ref[0, :, j * 256:j * 256 + 128] = kv[:, j * 128:(j + 1) * 128].astype(BF16)
        mk_ref[0, :, j * 256 + 128:(j + 1) * 256] = kr
    mv_ref[0, 0] = kv[:, 512:1024].T.astype(BF16)


def _diff_proj_kernel(has_res, nlat, *refs):
    n_x = 6 if has_res else 2
    x_refs, refs = refs[:n_x], refs[n_x:]
    mod_ref, g_ref, w_ref, c64_ref, s64_ref = refs[:5]
    outs = refs[5:]
    if has_res:
        xo_ref, outs = outs[0], outs[1:]
    else:
        xo_ref = None
    q_ref, k_ref, v_ref = outs
    h = _prenorm(has_res, nlat, x_refs, mod_ref, g_ref, xo_ref)
    p = jnp.dot(h.astype(BF16), w_ref[...], preferred_element_type=F32)
    c64, s64 = c64_ref[...], s64_ref[...]
    for j in range(8):
        blk = _rope_block(p[:, j * 128:(j + 1) * 128], c64, s64)
        q_ref[0, :, j * 128:(j + 1) * 128] = (blk * (HEAD_SCALE * LOG2E)).astype(BF16)
        blk = _rope_block(p[:, 1024 + j * 128:1024 + (j + 1) * 128], c64, s64)
        k_ref[0, :, j * 128:(j + 1) * 128] = blk.astype(BF16)
    v_ref[0, 0] = p[:, 2048:3072].T.astype(BF16)


def _row_spec(width):
    return pl.BlockSpec((1, TM, width), lambda b, i: (b, i, 0))


def _mod_spec(nlat):
    return pl.BlockSpec((1, 1, 8, D_MODEL), lambda b, i: (b, i // nlat, 0, 0))


def _full_spec(shape):
    nd = len(shape)
    return pl.BlockSpec(shape, lambda b, i: (0,) * nd)


def _tab_spec():
    return pl.BlockSpec((TM, LANES), lambda b, i: (i, 0))


def _split_specs(nlat, d):
    return [pl.BlockSpec((1, TM, d), lambda b, i: (b, jnp.minimum(i, nlat - 1), 0)),
            pl.BlockSpec((1, TM, d), lambda b, i: (b, 0, 0))]


def _proj_call(kind, x, res, mod, g, weights, tables, out_widths):
    has_res = res is not None
    if has_res:
        bsz, t, d = x.shape
    else:
        bsz, t, d = x[0].shape[0], x[0].shape[1] + x[1].shape[1], x[0].shape[2]
    nt = t // TM
    nlat = nt - 1
    if has_res:
        ya, yb, ga, gb, pmod = res
        ins = [x, ya, yb, ga, gb, pmod]
        specs = [_row_spec(d), _row_spec(d), _row_spec(d), _row_spec(LANES), _row_spec(LANES), _mod_spec(nlat)]
    else:
        ins = list(x)
        specs = _split_specs(nlat, d)
    ins += [mod, g.reshape(1, d)]
    specs += [_mod_spec(nlat), _full_spec((1, d))]
    for w in weights:
        ins.append(w)
        specs.append(_full_spec(w.shape))
    for tb in tables:
        ins.append(tb)
        specs.append(_tab_spec())
    out_shapes, out_specs = [], []
    if has_res:
        out_shapes.append(jax.ShapeDtypeStruct((bsz, t, d), F32))
        out_specs.append(_row_spec(d))
    for w in out_widths:
        if w < 0:
            out_shapes.append(jax.ShapeDtypeStruct((bsz, nt, -w, TM), BF16))
            out_specs.append(pl.BlockSpec((1, 1, -w, TM), lambda b, i: (b, i, 0, 0)))
        else:
            out_shapes.append(jax.ShapeDtypeStruct((bsz, t, w), BF16))
            out_specs.append(_row_spec(w))
    body = _ab_proj_kernel if kind == "ab" else _diff_proj_kernel
    outs = pl.pallas_call(
        functools.partial(body, has_res, nlat),
        grid=(bsz, nt),
        in_specs=specs,
        out_specs=out_specs,
        out_shape=out_shapes,
        compiler_params=_cparams(2),
        name=kind + "_proj",
    )(*ins)
    if has_res:
        return outs[0], outs[1:]
    return x, outs


def _pair_masks(mode, lane):
    if mode == "mla":
        in_a = (lane < 64) | ((lane >= 128) & (lane < 144)) | ((lane >= 192) & (lane < 208))
        in_b = ((lane >= 64) & (lane < 128)) | ((lane >= 144) & (lane < 160)) | ((lane >= 208) & (lane < 224))
    else:
        in_a = (lane < 32) | ((lane >= 64) & (lane < 96))
        in_b = ((lane >= 32) & (lane < 64)) | (lane >= 96)
    return in_a, in_b


def _dense_attn_kernel(mode, lambda_init, nlat, *refs):
    n_in = 8 if mode == "diff" else 3
    q_ref, k_ref, vt_ref = refs[:3]
    o_ref = refs[n_in]
    s_bufs = refs[n_in + 1:n_in + 3]
    p_bufs = refs[n_in + 3:n_in + 5]
    acc_buf, m_buf, l_buf, a_buf = refs[n_in + 5:]
    i = pl.program_id(2)
    qt = q_ref[0].T
    width, tq = qt.shape
    row = lax.broadcasted_iota(jnp.int32, (width, 1), 0)
    in_a, in_b = _pair_masks(mode, row)
    zero = jnp.zeros_like(qt)
    q2 = jnp.concatenate([jnp.where(in_a, qt, zero), jnp.where(in_b, qt, zero)], axis=1)
    chunks = [(nlat, 1)] + [(2 * c, 2) for c in range(nlat // 2)]
    n_lat_q = (nlat * TK) // tq

    def scores(chunk, s_buf):
        k0, n = chunk
        s_buf[0:n * TK, :] = jnp.dot(k_ref[0, k0 * TK:(k0 + n) * TK, :], q2, preferred_element_type=F32)

    def pv(chunk, p_buf):
        k0, n = chunk
        out = jnp.dot(vt_ref[0, k0], p_buf[0:TK, :], preferred_element_type=F32)
        for r in range(1, n):
            out = out + jnp.dot(vt_ref[0, k0 + r], p_buf[r * TK:(r + 1) * TK, :], preferred_element_type=F32)
        return out

    def softmax(chunk, s_buf, p_buf, first):
        n = chunk[1]
        s = s_buf[0:n * TK, :]
        mx = jnp.max(s, axis=0, keepdims=True)
        if first:
            mn = mx
        else:
            m = m_buf[...]
            mn = jnp.maximum(m, mx)
            a_buf[...] = jnp.exp2(m - mn)
        p = jnp.exp2(s - mn)
        ps = jnp.sum(p, axis=0, keepdims=True)
        l_buf[...] = ps if first else a_buf[...] * l_buf[...] + ps
        m_buf[...] = mn
        p_buf[0:n * TK, :] = p.astype(BF16)

    def accumulate(chunk, p_buf, first):
        if first:
            acc_buf[...] = pv(chunk, p_buf)
        else:
            acc_buf[...] = a_buf[...] * acc_buf[...] + pv(chunk, p_buf)

    def pipeline(chs):
        scores(chs[0], s_bufs[0])
        for c, ch in enumerate(chs):
            if c >= 2:
                accumulate(chs[c - 1], p_bufs[(c - 1) % 2], first=False)
            if c + 1 < len(chs):
                scores(chs[c + 1], s_bufs[(c + 1) % 2])
            softmax(ch, s_bufs[c % 2], p_bufs[c % 2], first=c == 0)
            if c == 1:
                accumulate(chs[0], p_bufs[0], first=True)
        last = len(chs) - 1
        if last == 0:
            accumulate(chs[0], p_bufs[0], first=True)
        else:
            accumulate(chs[last], p_bufs[last % 2], first=False)

    @pl.when(i < n_lat_q)
    def _():
        pipeline(chunks)

    @pl.when(i >= n_lat_q)
    def _():
        pipeline(chunks[:1])

    o2 = acc_buf[...] * (1.0 / l_buf[...])
    oa, ob = o2[:, :tq], o2[:, tq:]
    if mode == "diff":
        lq1_ref, lk1_ref, lq2_ref, lk2_ref, sub_ref = refs[3:8]
        lam = (jnp.exp(jnp.sum(lq1_ref[...] * lk1_ref[...], axis=1, keepdims=True))
               - jnp.exp(jnp.sum(lq2_ref[...] * lk2_ref[...], axis=1, keepdims=True)) + lambda_init)
        o = (oa - lam * ob).T
        o = _rms(o, sub_ref[...], DIFF_EPS) * (1.0 - lambda_init)
    else:
        vrow = lax.broadcasted_iota(jnp.int32, (LANES, 1), 0)
        o = jnp.where(vrow < 64, oa, ob).T
    o_ref[0] = o.astype(BF16)


def _dense_attn_call(mode, q, k, vt, extra=(), lambda_init=0.0):
    bsz, t, qtot = q.shape
    width = 256 if mode == "mla" else 128
    npairs = qtot // width
    nq = pl.cdiv(t, TQ)
    ins = [q, k, vt]
    specs = [pl.BlockSpec((1, TQ, width), lambda b, j, i: (b, i, j)),
             pl.BlockSpec((1, t, width), lambda b, j, i: (b, 0, j)),
             pl.BlockSpec((1, t // TK, LANES, TK), lambda b, j, i: (b, 0, j, 0))]
    for e in extra:
        ins.append(e)
        specs.append(pl.BlockSpec(e.shape, lambda b, j, i: (0, 0)))
    return pl.pallas_call(
        functools.partial(_dense_attn_kernel, mode, lambda_init, t // TK - 1),
        grid=(bsz, npairs, nq),
        in_specs=specs,
        out_specs=pl.BlockSpec((1, TQ, LANES), lambda b, j, i: (b, i, j)),
        out_shape=jax.ShapeDtypeStruct((bsz, t, npairs * LANES), BF16),
        scratch_shapes=[pltpu.VMEM((2 * TK, 2 * TQ), F32), pltpu.VMEM((2 * TK, 2 * TQ), F32),
                        pltpu.VMEM((2 * TK, 2 * TQ), BF16), pltpu.VMEM((2 * TK, 2 * TQ), BF16),
                        pltpu.VMEM((LANES, 2 * TQ), F32), pltpu.VMEM((1, 2 * TQ), F32),
                        pltpu.VMEM((1, 2 * TQ), F32), pltpu.VMEM((1, 2 * TQ), F32)],
        compiler_params=_cparams(3),
        name=mode + "_attn",
    )(*ins)


def _swa_kernel(nlat, sink_ref, q_ref, k_ref, vt_ref, o_ref):
    g = pl.program_id(1)
    i = pl.program_id(2)
    s_len = nlat * TM
    half = TM // 2
    q = q_ref[0]
    row = lax.broadcasted_iota(jnp.int32, (LANES, 1), 0)
    in_a, in_b = _pair_masks("pair", row)
    cols = []
    for pr in range(2):
        qt = q[:, pr * LANES:(pr + 1) * LANES].T
        zero = jnp.zeros_like(qt)
        cols += [jnp.where(in_a, qt, zero), jnp.where(in_b, qt, zero)]
    q4 = jnp.concatenate(cols, axis=1)

    start0 = pl.multiple_of(jnp.maximum(i * TM - half, 0), half)
    start1 = pl.multiple_of(i * TM, TM)
    start2 = pl.multiple_of(jnp.minimum((i + 1) * TM, s_len + half), half)
    k_cat = jnp.concatenate([k_ref[0, pl.ds(start0, half), :], k_ref[0, pl.ds(start1, TM), :],
                             k_ref[0, pl.ds(start2, half), :], k_ref[0, s_len:s_len + TM, :]], axis=0)
    b0 = jnp.maximum(i - 1, 0)
    b2 = jnp.minimum(i + 1, nlat)
    vt_cat = jnp.concatenate([vt_ref[0, b0][:, half:], vt_ref[0, i], vt_ref[0, b2][:, :half],
                              vt_ref[0, nlat]], axis=1)

    r = lax.broadcasted_iota(jnp.int32, (2 * TM, 1), 0)
    far = -4 * SWA_WINDOW
    lat = i < nlat
    pos0 = jnp.where(lat & (i >= 1), start0 + r, far)
    pos1 = jnp.where(lat, start1 + r - half, far)
    pos2 = jnp.where(lat & (i + 1 < nlat), start2 + r - half - TM, far)
    kpos = jnp.where(r < half, pos0, jnp.where(r < half + TM, pos1, pos2))
    qpos = i * TM + lax.broadcasted_iota(jnp.int32, (1, TM), 1)
    band = jnp.abs(qpos - kpos) <= SWA_WINDOW
    band4 = jnp.concatenate([band] * 4, axis=1)

    s = jnp.dot(k_cat, q4, preferred_element_type=F32)
    s_loc = jnp.where(band4, s[:2 * TM], NEG)
    s_ctx = s[2 * TM:]
    sink = jnp.concatenate([jnp.full((1, TM), sink_ref[4 * g + h], F32) for h in range(4)], axis=1) * LOG2E
    m = jnp.maximum(jnp.maximum(jnp.max(s_loc, axis=0, keepdims=True), jnp.max(s_ctx, axis=0, keepdims=True)), sink)
    p_loc = jnp.exp2(s_loc - m)
    p_ctx = jnp.exp2(s_ctx - m)
    l = jnp.sum(p_loc, axis=0, keepdims=True) + jnp.sum(p_ctx, axis=0, keepdims=True) + jnp.exp2(sink - m)
    p = jnp.concatenate([p_loc, p_ctx], axis=0).astype(BF16)
    o4 = jnp.dot(vt_cat, p, preferred_element_type=F32) * (1.0 / l)
    vrow = lax.broadcasted_iota(jnp.int32, (LANES, 1), 0)
    for pr in range(2):
        oa = o4[:, (2 * pr) * TM:(2 * pr + 1) * TM]
        ob = o4[:, (2 * pr + 1) * TM:(2 * pr + 2) * TM]
        o_ref[0, :, pr * LANES:(pr + 1) * LANES] = jnp.where(vrow < 64, oa, ob).T.astype(BF16)


def _swa_call(sink, q, k, vt):
    bsz, t, qtot = q.shape
    nkv = k.shape[-1] // LANES
    nt = t // TM
    grid_spec = pltpu.PrefetchScalarGridSpec(
        num_scalar_prefetch=1,
        grid=(bsz, nkv, nt),
        in_specs=[pl.BlockSpec((1, TM, 2 * LANES), lambda b, g, i, s: (b, i, g)),
                  pl.BlockSpec((1, t, LANES), lambda b, g, i, s: (b, 0, g)),
                  pl.BlockSpec((1, nt, LANES, TM), lambda b, g, i, s: (b, 0, g, 0))],
        out_specs=pl.BlockSpec((1, TM, 2 * LANES), lambda b, g, i, s: (b, i, g)),
    )
    return pl.pallas_call(
        functools.partial(_swa_kernel, nt - 1),
        grid_spec=grid_spec,
        out_shape=jax.ShapeDtypeStruct((bsz, t, qtot), BF16),
        compiler_params=_cparams(3),
        name="swa_attn",
    )(sink, q, k, vt)


def _out_proj_kernel(n_o, split, nlat, *refs):
    o_refs = refs[:n_o]
    w_refs = refs[n_o:2 * n_o]
    refs = refs[2 * n_o:]
    if split:
        x_in = _first_layer_x(refs[0], refs[1], nlat)
        refs = refs[2:]
    else:
        x_in = refs[0][0]
        refs = refs[1:]
    mod_ref, g_ref, wr_ref, br_ref, xo_ref, h_ref, lg_ref = refs
    acc = jnp.dot(o_refs[0][0], w_refs[0][...], preferred_element_type=F32)
    for n in range(1, n_o):
        acc = acc + jnp.dot(o_refs[n][0], w_refs[n][...], preferred_element_type=F32)
    m = mod_ref[0, 0]
    x = x_in + m[2:3, :] * acc
    xo_ref[0] = x
    h = _rms(x, g_ref[...], NORM_EPS) * (1.0 + m[4:5, :]) + m[3:4, :]
    hi = h.astype(BF16)
    lo = (h - hi.astype(F32)).astype(BF16)
    h_ref[0] = hi
    both = jnp.dot(hi, wr_ref[...], preferred_element_type=F32)
    lg_ref[0] = (both[:, :LANES] + both[:, LANES:]
                 + jnp.dot(lo, wr_ref[:, :LANES], preferred_element_type=F32) + br_ref[...])


def _out_proj_call(os_, ws, x, mod, g, wr, br):
    split = isinstance(x, (tuple, list))
    if split:
        bsz, t, d = x[0].shape[0], x[0].shape[1] + x[1].shape[1], x[0].shape[2]
    else:
        bsz, t, d = x.shape
    nt = t // TM
    nlat = nt - 1
    n_o = len(os_)
    x_ins = list(x) if split else [x]
    x_specs = _split_specs(nlat, d) if split else [_row_spec(d)]
    ins = list(os_) + list(ws) + x_ins + [mod, g.reshape(1, d), wr, br]
    specs = ([_row_spec(o.shape[-1]) for o in os_] + [_full_spec(w.shape) for w in ws]
             + x_specs + [_mod_spec(nlat), _full_spec((1, d)), _full_spec(wr.shape), _full_spec(br.shape)])
    return pl.pallas_call(
        functools.partial(_out_proj_kernel, n_o, split, nlat),
        grid=(bsz, nt),
        in_specs=specs,
        out_specs=[_row_spec(d), _row_spec(d), _row_spec(LANES)],
        out_shape=[jax.ShapeDtypeStruct((bsz, t, d), F32),
                   jax.ShapeDtypeStruct((bsz, t, d), BF16),
                   jax.ShapeDtypeStruct((bsz, t, LANES), F32)],
        compiler_params=_cparams(2),
        name="out_proj",
    )(*ins)


def _moe_kernel(be_ref, nu_ref, x_ref, w1_ref, w3_ref, w2_ref, o_ref, w1c, w3c, w2c):
    i = pl.program_id(0)
    e = be_ref[i]

    @pl.when((i == 0) | (e != be_ref[jnp.maximum(i - 1, 0)]))
    def _():
        w1c[...] = w1_ref[0, 0].astype(BF16)
        w3c[...] = w3_ref[0, 0].astype(BF16)
        w2c[...] = w2_ref[0, 0].astype(BF16)

    @pl.when(i < nu_ref[0])
    def _():
        x = x_ref[...]
        a = jnp.dot(x, w1c[...], preferred_element_type=F32)
        b = jnp.dot(x, w3c[...], preferred_element_type=F32)
        hmid = (a * jax.nn.sigmoid(a)) * b
        o_ref[...] = jnp.dot(hmid.astype(BF16), w2c[...], preferred_element_type=F32)

    @pl.when(i >= nu_ref[0])
    def _():
        o_ref[...] = jnp.zeros_like(o_ref)


def _moe_call(layer, block_e, n_used, xb, w1, w3, w2):
    n_slots, d = xb.shape
    n_blocks = n_slots // TMOE
    ff = w1.shape[-1]
    grid_spec = pltpu.PrefetchScalarGridSpec(
        num_scalar_prefetch=2,
        grid=(n_blocks,),
        in_specs=[pl.BlockSpec((TMOE, d), lambda i, be, nu: (i, 0)),
                  pl.BlockSpec((1, 1, d, ff), lambda i, be, nu: (layer, be[i], 0, 0)),
                  pl.BlockSpec((1, 1, d, ff), lambda i, be, nu: (layer, be[i], 0, 0)),
                  pl.BlockSpec((1, 1, ff, d), lambda i, be, nu: (layer, be[i], 0, 0))],
        out_specs=pl.BlockSpec((TMOE, d), lambda i, be, nu: (i, 0)),
        scratch_shapes=[pltpu.VMEM((d, ff), BF16), pltpu.VMEM((d, ff), BF16), pltpu.VMEM((ff, d), BF16)],
    )
    return pl.pallas_call(
        _moe_kernel,
        grid_spec=grid_spec,
        out_shape=jax.ShapeDtypeStruct((n_slots, d), F32),
        compiler_params=_cparams(1),
        name="moe_experts",
    )(block_e, n_used, xb, w1, w3, w2)


def _router_kernel(lg_ref, tri_ref, rt_ref, cnt_ref, ga_ref, gb_ref, run):
    i = pl.program_id(0)

    @pl.when(i == 0)
    def _():
        run[...] = jnp.zeros_like(run)

    lg = lg_ref[...]
    lane = lax.broadcasted_iota(jnp.int32, lg.shape, 1)
    gmask = lane < N_GROUPS
    gl = jnp.where(gmask, lg, NEG)
    gmax = jnp.max(gl, axis=1, keepdims=True)
    grp = jnp.min(jnp.where(gl == gmax, lane, LANES), axis=1, keepdims=True)
    p_grp = 1.0 / jnp.sum(jnp.where(gmask, jnp.exp(lg - gmax), 0.0), axis=1, keepdims=True)
    first = N_GROUPS + EXPERTS_PER_GROUP * grp
    emask = (lane >= first) & (lane < first + EXPERTS_PER_GROUP)
    el = jnp.where(emask, lg, NEG)
    e1 = jnp.max(el, axis=1, keepdims=True)
    i1 = jnp.min(jnp.where(el == e1, lane, LANES), axis=1, keepdims=True)
    el2 = jnp.where(lane == i1, NEG, el)
    e2 = jnp.max(el2, axis=1, keepdims=True)
    i2 = jnp.min(jnp.where(el2 == e2, lane, LANES), axis=1, keepdims=True)
    tt = jnp.exp(e2 - e1)
    g1 = p_grp / (1.0 + tt)
    g2 = g1 * tt
    oh1 = lane == i1
    oh2 = lane == i2
    onehot = jnp.where(oh1 | oh2, 1.0, 0.0)
    rank_all = jnp.dot(tri_ref[...], onehot.astype(BF16), preferred_element_type=F32) + run[...]
    r1 = jnp.sum(jnp.where(oh1, rank_all, 0.0), axis=1, keepdims=True)
    r2 = jnp.sum(jnp.where(oh2, rank_all, 0.0), axis=1, keepdims=True)
    run[...] = run[...] + jnp.sum(onehot, axis=0, keepdims=True)
    cnt_ref[...] = run[...]
    vals = [(i1 - N_GROUPS).astype(F32), (i2 - N_GROUPS).astype(F32), r1, r2, g1, g2]
    packed = jnp.zeros(lg.shape, F32)
    for n, v in enumerate(vals):
        packed = jnp.where(lane == n, v, packed)
    rt_ref[...] = packed.T[0:8, :]
    ga_ref[...] = jnp.broadcast_to(g1, lg.shape)
    gb_ref[...] = jnp.broadcast_to(g2, lg.shape)


def _router_call(logits):
    n_tok = logits.shape[0]
    tri = (jnp.arange(TM)[:, None] > jnp.arange(TM)[None, :]).astype(BF16)
    return pl.pallas_call(
        _router_kernel,
        grid=(n_tok // TM,),
        in_specs=[pl.BlockSpec((TM, LANES), lambda i: (i, 0)),
                  pl.BlockSpec((TM, TM), lambda i: (0, 0))],
        out_specs=[pl.BlockSpec((8, TM), lambda i: (0, i)),
                   pl.BlockSpec((1, LANES), lambda i: (0, 0)),
                   pl.BlockSpec((TM, LANES), lambda i: (i, 0)),
                   pl.BlockSpec((TM, LANES), lambda i: (i, 0))],
        out_shape=[jax.ShapeDtypeStruct((8, n_tok), F32), jax.ShapeDtypeStruct((1, LANES), F32),
                   jax.ShapeDtypeStruct((n_tok, LANES), F32), jax.ShapeDtypeStruct((n_tok, LANES), F32)],
        scratch_shapes=[pltpu.VMEM((1, LANES), F32)],
        compiler_params=_cparams(1),
        name="router",
    )(logits, tri)


def _dispatch_plan(rt, cnt):
    n_tok = rt.shape[1]
    eid = rt[0:2].astype(jnp.int32)
    rank = rt[2:4].astype(jnp.int32)
    counts = cnt[0, N_GROUPS:N_GROUPS + N_EXPERTS].astype(jnp.int32)
    padded = (counts + TMOE - 1) // TMOE * TMOE
    pad_end = jnp.cumsum(padded)
    pad_start = pad_end - padded
    experts = jnp.arange(N_EXPERTS, dtype=jnp.int32)[:, None, None]
    dest = jnp.sum(jnp.where(eid[None] == experts, pad_start[:, None, None], 0), axis=0) + rank
    n_assign = n_tok * TOP_K
    n_blocks = (n_assign + N_EXPERTS * (TMOE - 1) + TMOE - 1) // TMOE
    n_slots = n_blocks * TMOE
    tok = jnp.arange(n_tok, dtype=jnp.int32)
    slot_tok = jnp.zeros((n_slots,), jnp.int32).at[dest.reshape(n_assign)].set(jnp.concatenate([tok, tok]))
    block_start = jnp.arange(n_blocks, dtype=jnp.int32) * TMOE
    block_e = jnp.minimum(jnp.sum((block_start[:, None] >= pad_end[None, :]).astype(jnp.int32), axis=1),
                          N_EXPERTS - 1).astype(jnp.int32)
    n_used = (pad_end[-1:] // TMOE).astype(jnp.int32)
    return slot_tok, block_e, n_used, dest


def _final_kernel(x_ref, ya_ref, yb_ref, ga_ref, gb_ref, pmod_ref, g_ref, o_ref):
    x = _moe_residual(x_ref, ya_ref, yb_ref, ga_ref, gb_ref, pmod_ref)
    o_ref[0] = _rms(x, g_ref[...], NORM_EPS)


def _final_call(x, res, g, s_len):
    bsz, t, d = x.shape
    ya, yb, ga, gb, pmod = res
    return pl.pallas_call(
        _final_kernel,
        grid=(bsz, s_len // TM),
        in_specs=[_row_spec(d), _row_spec(d), _row_spec(d), _row_spec(LANES), _row_spec(LANES),
                  pl.BlockSpec((1, 1, 8, d), lambda b, i: (b, 0, 0, 0)),
                  _full_spec((1, d))],
        out_specs=_row_spec(d),
        out_shape=jax.ShapeDtypeStruct((bsz, s_len, d), F32),
        compiler_params=_cparams(2),
        name="final_norm",
    )(x, ya, yb, ga, gb, pmod, g.reshape(1, d))


def _take_cols(w, idx):
    wz = jnp.concatenate([w, jnp.zeros((w.shape[0], 1), w.dtype)], axis=1)
    return jnp.take(wz, jnp.asarray(idx, dtype=jnp.int32), axis=1).astype(BF16)


def _ab_layouts():
    zc = 1184
    cols = []
    for j in range(4):
        cols.append(j * 128 + _PAIR_PERM)
    for g in range(2):
        base = 512 + g * 64
        cols.append(base + np.concatenate([np.arange(0, 32), np.arange(0, 32), np.arange(32, 64), np.arange(32, 64)]))
    for g in range(2):
        base = 640 + g * 64
        cols.append(base + np.concatenate([np.arange(64), np.arange(64)]))
    cols.append(768 + np.arange(256))
    cols.append(1024 + np.arange(128))
    kr = 1152
    z32 = np.full((32,), zc)
    cols.append(np.concatenate([kr + np.arange(16), kr + np.arange(16), z32,
                                kr + 16 + np.arange(16), kr + 16 + np.arange(16), z32]))
    w1_idx = np.concatenate(cols)

    zq = 768
    uq = []
    z32q = np.full((32,), zq)
    for j in range(4):
        a, b = 2 * j * 96, (2 * j + 1) * 96
        uq.append(np.concatenate([a + np.arange(64), b + np.arange(64),
                                  a + 64 + np.arange(16), b + 64 + np.arange(16), z32q,
                                  a + 80 + np.arange(16), b + 80 + np.arange(16), z32q]))
    uq_idx = np.concatenate(uq)

    kn, mv = [], []
    for h in range(8):
        kn.append(h * 128 + np.arange(64))
        mv.append(h * 128 + 64 + np.arange(64))
    ukv_idx = np.concatenate(kn + mv)
    return w1_idx, uq_idx, ukv_idx


def _diff_layout():
    cols = []
    for part in range(2):
        for h in range(8):
            cols.append(part * 1024 + h * 128 + _PAIR_PERM)
    cols.append(2048 + np.arange(1024))
    return np.concatenate(cols)


def kernel(x, c, ctx, c_ctx, norm_mix, norm_ffn, ada_w, ada_b, ab_w_in, mla_q_norm, mla_w_uq, mla_kv_norm, mla_w_ukv, swa_sink, ab_w_out, diff_w_in, diff_lambda_q1, diff_lambda_k1, diff_lambda_q2, diff_lambda_k2, diff_subln, diff_w_out, router_group_w, router_group_b, router_expert_w, router_expert_b, expert_w1, expert_w3, expert_w2, final_norm):
    bsz, s_len, d = x.shape
    c_len = ctx.shape[1]
    depth = ada_w.shape[0]
    assert d == D_MODEL and c_len == TM and s_len % TQ == 0 and s_len % (2 * TK) == 0
    t = s_len + c_len

    xs = (x, ctx)
    tables = _rope_tables(s_len, c_len)
    c64, s64, c32, s32 = tables

    n_rows = (bsz + 1 + 7) // 8 * 8
    rows = jnp.concatenate([c, c_ctx[None, :], jnp.zeros((n_rows - bsz - 1, d), F32)], axis=0)
    mod_all = _ada_call(rows, ada_w, ada_b)
    mod_lat = mod_all[:, :bsz].reshape(depth, bsz, 1, 6, d)
    mod_ctx = jnp.broadcast_to(mod_all[:, bsz].reshape(depth, 1, 1, 6, d), (depth, bsz, 1, 6, d))
    mods = jnp.concatenate([mod_lat, mod_ctx], axis=2)
    mods = jnp.concatenate([mods, jnp.zeros((depth, bsz, 2, 2, d), F32)], axis=3)

    w1_idx, uq_idx, ukv_idx = _ab_layouts()
    diff_idx = _diff_layout()

    res = None
    for l in range(depth):
        j = l // 2
        mod = mods[l]
        if l % 2 == 0:
            weights = [_take_cols(ab_w_in[j], w1_idx), mla_q_norm[j].reshape(1, -1), mla_kv_norm[j].reshape(1, -1),
                       _take_cols(mla_w_uq[j], uq_idx), _take_cols(mla_w_ukv[j], ukv_idx)]
            xs, (sq, sk, sv, mq, mk, mv) = _proj_call("ab", xs, res, mod, norm_mix[l], weights,
                                                      [c64, s64, c32, s32], [512, 256, -256, 1024, 1024, -512])
            o_a = _swa_call(swa_sink[j], sq, sk, sv)
            o_b = _dense_attn_call("mla", mq, mk, mv)
            w_out = ab_w_out[j].astype(BF16)
            attn_outs, out_ws = [o_a, o_b], [w_out[:512], w_out[512:]]
        else:
            lambda_init = 0.8 - 0.6 * math.exp(-0.3 * l)
            weights = [_take_cols(diff_w_in[j], diff_idx)]
            xs, (dq, dk, dv) = _proj_call("diff", xs, res, mod, norm_mix[l], weights, [c64, s64], [1024, 1024, -1024])
            extra = [diff_lambda_q1[j].reshape(1, -1), diff_lambda_k1[j].reshape(1, -1),
                     diff_lambda_q2[j].reshape(1, -1), diff_lambda_k2[j].reshape(1, -1),
                     diff_subln[j].reshape(1, -1)]
            o_d = _dense_attn_call("diff", dq, dk, dv, extra=extra, lambda_init=lambda_init)
            attn_outs, out_ws = [o_d], [diff_w_out[j].astype(BF16)]

        wr = jnp.concatenate([router_group_w[l], router_expert_w[l],
                              jnp.zeros((d, LANES - N_GROUPS - N_EXPERTS), F32)], axis=1)
        wr_hi = wr.astype(BF16)
        wr = jnp.concatenate([wr_hi, (wr - wr_hi.astype(F32)).astype(BF16)], axis=1)
        br = jnp.concatenate([router_group_b[l], router_expert_b[l],
                              jnp.zeros((LANES - N_GROUPS - N_EXPERTS,), F32)]).reshape(1, LANES)
        xs, h2, logits = _out_proj_call(attn_outs, out_ws, xs, mod, norm_ffn[l], wr, br)

        n_tok = bsz * t
        rt, cnt, ga, gb = _router_call(logits.reshape(n_tok, LANES))
        slot_tok, block_e, n_used, dest = _dispatch_plan(rt, cnt)
        xb = jnp.take(h2.reshape(n_tok, d), slot_tok, axis=0, mode="clip")
        yb = _moe_call(l, block_e, n_used, xb, expert_w1, expert_w3, expert_w2)
        res = (jnp.take(yb, dest[0], axis=0, mode="clip").reshape(bsz, t, d),
               jnp.take(yb, dest[1], axis=0, mode="clip").reshape(bsz, t, d),
               ga.reshape(bsz, t, LANES), gb.reshape(bsz, t, LANES), mod)

    return _final_call(xs, res, final_norm, s_len)
```

```python
import functools
import math

import numpy as np
import jax
import jax.numpy as jnp
from jax import lax
from jax.experimental import pallas as pl
from jax.experimental.pallas import tpu as pltpu

F32 = jnp.float32
BF16 = jnp.bfloat16

D_MODEL = 1024
GRID_W = 64
ROPE_BASE = 10000.0
NORM_EPS = 1e-6
DIFF_EPS = 1e-5
NEG = -1e30

SWA_WINDOW = 128
MLA_SCALE = (64 + 32) ** -0.5
HEAD_SCALE = 64 ** -0.5
LOG2E = math.log2(math.e)

N_GROUPS = 4
EXPERTS_PER_GROUP = 8
N_EXPERTS = 32
TOP_K = 2
EXPERT_FF = 512

LANES = 128
TM = 256
TK = 256
TQ = 1024
TMOE = 256
VMEM_LIMIT = 56 * 1024 * 1024


def _cparams(n_axes):
    return pltpu.CompilerParams(dimension_semantics=("arbitrary",) * n_axes,
                                vmem_limit_bytes=VMEM_LIMIT)


def _rms(x, g, eps):
    return x * lax.rsqrt(jnp.mean(x * x, axis=-1, keepdims=True) + eps) * g


def _rope_block(x, c, s):
    return x * c + pltpu.roll(x, 64, 1) * s


_PAIR_PERM = np.concatenate([np.arange(0, 32), np.arange(64, 96), np.arange(32, 64), np.arange(96, 128)])


def _rope_tables(s_len, c_len):
    rows = s_len // GRID_W
    row = jnp.repeat(jnp.arange(rows, dtype=F32), GRID_W)
    col = jnp.tile(jnp.arange(GRID_W, dtype=F32), rows)

    def tab(dim):
        nf = dim // 4
        inv = ROPE_BASE ** (-jnp.arange(nf, dtype=F32) / nf)
        ang = jnp.concatenate([row[:, None] * inv, col[:, None] * inv], axis=-1)
        return jnp.cos(ang), jnp.sin(ang)

    cos64, sin64 = tab(64)
    cos32, sin32 = tab(32)
    c64 = jnp.concatenate([cos64] * 4, axis=-1)
    s64 = jnp.concatenate([-sin64, -sin64, sin64, sin64], axis=-1)
    one = jnp.ones((s_len, 32), F32)
    zero = jnp.zeros((s_len, 32), F32)
    c32 = jnp.concatenate([cos32, cos32, one, cos32, cos32, one], axis=-1)
    s32 = jnp.concatenate([-sin32, -sin32, zero, sin32, sin32, zero], axis=-1)

    def ext(t, fill):
        return jnp.concatenate([t, jnp.full((c_len, LANES), fill, F32)], axis=0)

    return ext(c64, 1.0), ext(s64, 0.0), ext(c32, 1.0), ext(s32, 0.0)


def _ada_kernel(x_ref, w_ref, b_ref, o_ref):
    x = x_ref[...]
    sx = x * jax.nn.sigmoid(x)
    o_ref[0] = jnp.dot(sx.astype(BF16), w_ref[0].astype(BF16), preferred_element_type=F32) + b_ref[0]


def _ada_call(rows, ada_w, ada_b):
    depth, d, n6 = ada_w.shape
    r = rows.shape[0]
    tn = 1536
    return pl.pallas_call(
        _ada_kernel,
        grid=(depth, n6 // tn),
        in_specs=[pl.BlockSpec((r, d), lambda l, j: (0, 0)),
                  pl.BlockSpec((1, d, tn), lambda l, j: (l, 0, j)),
                  pl.BlockSpec((1, 1, tn), lambda l, j: (l, 0, j))],
        out_specs=pl.BlockSpec((1, r, tn), lambda l, j: (l, 0, j)),
        out_shape=jax.ShapeDtypeStruct((depth, r, n6), F32),
        compiler_params=_cparams(2),
        name="ada_mod",
    )(rows, ada_w, ada_b.reshape(depth, 1, n6))


def _first_layer_x(x_ref, ctx_ref, nlat):
    return jnp.where(pl.program_id(1) < nlat, x_ref[0], ctx_ref[0])


def _moe_residual(x_ref, ya_ref, yb_ref, ga_ref, gb_ref, pmod_ref):
    reps = x_ref.shape[-1] // LANES
    ga = jnp.concatenate([ga_ref[0]] * reps, axis=1)
    gb = jnp.concatenate([gb_ref[0]] * reps, axis=1)
    y = ga * ya_ref[0] + gb * yb_ref[0]
    return x_ref[0] + pmod_ref[0, 0][5:6, :] * y


def _prenorm(has_res, nlat, x_refs, mod_ref, g_ref, xo_ref):
    if has_res:
        x = _moe_residual(*x_refs)
        xo_ref[0] = x
    else:
        x = _first_layer_x(x_refs[0], x_refs[1], nlat)
    m = mod_ref[0, 0]
    return _rms(x, g_ref[...], NORM_EPS) * (1.0 + m[1:2, :]) + m[0:1, :]


def _ab_proj_kernel(has_res, nlat, *refs):
    n_x = 6 if has_res else 2
    x_refs, refs = refs[:n_x], refs[n_x:]
    (mod_ref, g_ref, w1_ref, qn_ref, kvn_ref, wuq_ref, wukv_ref,
     c64_ref, s64_ref, c32_ref, s32_ref) = refs[:11]
    outs = refs[11:]
    if has_res:
        xo_ref, outs = outs[0], outs[1:]
    else:
        xo_ref = None
    sq_ref, sk_ref, sv_ref, mq_ref, mk_ref, mv_ref = outs

    h = _prenorm(has_res, nlat, x_refs, mod_ref, g_ref, xo_ref)
    p = jnp.dot(h.astype(BF16), w1_ref[...], preferred_element_type=F32)
    c64, s64, c32, s32 = c64_ref[...], s64_ref[...], c32_ref[...], s32_ref[...]
    for j in range(4):
        blk = _rope_block(p[:, j * 128:(j + 1) * 128], c64, s64)
        sq_ref[0, :, j * 128:(j + 1) * 128] = (blk * (HEAD_SCALE * LOG2E)).astype(BF16)
    for j in range(2):
        blk = _rope_block(p[:, 512 + j * 128:512 + (j + 1) * 128], c64, s64)
        sk_ref[0, :, j * 128:(j + 1) * 128] = blk.astype(BF16)
    sv_ref[0, 0] = p[:, 768:1024].T.astype(BF16)
    cq = p[:, 1024:1280]
    ckv = p[:, 1280:1408]
    kr = _rope_block(p[:, 1408:1536], c32, s32).astype(BF16)
    qm = jnp.dot(_rms(cq, qn_ref[...], NORM_EPS).astype(BF16), wuq_ref[...], preferred_element_type=F32)
    kv = jnp.dot(_rms(ckv, kvn_ref[...], NORM_EPS).astype(BF16), wukv_ref[...], preferred_element_type=F32)
    for j in range(4):
        mq_ref[0, :, j * 256:j * 256 + 128] = (qm[:, j * 256:j * 256 + 128] * (MLA_SCALE * LOG2E)).astype(BF16)
        rr = _rope_block(qm[:, j * 256 + 128:(j + 1) * 256], c32, s32)
        mq_ref[0, :, j * 256 + 128:(j + 1) * 256] = (rr * (MLA_SCALE * LOG2E)).astype(BF16)
        mk_ref[0, :, j * 256:j * 256 + 128] = kv[:, j * 128:(j + 1) * 128].astype(BF16)
        mk_ref[0, :, j * 256 + 128:(j + 1) * 256] = kr
    mv_ref[0, 0] = kv[:, 512:1024].T.astype(BF16)


def _diff_proj_kernel(has_res, nlat, *refs):
    n_x = 6 if has_res else 2
    x_refs, refs = refs[:n_x], refs[n_x:]
    mod_ref, g_ref, w_ref, c64_ref, s64_ref = refs[:5]
    outs = refs[5:]
    if has_res:
        xo_ref, outs = outs[0], outs[1:]
    else:
        xo_ref = None
    q_ref, k_ref, v_ref = outs
    h = _prenorm(has_res, nlat, x_refs, mod_ref, g_ref, xo_ref)
    p = jnp.dot(h.astype(BF16), w_ref[...], preferred_element_type=F32)
    c64, s64 = c64_ref[...], s64_ref[...]
    for j in range(8):
        blk = _rope_block(p[:, j * 128:(j + 1) * 128], c64, s64)
        q_ref[0, :, j * 128:(j + 1) * 128] = (blk * (HEAD_SCALE * LOG2E)).astype(BF16)
        blk = _rope_block(p[:, 1024 + j * 128:1024 + (j + 1) * 128], c64, s64)
        k_ref[0, :, j * 128:(j + 1) * 128] = blk.astype(BF16)
    v_ref[0, 0] = p[:, 2048:3072].T.astype(BF16)


def _row_spec(width):
    return pl.BlockSpec((1, TM, width), lambda b, i: (b, i, 0))


def _mod_spec(nlat):
    return pl.BlockSpec((1, 1, 8, D_MODEL), lambda b, i: (b, i // nlat, 0, 0))


def _full_spec(shape):
    nd = len(shape)
    return pl.BlockSpec(shape, lambda b, i: (0,) * nd)


def _tab_spec():
    return pl.BlockSpec((TM, LANES), lambda b, i: (i, 0))


def _split_specs(nlat, d):
    return [pl.BlockSpec((1, TM, d), lambda b, i: (b, jnp.minimum(i, nlat - 1), 0)),
            pl.BlockSpec((1, TM, d), lambda b, i: (b, 0, 0))]


def _proj_call(kind, x, res, mod, g, weights, tables, out_widths):
    has_res = res is not None
    if has_res:
        bsz, t, d = x.shape
    else:
        bsz, t, d = x[0].shape[0], x[0].shape[1] + x[1].shape[1], x[0].shape[2]
    nt = t // TM
    nlat = nt - 1
    if has_res:
        ya, yb, ga, gb, pmod = res
        ins = [x, ya, yb, ga, gb, pmod]
        specs = [_row_spec(d), _row_spec(d), _row_spec(d), _row_spec(LANES), _row_spec(LANES), _mod_spec(nlat)]
    else:
        ins = list(x)
        specs = _split_specs(nlat, d)
    ins += [mod, g.reshape(1, d)]
    specs += [_mod_spec(nlat), _full_spec((1, d))]
    for w in weights:
        ins.append(w)
        specs.append(_full_spec(w.shape))
    for tb in tables:
        ins.append(tb)
        specs.append(_tab_spec())
    out_shapes, out_specs = [], []
    if has_res:
        out_shapes.append(jax.ShapeDtypeStruct((bsz, t, d), F32))
        out_specs.append(_row_spec(d))
    for w in out_widths:
        if w < 0:
            out_shapes.append(jax.ShapeDtypeStruct((bsz, nt, -w, TM), BF16))
            out_specs.append(pl.BlockSpec((1, 1, -w, TM), lambda b, i: (b, i, 0, 0)))
        else:
            out_shapes.append(jax.ShapeDtypeStruct((bsz, t, w), BF16))
            out_specs.append(_row_spec(w))
    body = _ab_proj_kernel if kind == "ab" else _diff_proj_kernel
    outs = pl.pallas_call(
        functools.partial(body, has_res, nlat),
        grid=(bsz, nt),
        in_specs=specs,
        out_specs=out_specs,
        out_shape=out_shapes,
        compiler_params=_cparams(2),
        name=kind + "_proj",
    )(*ins)
    if has_res:
        return outs[0], outs[1:]
    return x, outs


def _pair_masks(mode, lane):
    if mode == "mla":
        in_a = (lane < 64) | ((lane >= 128) & (lane < 144)) | ((lane >= 192) & (lane < 208))
        in_b = ((lane >= 64) & (lane < 128)) | ((lane >= 144) & (lane < 160)) | ((lane >= 208) & (lane < 224))
    else:
        in_a = (lane < 32) | ((lane >= 64) & (lane < 96))
        in_b = ((lane >= 32) & (lane < 64)) | (lane >= 96)
    return in_a, in_b


def _dense_attn_kernel(mode, lambda_init, nlat, *refs):
    n_in = 8 if mode == "diff" else 3
    q_ref, k_ref, vt_ref = refs[:3]
    o_ref = refs[n_in]
    s_bufs = refs[n_in + 1:n_in + 3]
    p_bufs = refs[n_in + 3:n_in + 5]
    acc_buf, m_buf, l_buf, a_buf = refs[n_in + 5:]
    i = pl.program_id(2)
    qt = q_ref[0].T
    width, tq = qt.shape
    row = lax.broadcasted_iota(jnp.int32, (width, 1), 0)
    in_a, in_b = _pair_masks(mode, row)
    zero = jnp.zeros_like(qt)
    q2 = jnp.concatenate([jnp.where(in_a, qt, zero), jnp.where(in_b, qt, zero)], axis=1)
    chunks = [(nlat, 1)] + [(2 * c, 2) for c in range(nlat // 2)]
    n_lat_q = (nlat * TK) // tq

    def scores(chunk, s_buf):
        k0, n = chunk
        s_buf[0:n * TK, :] = jnp.dot(k_ref[0, k0 * TK:(k0 + n) * TK, :], q2, preferred_element_type=F32)

    def pv(chunk, p_buf):
        k0, n = chunk
        out = jnp.dot(vt_ref[0, k0], p_buf[0:TK, :], preferred_element_type=F32)
        for r in range(1, n):
            out = out + jnp.dot(vt_ref[0, k0 + r], p_buf[r * TK:(r + 1) * TK, :], preferred_element_type=F32)
        return out

    def softmax(chunk, s_buf, p_buf, first):
        n = chunk[1]
        s = s_buf[0:n * TK, :]
        mx = jnp.max(s, axis=0, keepdims=True)
        if first:
            mn = mx
        else:
            m = m_buf[...]
            mn = jnp.maximum(m, mx)
            a_buf[...] = jnp.exp2(m - mn)
        p = jnp.exp2(s - mn)
        ps = jnp.sum(p, axis=0, keepdims=True)
        l_buf[...] = ps if first else a_buf[...] * l_buf[...] + ps
        m_buf[...] = mn
        p_buf[0:n * TK, :] = p.astype(BF16)

    def accumulate(chunk, p_buf, first):
        if first:
            acc_buf[...] = pv(chunk, p_buf)
        else:
            acc_buf[...] = a_buf[...] * acc_buf[...] + pv(chunk, p_buf)

    def pipeline(chs):
        scores(chs[0], s_bufs[0])
        for c, ch in enumerate(chs):
            if c >= 2:
                accumulate(chs[c - 1], p_bufs[(c - 1) % 2], first=False)
            if c + 1 < len(chs):
                scores(chs[c + 1], s_bufs[(c + 1) % 2])
            softmax(ch, s_bufs[c % 2], p_bufs[c % 2], first=c == 0)
            if c == 1:
                accumulate(chs[0], p_bufs[0], first=True)
        last = len(chs) - 1
        if last == 0:
            accumulate(chs[0], p_bufs[0], first=True)
        else:
            accumulate(chs[last], p_bufs[last % 2], first=False)

    @pl.when(i < n_lat_q)
    def _():
        pipeline(chunks)

    @pl.when(i >= n_lat_q)
    def _():
        pipeline(chunks[:1])

    o2 = acc_buf[...] * (1.0 / l_buf[...])
    oa, ob = o2[:, :tq], o2[:, tq:]
    if mode == "diff":
        lq1_ref, lk1_ref, lq2_ref, lk2_ref, sub_ref = refs[3:8]
        lam = (jnp.exp(jnp.sum(lq1_ref[...] * lk1_ref[...], axis=1, keepdims=True))
               - jnp.exp(jnp.sum(lq2_ref[...] * lk2_ref[...], axis=1, keepdims=True)) + lambda_init)
        o = (oa - lam * ob).T
        o = _rms(o, sub_ref[...], DIFF_EPS) * (1.0 - lambda_init)
    else:
        vrow = lax.broadcasted_iota(jnp.int32, (LANES, 1), 0)
        o = jnp.where(vrow < 64, oa, ob).T
    o_ref[0] = o.astype(BF16)


def _dense_attn_call(mode, q, k, vt, extra=(), lambda_init=0.0):
    bsz, t, qtot = q.shape
    width = 256 if mode == "mla" else 128
    npairs = qtot // width
    nq = pl.cdiv(t, TQ)
    ins = [q, k, vt]
    specs = [pl.BlockSpec((1, TQ, width), lambda b, j, i: (b, i, j)),
             pl.BlockSpec((1, t, width), lambda b, j, i: (b, 0, j)),
             pl.BlockSpec((1, t // TK, LANES, TK), lambda b, j, i: (b, 0, j, 0))]
    for e in extra:
        ins.append(e)
        specs.append(pl.BlockSpec(e.shape, lambda b, j, i: (0, 0)))
    return pl.pallas_call(
        functools.partial(_dense_attn_kernel, mode, lambda_init, t // TK - 1),
        grid=(bsz, npairs, nq),
        in_specs=specs,
        out_specs=pl.BlockSpec((1, TQ, LANES), lambda b, j, i: (b, i, j)),
        out_shape=jax.ShapeDtypeStruct((bsz, t, npairs * LANES), BF16),
        scratch_shapes=[pltpu.VMEM((2 * TK, 2 * TQ), F32), pltpu.VMEM((2 * TK, 2 * TQ), F32),
                        pltpu.VMEM((2 * TK, 2 * TQ), BF16), pltpu.VMEM((2 * TK, 2 * TQ), BF16),
                        pltpu.VMEM((LANES, 2 * TQ), F32), pltpu.VMEM((1, 2 * TQ), F32),
                        pltpu.VMEM((1, 2 * TQ), F32), pltpu.VMEM((1, 2 * TQ), F32)],
        compiler_params=_cparams(3),
        name=mode + "_attn",
    )(*ins)


def _swa_kernel(nlat, sink_ref, q_ref, k_ref, vt_ref, o_ref):
    g = pl.program_id(1)
    i = pl.program_id(2)
    s_len = nlat * TM
    half = TM // 2
    q = q_ref[0]
    row = lax.broadcasted_iota(jnp.int32, (LANES, 1), 0)
    in_a, in_b = _pair_masks("pair", row)
    cols = []
    for pr in range(2):
        qt = q[:, pr * LANES:(pr + 1) * LANES].T
        zero = jnp.zeros_like(qt)
        cols += [jnp.where(in_a, qt, zero), jnp.where(in_b, qt, zero)]
    q4 = jnp.concatenate(cols, axis=1)

    start0 = pl.multiple_of(jnp.maximum(i * TM - half, 0), half)
    start1 = pl.multiple_of(i * TM, TM)
    start2 = pl.multiple_of(jnp.minimum((i + 1) * TM, s_len + half), half)
    k_cat = jnp.concatenate([k_ref[0, pl.ds(start0, half), :], k_ref[0, pl.ds(start1, TM), :],
                             k_ref[0, pl.ds(start2, half), :], k_ref[0, s_len:s_len + TM, :]], axis=0)
    b0 = jnp.maximum(i - 1, 0)
    b2 = jnp.minimum(i + 1, nlat)
    vt_cat = jnp.concatenate([vt_ref[0, b0][:, half:], vt_ref[0, i], vt_ref[0, b2][:, :half],
                              vt_ref[0, nlat]], axis=1)

    r = lax.broadcasted_iota(jnp.int32, (2 * TM, 1), 0)
    far = -4 * SWA_WINDOW
    lat = i < nlat
    pos0 = jnp.where(lat & (i >= 1), start0 + r, far)
    pos1 = jnp.where(lat, start1 + r - half, far)
    pos2 = jnp.where(lat & (i + 1 < nlat), start2 + r - half - TM, far)
    kpos = jnp.where(r < half, pos0, jnp.where(r < half + TM, pos1, pos2))
    qpos = i * TM + lax.broadcasted_iota(jnp.int32, (1, TM), 1)
    band = jnp.abs(qpos - kpos) <= SWA_WINDOW
    band4 = jnp.concatenate([band] * 4, axis=1)

    s = jnp.dot(k_cat, q4, preferred_element_type=F32)
    s_loc = jnp.where(band4, s[:2 * TM], NEG)
    s_ctx = s[2 * TM:]
    sink = jnp.concatenate([jnp.full((1, TM), sink_ref[4 * g + h], F32) for h in range(4)], axis=1) * LOG2E
    m = jnp.maximum(jnp.maximum(jnp.max(s_loc, axis=0, keepdims=True), jnp.max(s_ctx, axis=0, keepdims=True)), sink)
    p_loc = jnp.exp2(s_loc - m)
    p_ctx = jnp.exp2(s_ctx - m)
    l = jnp.sum(p_loc, axis=0, keepdims=True) + jnp.sum(p_ctx, axis=0, keepdims=True) + jnp.exp2(sink - m)
    p = jnp.concatenate([p_loc, p_ctx], axis=0).astype(BF16)
    o4 = jnp.dot(vt_cat, p, preferred_element_type=F32) * (1.0 / l)
    vrow = lax.broadcasted_iota(jnp.int32, (LANES, 1), 0)
    for pr in range(2):
        oa = o4[:, (2 * pr) * TM:(2 * pr + 1) * TM]
        ob = o4[:, (2 * pr + 1) * TM:(2 * pr + 2) * TM]
        o_ref[0, :, pr * LANES:(pr + 1) * LANES] = jnp.where(vrow < 64, oa, ob).T.astype(BF16)


def _swa_call(sink, q, k, vt):
    bsz, t, qtot = q.shape
    nkv = k.shape[-1] // LANES
    nt = t // TM
    grid_spec = pltpu.PrefetchScalarGridSpec(
        num_scalar_prefetch=1,
        grid=(bsz, nkv, nt),
        in_specs=[pl.BlockSpec((1, TM, 2 * LANES), lambda b, g, i, s: (b, i, g)),
                  pl.BlockSpec((1, t, LANES), lambda b, g, i, s: (b, 0, g)),
                  pl.BlockSpec((1, nt, LANES, TM), lambda b, g, i, s: (b, 0, g, 0))],
        out_specs=pl.BlockSpec((1, TM, 2 * LANES), lambda b, g, i, s: (b, i, g)),
    )
    return pl.pallas_call(
        functools.partial(_swa_kernel, nt - 1),
        grid_spec=grid_spec,
        out_shape=jax.ShapeDtypeStruct((bsz, t, qtot), BF16),
        compiler_params=_cparams(3),
        name="swa_attn",
    )(sink, q, k, vt)


def _out_proj_kernel(n_o, split, nlat, *refs):
    o_refs = refs[:n_o]
    w_refs = refs[n_o:2 * n_o]
    refs = refs[2 * n_o:]
    if split:
        x_in = _first_layer_x(refs[0], refs[1], nlat)
        refs = refs[2:]
    else:
        x_in = refs[0][0]
        refs = refs[1:]
    mod_ref, g_ref, wr_ref, br_ref, xo_ref, h_ref, lg_ref = refs
    acc = jnp.dot(o_refs[0][0], w_refs[0][...], preferred_element_type=F32)
    for n in range(1, n_o):
        acc = acc + jnp.dot(o_refs[n][0], w_refs[n][...], preferred_element_type=F32)
    m = mod_ref[0, 0]
    x = x_in + m[2:3, :] * acc
    xo_ref[0] = x
    h = _rms(x, g_ref[...], NORM_EPS) * (1.0 + m[4:5, :]) + m[3:4, :]
    hi = h.astype(BF16)
    lo = (h - hi.astype(F32)).astype(BF16)
    h_ref[0] = hi
    both = jnp.dot(hi, wr_ref[...], preferred_element_type=F32)
    lg_ref[0] = (both[:, :LANES] + both[:, LANES:]
                 + jnp.dot(lo, wr_ref[:, :LANES], preferred_element_type=F32) + br_ref[...])


def _out_proj_call(os_, ws, x, mod, g, wr, br):
    split = isinstance(x, (tuple, list))
    if split:
        bsz, t, d = x[0].shape[0], x[0].shape[1] + x[1].shape[1], x[0].shape[2]
    else:
        bsz, t, d = x.shape
    nt = t // TM
    nlat = nt - 1
    n_o = len(os_)
    x_ins = list(x) if split else [x]
    x_specs = _split_specs(nlat, d) if split else [_row_spec(d)]
    ins = list(os_) + list(ws) + x_ins + [mod, g.reshape(1, d), wr, br]
    specs = ([_row_spec(o.shape[-1]) for o in os_] + [_full_spec(w.shape) for w in ws]
             + x_specs + [_mod_spec(nlat), _full_spec((1, d)), _full_spec(wr.shape), _full_spec(br.shape)])
    return pl.pallas_call(
        functools.partial(_out_proj_kernel, n_o, split, nlat),
        grid=(bsz, nt),
        in_specs=specs,
        out_specs=[_row_spec(d), _row_spec(d), _row_spec(LANES)],
        out_shape=[jax.ShapeDtypeStruct((bsz, t, d), F32),
                   jax.ShapeDtypeStruct((bsz, t, d), BF16),
                   jax.ShapeDtypeStruct((bsz, t, LANES), F32)],
        compiler_params=_cparams(2),
        name="out_proj",
    )(*ins)


def _moe_kernel(be_ref, nu_ref, xa_ref, xb_ref, w1_ref, w3_ref, w2_ref, o_ref, w1c, w3c, w2c):
    i = pl.program_id(0)
    e = be_ref[i]
    half = pl.num_programs(0) // 2

    @pl.when((i == 0) | (e != be_ref[jnp.maximum(i - 1, 0)]))
    def _():
        w1c[...] = w1_ref[0, 0].astype(BF16)
        w3c[...] = w3_ref[0, 0].astype(BF16)
        w2c[...] = w2_ref[0, 0].astype(BF16)

    @pl.when(i < nu_ref[0])
    def _():
        x = jnp.where(i < half, xa_ref[...], xb_ref[...])
        a = jnp.dot(x, w1c[...], preferred_element_type=F32)
        b = jnp.dot(x, w3c[...], preferred_element_type=F32)
        hmid = (a * jax.nn.sigmoid(a)) * b
        o_ref[...] = jnp.dot(hmid.astype(BF16), w2c[...], preferred_element_type=F32)

    @pl.when(i >= nu_ref[0])
    def _():
        o_ref[...] = jnp.zeros_like(o_ref)


def _moe_call(layer, block_e, n_used, xa, xb, w1, w3, w2):
    d = xa.shape[1]
    half = xa.shape[0] // TMOE
    n_blocks = 2 * half
    n_slots = n_blocks * TMOE
    ff = w1.shape[-1]
    grid_spec = pltpu.PrefetchScalarGridSpec(
        num_scalar_prefetch=2,
        grid=(n_blocks,),
        in_specs=[pl.BlockSpec((TMOE, d), lambda i, be, nu: (jnp.minimum(i, half - 1), 0)),
                  pl.BlockSpec((TMOE, d), lambda i, be, nu: (jnp.maximum(i - half, 0), 0)),
                  pl.BlockSpec((1, 1, d, ff), lambda i, be, nu: (layer, be[i], 0, 0)),
                  pl.BlockSpec((1, 1, d, ff), lambda i, be, nu: (layer, be[i], 0, 0)),
                  pl.BlockSpec((1, 1, ff, d), lambda i, be, nu: (layer, be[i], 0, 0))],
        out_specs=pl.BlockSpec((TMOE, d), lambda i, be, nu: (i, 0)),
        scratch_shapes=[pltpu.VMEM((d, ff), BF16), pltpu.VMEM((d, ff), BF16), pltpu.VMEM((ff, d), BF16)],
    )
    return pl.pallas_call(
        _moe_kernel,
        grid_spec=grid_spec,
        out_shape=jax.ShapeDtypeStruct((n_slots, d), F32),
        compiler_params=_cparams(1),
        name="moe_experts",
    )(block_e, n_used, xa, xb, w1, w3, w2)


def _router_kernel(lg_ref, tri_ref, rt_ref, cnt_ref, ga_ref, gb_ref, run):
    i = pl.program_id(0)

    @pl.when(i == 0)
    def _():
        run[...] = jnp.zeros_like(run)

    lg = lg_ref[...]
    lane = lax.broadcasted_iota(jnp.int32, lg.shape, 1)
    gmask = lane < N_GROUPS
    gl = jnp.where(gmask, lg, NEG)
    gmax = jnp.max(gl, axis=1, keepdims=True)
    grp = jnp.min(jnp.where(gl == gmax, lane, LANES), axis=1, keepdims=True)
    p_grp = 1.0 / jnp.sum(jnp.where(gmask, jnp.exp(lg - gmax), 0.0), axis=1, keepdims=True)
    first = N_GROUPS + EXPERTS_PER_GROUP * grp
    emask = (lane >= first) & (lane < first + EXPERTS_PER_GROUP)
    el = jnp.where(emask, lg, NEG)
    e1 = jnp.max(el, axis=1, keepdims=True)
    i1 = jnp.min(jnp.where(el == e1, lane, LANES), axis=1, keepdims=True)
    el2 = jnp.where(lane == i1, NEG, el)
    e2 = jnp.max(el2, axis=1, keepdims=True)
    i2 = jnp.min(jnp.where(el2 == e2, lane, LANES), axis=1, keepdims=True)
    tt = jnp.exp(e2 - e1)
    g1 = p_grp / (1.0 + tt)
    g2 = g1 * tt
    oh1 = lane == i1
    oh2 = lane == i2
    onehot = jnp.where(oh1 | oh2, 1.0, 0.0)
    rank_all = jnp.dot(tri_ref[...], onehot.astype(BF16), preferred_element_type=F32) + run[...]
    r1 = jnp.sum(jnp.where(oh1, rank_all, 0.0), axis=1, keepdims=True)
    r2 = jnp.sum(jnp.where(oh2, rank_all, 0.0), axis=1, keepdims=True)
    run[...] = run[...] + jnp.sum(onehot, axis=0, keepdims=True)
    cnt_ref[...] = run[...]
    vals = [(i1 - N_GROUPS).astype(F32), (i2 - N_GROUPS).astype(F32), r1, r2, g1, g2]
    packed = jnp.zeros(lg.shape, F32)
    for n, v in enumerate(vals):
        packed = jnp.where(lane == n, v, packed)
    rt_ref[...] = packed.T[0:8, :]
    ga_ref[...] = jnp.broadcast_to(g1, lg.shape)
    gb_ref[...] = jnp.broadcast_to(g2, lg.shape)


def _router_call(logits):
    n_tok = logits.shape[0]
    tri = (jnp.arange(TM)[:, None] > jnp.arange(TM)[None, :]).astype(BF16)
    return pl.pallas_call(
        _router_kernel,
        grid=(n_tok // TM,),
        in_specs=[pl.BlockSpec((TM, LANES), lambda i: (i, 0)),
                  pl.BlockSpec((TM, TM), lambda i: (0, 0))],
        out_specs=[pl.BlockSpec((8, TM), lambda i: (0, i)),
                   pl.BlockSpec((1, LANES), lambda i: (0, 0)),
                   pl.BlockSpec((TM, LANES), lambda i: (i, 0)),
                   pl.BlockSpec((TM, LANES), lambda i: (i, 0))],
        out_shape=[jax.ShapeDtypeStruct((8, n_tok), F32), jax.ShapeDtypeStruct((1, LANES), F32),
                   jax.ShapeDtypeStruct((n_tok, LANES), F32), jax.ShapeDtypeStruct((n_tok, LANES), F32)],
        scratch_shapes=[pltpu.VMEM((1, LANES), F32)],
        compiler_params=_cparams(1),
        name="router",
    )(logits, tri)


def _dispatch_plan(rt, cnt):
    n_tok = rt.shape[1]
    eid = rt[0:2].astype(jnp.int32)
    rank = rt[2:4].astype(jnp.int32)
    counts = cnt[0, N_GROUPS:N_GROUPS + N_EXPERTS].astype(jnp.int32)
    padded = (counts + TMOE - 1) // TMOE * TMOE
    pad_end = jnp.cumsum(padded)
    pad_start = pad_end - padded
    experts = jnp.arange(N_EXPERTS, dtype=jnp.int32)[:, None, None]
    dest = jnp.sum(jnp.where(eid[None] == experts, pad_start[:, None, None], 0), axis=0) + rank
    n_assign = n_tok * TOP_K
    n_blocks = (n_assign + N_EXPERTS * (TMOE - 1) + TMOE - 1) // TMOE
    n_blocks += n_blocks % 2
    n_slots = n_blocks * TMOE
    tok = jnp.arange(n_tok, dtype=jnp.int32)
    slot_tok = jnp.zeros((n_slots,), jnp.int32).at[dest.reshape(n_assign)].set(jnp.concatenate([tok, tok]))
    block_start = jnp.arange(n_blocks, dtype=jnp.int32) * TMOE
    block_e = jnp.minimum(jnp.sum((block_start[:, None] >= pad_end[None, :]).astype(jnp.int32), axis=1),
                          N_EXPERTS - 1).astype(jnp.int32)
    n_used = (pad_end[-1:] // TMOE).astype(jnp.int32)
    return slot_tok, block_e, n_used, dest


def _final_kernel(x_ref, ya_ref, yb_ref, ga_ref, gb_ref, pmod_ref, g_ref, o_ref):
    x = _moe_residual(x_ref, ya_ref, yb_ref, ga_ref, gb_ref, pmod_ref)
    o_ref[0] = _rms(x, g_ref[...], NORM_EPS)


def _final_call(x, res, g, s_len):
    bsz, t, d = x.shape
    ya, yb, ga, gb, pmod = res
    return pl.pallas_call(
        _final_kernel,
        grid=(bsz, s_len // TM),
        in_specs=[_row_spec(d), _row_spec(d), _row_spec(d), _row_spec(LANES), _row_spec(LANES),
                  pl.BlockSpec((1, 1, 8, d), lambda b, i: (b, 0, 0, 0)),
                  _full_spec((1, d))],
        out_specs=_row_spec(d),
        out_shape=jax.ShapeDtypeStruct((bsz, s_len, d), F32),
        compiler_params=_cparams(2),
        name="final_norm",
    )(x, ya, yb, ga, gb, pmod, g.reshape(1, d))


def _take_cols(w, idx):
    wz = jnp.concatenate([w, jnp.zeros((w.shape[0], 1), w.dtype)], axis=1)
    return jnp.take(wz, jnp.asarray(idx, dtype=jnp.int32), axis=1).astype(BF16)


def _ab_layouts():
    zc = 1184
    cols = []
    for j in range(4):
        cols.append(j * 128 + _PAIR_PERM)
    for g in range(2):
        base = 512 + g * 64
        cols.append(base + np.concatenate([np.arange(0, 32), np.arange(0, 32), np.arange(32, 64), np.arange(32, 64)]))
    for g in range(2):
        base = 640 + g * 64
        cols.append(base + np.concatenate([np.arange(64), np.arange(64)]))
    cols.append(768 + np.arange(256))
    cols.append(1024 + np.arange(128))
    kr = 1152
    z32 = np.full((32,), zc)
    cols.append(np.concatenate([kr + np.arange(16), kr + np.arange(16), z32,
                                kr + 16 + np.arange(16), kr + 16 + np.arange(16), z32]))
    w1_idx = np.concatenate(cols)

    zq = 768
    uq = []
    z32q = np.full((32,), zq)
    for j in range(4):
        a, b = 2 * j * 96, (2 * j + 1) * 96
        uq.append(np.concatenate([a + np.arange(64), b + np.arange(64),
                                  a + 64 + np.arange(16), b + 64 + np.arange(16), z32q,
                                  a + 80 + np.arange(16), b + 80 + np.arange(16), z32q]))
    uq_idx = np.concatenate(uq)

    kn, mv = [], []
    for h in range(8):
        kn.append(h * 128 + np.arange(64))
        mv.append(h * 128 + 64 + np.arange(64))
    ukv_idx = np.concatenate(kn + mv)
    return w1_idx, uq_idx, ukv_idx


def _diff_layout():
    cols = []
    for part in range(2):
        for h in range(8):
            cols.append(part * 1024 + h * 128 + _PAIR_PERM)
    cols.append(2048 + np.arange(1024))
    return np.concatenate(cols)


def kernel(x, c, ctx, c_ctx, norm_mix, norm_ffn, ada_w, ada_b, ab_w_in, mla_q_norm, mla_w_uq, mla_kv_norm, mla_w_ukv, swa_sink, ab_w_out, diff_w_in, diff_lambda_q1, diff_lambda_k1, diff_lambda_q2, diff_lambda_k2, diff_subln, diff_w_out, router_group_w, router_group_b, router_expert_w, router_expert_b, expert_w1, expert_w3, expert_w2, final_norm):
    bsz, s_len, d = x.shape
    c_len = ctx.shape[1]
    depth = ada_w.shape[0]
    assert d == D_MODEL and c_len == TM and s_len % TQ == 0 and s_len % (2 * TK) == 0
    t = s_len + c_len

    xs = (x, ctx)
    tables = _rope_tables(s_len, c_len)
    c64, s64, c32, s32 = tables

    n_rows = (bsz + 1 + 7) // 8 * 8
    rows = jnp.concatenate([c, c_ctx[None, :], jnp.zeros((n_rows - bsz - 1, d), F32)], axis=0)
    mod_all = _ada_call(rows, ada_w, ada_b)
    mod_lat = mod_all[:, :bsz].reshape(depth, bsz, 1, 6, d)
    mod_ctx = jnp.broadcast_to(mod_all[:, bsz].reshape(depth, 1, 1, 6, d), (depth, bsz, 1, 6, d))
    mods = jnp.concatenate([mod_lat, mod_ctx], axis=2)
    mods = jnp.concatenate([mods, jnp.zeros((depth, bsz, 2, 2, d), F32)], axis=3)

    w1_idx, uq_idx, ukv_idx = _ab_layouts()
    diff_idx = _diff_layout()

    res = None
    for l in range(depth):
        j = l // 2
        mod = mods[l]
        if l % 2 == 0:
            weights = [_take_cols(ab_w_in[j], w1_idx), mla_q_norm[j].reshape(1, -1), mla_kv_norm[j].reshape(1, -1),
                       _take_cols(mla_w_uq[j], uq_idx), _take_cols(mla_w_ukv[j], ukv_idx)]
            xs, (sq, sk, sv, mq, mk, mv) = _proj_call("ab", xs, res, mod, norm_mix[l], weights,
                                                      [c64, s64, c32, s32], [512, 256, -256, 1024, 1024, -512])
            o_a = _swa_call(swa_sink[j], sq, sk, sv)
            o_b = _dense_attn_call("mla", mq, mk, mv)
            w_out = ab_w_out[j].astype(BF16)
            attn_outs, out_ws = [o_a, o_b], [w_out[:512], w_out[512:]]
        else:
            lambda_init = 0.8 - 0.6 * math.exp(-0.3 * l)
            weights = [_take_cols(diff_w_in[j], diff_idx)]
            xs, (dq, dk, dv) = _proj_call("diff", xs, res, mod, norm_mix[l], weights, [c64, s64], [1024, 1024, -1024])
            extra = [diff_lambda_q1[j].reshape(1, -1), diff_lambda_k1[j].reshape(1, -1),
                     diff_lambda_q2[j].reshape(1, -1), diff_lambda_k2[j].reshape(1, -1),
                     diff_subln[j].reshape(1, -1)]
            o_d = _dense_attn_call("diff", dq, dk, dv, extra=extra, lambda_init=lambda_init)
            attn_outs, out_ws = [o_d], [diff_w_out[j].astype(BF16)]

        wr = jnp.concatenate([router_group_w[l], router_expert_w[l],
                              jnp.zeros((d, LANES - N_GROUPS - N_EXPERTS), F32)], axis=1)
        wr_hi = wr.astype(BF16)
        wr = jnp.concatenate([wr_hi, (wr - wr_hi.astype(F32)).astype(BF16)], axis=1)
        br = jnp.concatenate([router_group_b[l], router_expert_b[l],
                              jnp.zeros((LANES - N_GROUPS - N_EXPERTS,), F32)]).reshape(1, LANES)
        xs, h2, logits = _out_proj_call(attn_outs, out_ws, xs, mod, norm_ffn[l], wr, br)

        n_tok = bsz * t
        rt, cnt, ga, gb = _router_call(logits.reshape(n_tok, LANES))
        slot_tok, block_e, n_used, dest = _dispatch_plan(rt, cnt)
        h2f = h2.reshape(n_tok, d)
        n_half = slot_tok.shape[0] // 2
        xa = jnp.take(h2f, slot_tok[:n_half], axis=0, mode="clip")
        xb = jnp.take(h2f, slot_tok[n_half:], axis=0, mode="clip")
        yb = _moe_call(l, block_e, n_used, xa, xb, expert_w1, expert_w3, expert_w2)
        res = (jnp.take(yb, dest[0], axis=0, mode="clip").reshape(bsz, t, d),
               jnp.take(yb, dest[1], axis=0, mode="clip").reshape(bsz, t, d),
               ga.reshape(bsz, t, LANES), gb.reshape(bsz, t, LANES), mod)

    return _final_call(xs, res, final_norm, s_len)
```

```python
import functools
import math

import numpy as np
import jax
import jax.numpy as jnp
from jax import lax
from jax.experimental import pallas as pl
from jax.experimental.pallas import tpu as pltpu

F32 = jnp.float32
BF16 = jnp.bfloat16

D_MODEL = 1024
GRID_W = 64
ROPE_BASE = 10000.0
NORM_EPS = 1e-6
DIFF_EPS = 1e-5
NEG = -1e30

SWA_WINDOW = 128
MLA_SCALE = (64 + 32) ** -0.5
HEAD_SCALE = 64 ** -0.5
LOG2E = math.log2(math.e)

N_GROUPS = 4
EXPERTS_PER_GROUP = 8
N_EXPERTS = 32
TOP_K = 2
EXPERT_FF = 512

LANES = 128
TM = 256
TK = 256
TQ = 1024
TMOE = 256
VMEM_LIMIT = 56 * 1024 * 1024


def _cparams(n_axes):
    return pltpu.CompilerParams(dimension_semantics=("arbitrary",) * n_axes,
                                vmem_limit_bytes=VMEM_LIMIT)


def _rms(x, g, eps):
    return x * lax.rsqrt(jnp.mean(x * x, axis=-1, keepdims=True) + eps) * g


def _rope_block(x, c, s):
    return x * c + pltpu.roll(x, 64, 1) * s


_PAIR_PERM = np.concatenate([np.arange(0, 32), np.arange(64, 96), np.arange(32, 64), np.arange(96, 128)])


def _rope_tables(s_len, c_len):
    rows = s_len // GRID_W
    row = jnp.repeat(jnp.arange(rows, dtype=F32), GRID_W)
    col = jnp.tile(jnp.arange(GRID_W, dtype=F32), rows)

    def tab(dim):
        nf = dim // 4
        inv = ROPE_BASE ** (-jnp.arange(nf, dtype=F32) / nf)
        ang = jnp.concatenate([row[:, None] * inv, col[:, None] * inv], axis=-1)
        return jnp.cos(ang), jnp.sin(ang)

    cos64, sin64 = tab(64)
    cos32, sin32 = tab(32)
    c64 = jnp.concatenate([cos64] * 4, axis=-1)
    s64 = jnp.concatenate([-sin64, -sin64, sin64, sin64], axis=-1)
    one = jnp.ones((s_len, 32), F32)
    zero = jnp.zeros((s_len, 32), F32)
    c32 = jnp.concatenate([cos32, cos32, one, cos32, cos32, one], axis=-1)
    s32 = jnp.concatenate([-sin32, -sin32, zero, sin32, sin32, zero], axis=-1)

    def ext(t, fill):
        return jnp.concatenate([t, jnp.full((c_len, LANES), fill, F32)], axis=0)

    return ext(c64, 1.0), ext(s64, 0.0), ext(c32, 1.0), ext(s32, 0.0)


def _ada_kernel(x_ref, w_ref, b_ref, o_ref):
    x = x_ref[...]
    sx = x * jax.nn.sigmoid(x)
    o_ref[0] = jnp.dot(sx.astype(BF16), w_ref[0].astype(BF16), preferred_element_type=F32) + b_ref[0]


def _ada_call(rows, ada_w, ada_b):
    depth, d, n6 = ada_w.shape
    r = rows.shape[0]
    tn = 1536
    return pl.pallas_call(
        _ada_kernel,
        grid=(depth, n6 // tn),
        in_specs=[pl.BlockSpec((r, d), lambda l, j: (0, 0)),
                  pl.BlockSpec((1, d, tn), lambda l, j: (l, 0, j)),
                  pl.BlockSpec((1, 1, tn), lambda l, j: (l, 0, j))],
        out_specs=pl.BlockSpec((1, r, tn), lambda l, j: (l, 0, j)),
        out_shape=jax.ShapeDtypeStruct((depth, r, n6), F32),
        compiler_params=_cparams(2),
        name="ada_mod",
    )(rows, ada_w, ada_b.reshape(depth, 1, n6))


def _first_layer_x(x_ref, ctx_ref, nlat):
    return jnp.where(pl.program_id(1) < nlat, x_ref[0], ctx_ref[0])


def _moe_residual(x_ref, ya_ref, yb_ref, ga_ref, gb_ref, pmod_ref):
    reps = x_ref.shape[-1] // LANES
    ga = jnp.concatenate([ga_ref[0]] * reps, axis=1)
    gb = jnp.concatenate([gb_ref[0]] * reps, axis=1)
    y = ga * ya_ref[0] + gb * yb_ref[0]
    return x_ref[0] + pmod_ref[0, 0][5:6, :] * y


def _prenorm(has_res, nlat, x_refs, mod_ref, g_ref, xo_ref):
    if has_res:
        x = _moe_residual(*x_refs)
        xo_ref[0] = x
    else:
        x = _first_layer_x(x_refs[0], x_refs[1], nlat)
    m = mod_ref[0, 0]
    return _rms(x, g_ref[...], NORM_EPS) * (1.0 + m[1:2, :]) + m[0:1, :]


def _ab_proj_kernel(has_res, nlat, *refs):
    n_x = 6 if has_res else 2
    x_refs, refs = refs[:n_x], refs[n_x:]
    (mod_ref, g_ref, w1_ref, qn_ref, kvn_ref, wuq_ref, wukv_ref,
     c64_ref, s64_ref, c32_ref, s32_ref) = refs[:11]
    outs = refs[11:]
    if has_res:
        xo_ref, outs = outs[0], outs[1:]
    else:
        xo_ref = None
    sq_ref, sk_ref, sv_ref, mq_ref, mk_ref, mv_ref = outs

    h = _prenorm(has_res, nlat, x_refs, mod_ref, g_ref, xo_ref)
    p = jnp.dot(h.astype(BF16), w1_ref[...], preferred_element_type=F32)
    c64, s64, c32, s32 = c64_ref[...], s64_ref[...], c32_ref[...], s32_ref[...]
    for j in range(4):
        blk = _rope_block(p[:, j * 128:(j + 1) * 128], c64, s64)
        sq_ref[0, :, j * 128:(j + 1) * 128] = (blk * (HEAD_SCALE * LOG2E)).astype(BF16)
    for j in range(2):
        blk = _rope_block(p[:, 512 + j * 128:512 + (j + 1) * 128], c64, s64)
        sk_ref[0, :, j * 128:(j + 1) * 128] = blk.astype(BF16)
    sv_ref[0, 0] = p[:, 768:1024].T.astype(BF16)
    cq = p[:, 1024:1280]
    ckv = p[:, 1280:1408]
    kr = _rope_block(p[:, 1408:1536], c32, s32).astype(BF16)
    qm = jnp.dot(_rms(cq, qn_ref[...], NORM_EPS).astype(BF16), wuq_ref[...], preferred_element_type=F32)
    kv = jnp.dot(_rms(ckv, kvn_ref[...], NORM_EPS).astype(BF16), wukv_ref[...], preferred_element_type=F32)
    for j in range(4):
        mq_ref[0, :, j * 256:j * 256 + 128] = (qm[:, j * 256:j * 256 + 128] * (MLA_SCALE * LOG2E)).astype(BF16)
        rr = _rope_block(qm[:, j * 256 + 128:(j + 1) * 256], c32, s32)
        mq_ref[0, :, j * 256 + 128:(j + 1) * 256] = (rr * (MLA_SCALE * LOG2E)).astype(BF16)
        mk_ref[0, :, j * 256:j * 256 + 128] = kv[:, j * 128:(j + 1) * 128].astype(BF16)
        mk_ref[0, :, j * 256 + 128:(j + 1) * 256] = kr
    mv_ref[0, 0] = kv[:, 512:1024].T.astype(BF16)


def _diff_proj_kernel(has_res, nlat, *refs):
    n_x = 6 if has_res else 2
    x_refs, refs = refs[:n_x], refs[n_x:]
    mod_ref, g_ref, w_ref, c64_ref, s64_ref = refs[:5]
    outs = refs[5:]
    if has_res:
        xo_ref, outs = outs[0], outs[1:]
    else:
        xo_ref = None
    q_ref, k_ref, v_ref = outs
    h = _prenorm(has_res, nlat, x_refs, mod_ref, g_ref, xo_ref)
    p = jnp.dot(h.astype(BF16), w_ref[...], preferred_element_type=F32)
    c64, s64 = c64_ref[...], s64_ref[...]
    for j in range(8):
        blk = _rope_block(p[:, j * 128:(j + 1) * 128], c64, s64)
        q_ref[0, :, j * 128:(j + 1) * 128] = (blk * (HEAD_SCALE * LOG2E)).astype(BF16)
        blk = _rope_block(p[:, 1024 + j * 128:1024 + (j + 1) * 128], c64, s64)
        k_ref[0, :, j * 128:(j + 1) * 128] = blk.astype(BF16)
    v_ref[0, 0] = p[:, 2048:3072].T.astype(BF16)


def _row_spec(width):
    return pl.BlockSpec((1, TM, width), lambda b, i: (b, i, 0))


def _mod_spec(nlat):
    return pl.BlockSpec((1, 1, 8, D_MODEL), lambda b, i: (b, i // nlat, 0, 0))


def _full_spec(shape):
    nd = len(shape)
    return pl.BlockSpec(shape, lambda b, i: (0,) * nd)


def _tab_spec():
    return pl.BlockSpec((TM, LANES), lambda b, i: (i, 0))


def _split_specs(nlat, d):
    return [pl.BlockSpec((1, TM, d), lambda b, i: (b, jnp.minimum(i, nlat - 1), 0)),
            pl.BlockSpec((1, TM, d), lambda b, i: (b, 0, 0))]


def _proj_call(kind, x, res, mod, g, weights, tables, out_widths):
    has_res = res is not None
    if has_res:
        bsz, t, d = x.shape
    else:
        bsz, t, d = x[0].shape[0], x[0].shape[1] + x[1].shape[1], x[0].shape[2]
    nt = t // TM
    nlat = nt - 1
    if has_res:
        ya, yb, ga, gb, pmod = res
        ins = [x, ya, yb, ga, gb, pmod]
        specs = [_row_spec(d), _row_spec(d), _row_spec(d), _row_spec(LANES), _row_spec(LANES), _mod_spec(nlat)]
    else:
        ins = list(x)
        specs = _split_specs(nlat, d)
    ins += [mod, g.reshape(1, d)]
    specs += [_mod_spec(nlat), _full_spec((1, d))]
    for w in weights:
        ins.append(w)
        specs.append(_full_spec(w.shape))
    for tb in tables:
        ins.append(tb)
        specs.append(_tab_spec())
    out_shapes, out_specs = [], []
    if has_res:
        out_shapes.append(jax.ShapeDtypeStruct((bsz, t, d), F32))
        out_specs.append(_row_spec(d))
    for w in out_widths:
        if w < 0:
            out_shapes.append(jax.ShapeDtypeStruct((bsz, nt, -w, TM), BF16))
            out_specs.append(pl.BlockSpec((1, 1, -w, TM), lambda b, i: (b, i, 0, 0)))
        else:
            out_shapes.append(jax.ShapeDtypeStruct((bsz, t, w), BF16))
            out_specs.append(_row_spec(w))
    body = _ab_proj_kernel if kind == "ab" else _diff_proj_kernel
    outs = pl.pallas_call(
        functools.partial(body, has_res, nlat),
        grid=(bsz, nt),
        in_specs=specs,
        out_specs=out_specs,
        out_shape=out_shapes,
        compiler_params=_cparams(2),
        name=kind + "_proj",
    )(*ins)
    if has_res:
        return outs[0], outs[1:]
    return x, outs


def _pair_masks(mode, lane):
    if mode == "mla":
        in_a = (lane < 64) | ((lane >= 128) & (lane < 144)) | ((lane >= 192) & (lane < 208))
        in_b = ((lane >= 64) & (lane < 128)) | ((lane >= 144) & (lane < 160)) | ((lane >= 208) & (lane < 224))
    else:
        in_a = (lane < 32) | ((lane >= 64) & (lane < 96))
        in_b = ((lane >= 32) & (lane < 64)) | (lane >= 96)
    return in_a, in_b


def _dense_attn_kernel(mode, lambda_init, nlat, *refs):
    n_in = 8 if mode == "diff" else 3
    q_ref, k_ref, vt_ref = refs[:3]
    o_ref = refs[n_in]
    s_bufs = refs[n_in + 1:n_in + 3]
    p_bufs = refs[n_in + 3:n_in + 5]
    acc_buf, m_buf, l_buf, a_buf = refs[n_in + 5:]
    i = pl.program_id(2)
    qt = q_ref[0].T
    width, tq = qt.shape
    row = lax.broadcasted_iota(jnp.int32, (width, 1), 0)
    in_a, in_b = _pair_masks(mode, row)
    zero = jnp.zeros_like(qt)
    q2 = jnp.concatenate([jnp.where(in_a, qt, zero), jnp.where(in_b, qt, zero)], axis=1)
    chunks = [(nlat, 1)] + [(2 * c, 2) for c in range(nlat // 2)]
    n_lat_q = (nlat * TK) // tq

    def scores(chunk, s_buf):
        k0, n = chunk
        s_buf[0:n * TK, :] = jnp.dot(k_ref[0, k0 * TK:(k0 + n) * TK, :], q2, preferred_element_type=F32)

    def pv(chunk, p_buf):
        k0, n = chunk
        out = jnp.dot(vt_ref[0, k0], p_buf[0:TK, :], preferred_element_type=F32)
        for r in range(1, n):
            out = out + jnp.dot(vt_ref[0, k0 + r], p_buf[r * TK:(r + 1) * TK, :], preferred_element_type=F32)
        return out

    def softmax(chunk, s_buf, p_buf, first):
        n = chunk[1]
        s = s_buf[0:n * TK, :]
        mx = jnp.max(s, axis=0, keepdims=True)
        if first:
            mn = mx
        else:
            m = m_buf[...]
            mn = jnp.maximum(m, mx)
            a_buf[...] = jnp.exp2(m - mn)
        p = jnp.exp2(s - mn)
        ps = jnp.sum(p, axis=0, keepdims=True)
        l_buf[...] = ps if first else a_buf[...] * l_buf[...] + ps
        m_buf[...] = mn
        p_buf[0:n * TK, :] = p.astype(BF16)

    def accumulate(chunk, p_buf, first):
        if first:
            acc_buf[...] = pv(chunk, p_buf)
        else:
            acc_buf[...] = a_buf[...] * acc_buf[...] + pv(chunk, p_buf)

    def pipeline(chs):
        scores(chs[0], s_bufs[0])
        for c, ch in enumerate(chs):
            if c >= 2:
                accumulate(chs[c - 1], p_bufs[(c - 1) % 2], first=False)
            if c + 1 < len(chs):
                scores(chs[c + 1], s_bufs[(c + 1) % 2])
            softmax(ch, s_bufs[c % 2], p_bufs[c % 2], first=c == 0)
            if c == 1:
                accumulate(chs[0], p_bufs[0], first=True)
        last = len(chs) - 1
        if last == 0:
            accumulate(chs[0], p_bufs[0], first=True)
        else:
            accumulate(chs[last], p_bufs[last % 2], first=False)

    @pl.when(i < n_lat_q)
    def _():
        pipeline(chunks)

    @pl.when(i >= n_lat_q)
    def _():
        pipeline(chunks[:1])

    o2 = acc_buf[...] * (1.0 / l_buf[...])
    oa, ob = o2[:, :tq], o2[:, tq:]
    if mode == "diff":
        lq1_ref, lk1_ref, lq2_ref, lk2_ref, sub_ref = refs[3:8]
        lam = (jnp.exp(jnp.sum(lq1_ref[...] * lk1_ref[...], axis=1, keepdims=True))
               - jnp.exp(jnp.sum(lq2_ref[...] * lk2_ref[...], axis=1, keepdims=True)) + lambda_init)
        o = (oa - lam * ob).T
        o = _rms(o, sub_ref[...], DIFF_EPS) * (1.0 - lambda_init)
    else:
        vrow = lax.broadcasted_iota(jnp.int32, (LANES, 1), 0)
        o = jnp.where(vrow < 64, oa, ob).T
    o_ref[0] = o.astype(BF16)


def _dense_attn_call(mode, q, k, vt, extra=(), lambda_init=0.0):
    bsz, t, qtot = q.shape
    width = 256 if mode == "mla" else 128
    npairs = qtot // width
    nq = pl.cdiv(t, TQ)
    ins = [q, k, vt]
    specs = [pl.BlockSpec((1, TQ, width), lambda b, j, i: (b, i, j)),
             pl.BlockSpec((1, t, width), lambda b, j, i: (b, 0, j)),
             pl.BlockSpec((1, t // TK, LANES, TK), lambda b, j, i: (b, 0, j, 0))]
    for e in extra:
        ins.append(e)
        specs.append(pl.BlockSpec(e.shape, lambda b, j, i: (0, 0)))
    return pl.pallas_call(
        functools.partial(_dense_attn_kernel, mode, lambda_init, t // TK - 1),
        grid=(bsz, npairs, nq),
        in_specs=specs,
        out_specs=pl.BlockSpec((1, TQ, LANES), lambda b, j, i: (b, i, j)),
        out_shape=jax.ShapeDtypeStruct((bsz, t, npairs * LANES), BF16),
        scratch_shapes=[pltpu.VMEM((2 * TK, 2 * TQ), F32), pltpu.VMEM((2 * TK, 2 * TQ), F32),
                        pltpu.VMEM((2 * TK, 2 * TQ), BF16), pltpu.VMEM((2 * TK, 2 * TQ), BF16),
                        pltpu.VMEM((LANES, 2 * TQ), F32), pltpu.VMEM((1, 2 * TQ), F32),
                        pltpu.VMEM((1, 2 * TQ), F32), pltpu.VMEM((1, 2 * TQ), F32)],
        compiler_params=_cparams(3),
        name=mode + "_attn",
    )(*ins)


def _swa_kernel(nlat, sink_ref, q_ref, k_ref, vt_ref, o_ref):
    g = pl.program_id(1)
    i = pl.program_id(2)
    s_len = nlat * TM
    half = TM // 2
    q = q_ref[0]
    row = lax.broadcasted_iota(jnp.int32, (LANES, 1), 0)
    in_a, in_b = _pair_masks("pair", row)
    cols = []
    for pr in range(2):
        qt = q[:, pr * LANES:(pr + 1) * LANES].T
        zero = jnp.zeros_like(qt)
        cols += [jnp.where(in_a, qt, zero), jnp.where(in_b, qt, zero)]
    q4 = jnp.concatenate(cols, axis=1)

    start0 = pl.multiple_of(jnp.maximum(i * TM - half, 0), half)
    start1 = pl.multiple_of(i * TM, TM)
    start2 = pl.multiple_of(jnp.minimum((i + 1) * TM, s_len + half), half)
    k_cat = jnp.concatenate([k_ref[0, pl.ds(start0, half), :], k_ref[0, pl.ds(start1, TM), :],
                             k_ref[0, pl.ds(start2, half), :], k_ref[0, s_len:s_len + TM, :]], axis=0)
    b0 = jnp.maximum(i - 1, 0)
    b2 = jnp.minimum(i + 1, nlat)
    vt_cat = jnp.concatenate([vt_ref[0, b0][:, half:], vt_ref[0, i], vt_ref[0, b2][:, :half],
                              vt_ref[0, nlat]], axis=1)

    r = lax.broadcasted_iota(jnp.int32, (2 * TM, 1), 0)
    far = -4 * SWA_WINDOW
    lat = i < nlat
    pos0 = jnp.where(lat & (i >= 1), start0 + r, far)
    pos1 = jnp.where(lat, start1 + r - half, far)
    pos2 = jnp.where(lat & (i + 1 < nlat), start2 + r - half - TM, far)
    kpos = jnp.where(r < half, pos0, jnp.where(r < half + TM, pos1, pos2))
    qpos = i * TM + lax.broadcasted_iota(jnp.int32, (1, TM), 1)
    band = jnp.abs(qpos - kpos) <= SWA_WINDOW
    band4 = jnp.concatenate([band] * 4, axis=1)

    s = jnp.dot(k_cat, q4, preferred_element_type=F32)
    s_loc = jnp.where(band4, s[:2 * TM], NEG)
    s_ctx = s[2 * TM:]
    sink = jnp.concatenate([jnp.full((1, TM), sink_ref[4 * g + h], F32) for h in range(4)], axis=1) * LOG2E
    m = jnp.maximum(jnp.maximum(jnp.max(s_loc, axis=0, keepdims=True), jnp.max(s_ctx, axis=0, keepdims=True)), sink)
    p_loc = jnp.exp2(s_loc - m)
    p_ctx = jnp.exp2(s_ctx - m)
    l = jnp.sum(p_loc, axis=0, keepdims=True) + jnp.sum(p_ctx, axis=0, keepdims=True) + jnp.exp2(sink - m)
    p = jnp.concatenate([p_loc, p_ctx], axis=0).astype(BF16)
    o4 = jnp.dot(vt_cat, p, preferred_element_type=F32) * (1.0 / l)
    vrow = lax.broadcasted_iota(jnp.int32, (LANES, 1), 0)
    for pr in range(2):
        oa = o4[:, (2 * pr) * TM:(2 * pr + 1) * TM]
        ob = o4[:, (2 * pr + 1) * TM:(2 * pr + 2) * TM]
        o_ref[0, :, pr * LANES:(pr + 1) * LANES] = jnp.where(vrow < 64, oa, ob).T.astype(BF16)


def _swa_call(sink, q, k, vt):
    bsz, t, qtot = q.shape
    nkv = k.shape[-1] // LANES
    nt = t // TM
    grid_spec = pltpu.PrefetchScalarGridSpec(
        num_scalar_prefetch=1,
        grid=(bsz, nkv, nt),
        in_specs=[pl.BlockSpec((1, TM, 2 * LANES), lambda b, g, i, s: (b, i, g)),
                  pl.BlockSpec((1, t, LANES), lambda b, g, i, s: (b, 0, g)),
                  pl.BlockSpec((1, nt, LANES, TM), lambda b, g, i, s: (b, 0, g, 0))],
        out_specs=pl.BlockSpec((1, TM, 2 * LANES), lambda b, g, i, s: (b, i, g)),
    )
    return pl.pallas_call(
        functools.partial(_swa_kernel, nt - 1),
        grid_spec=grid_spec,
        out_shape=jax.ShapeDtypeStruct((bsz, t, qtot), BF16),
        compiler_params=_cparams(3),
        name="swa_attn",
    )(sink, q, k, vt)


def _out_proj_kernel(n_o, split, nlat, *refs):
    o_refs = refs[:n_o]
    w_refs = refs[n_o:2 * n_o]
    refs = refs[2 * n_o:]
    if split:
        x_in = _first_layer_x(refs[0], refs[1], nlat)
        refs = refs[2:]
    else:
        x_in = refs[0][0]
        refs = refs[1:]
    mod_ref, g_ref, wr_ref, br_ref, xo_ref, h_ref, lg_ref = refs
    acc = jnp.dot(o_refs[0][0], w_refs[0][...], preferred_element_type=F32)
    for n in range(1, n_o):
        acc = acc + jnp.dot(o_refs[n][0], w_refs[n][...], preferred_element_type=F32)
    m = mod_ref[0, 0]
    x = x_in + m[2:3, :] * acc
    xo_ref[0] = x
    h = _rms(x, g_ref[...], NORM_EPS) * (1.0 + m[4:5, :]) + m[3:4, :]
    hi = h.astype(BF16)
    lo = (h - hi.astype(F32)).astype(BF16)
    h_ref[0] = hi
    both = jnp.dot(hi, wr_ref[...], preferred_element_type=F32)
    lg_ref[0] = (both[:, :LANES] + both[:, LANES:]
                 + jnp.dot(lo, wr_ref[:, :LANES], preferred_element_type=F32) + br_ref[...])


def _out_proj_call(os_, ws, x, mod, g, wr, br):
    split = isinstance(x, (tuple, list))
    if split:
        bsz, t, d = x[0].shape[0], x[0].shape[1] + x[1].shape[1], x[0].shape[2]
    else:
        bsz, t, d = x.shape
    nt = t // TM
    nlat = nt - 1
    n_o = len(os_)
    x_ins = list(x) if split else [x]
    x_specs = _split_specs(nlat, d) if split else [_row_spec(d)]
    ins = list(os_) + list(ws) + x_ins + [mod, g.reshape(1, d), wr, br]
    specs = ([_row_spec(o.shape[-1]) for o in os_] + [_full_spec(w.shape) for w in ws]
             + x_specs + [_mod_spec(nlat), _full_spec((1, d)), _full_spec(wr.shape), _full_spec(br.shape)])
    return pl.pallas_call(
        functools.partial(_out_proj_kernel, n_o, split, nlat),
        grid=(bsz, nt),
        in_specs=specs,
        out_specs=[_row_spec(d), _row_spec(d), _row_spec(LANES)],
        out_shape=[jax.ShapeDtypeStruct((bsz, t, d), F32),
                   jax.ShapeDtypeStruct((bsz, t, d), BF16),
                   jax.ShapeDtypeStruct((bsz, t, LANES), F32)],
        compiler_params=_cparams(2),
        name="out_proj",
    )(*ins)


def _moe_kernel(first, be_ref, nu_ref, x_ref, w1_ref, w3_ref, w2_ref, *rest):
    o_ref, w1c, w3c, w2c = rest[-4:]
    i = pl.program_id(0)
    gi = first + i
    e = be_ref[gi]

    @pl.when((i == 0) | (e != be_ref[jnp.maximum(gi - 1, 0)]))
    def _():
        w1c[...] = w1_ref[0, 0].astype(BF16)
        w3c[...] = w3_ref[0, 0].astype(BF16)
        w2c[...] = w2_ref[0, 0].astype(BF16)

    @pl.when(gi < nu_ref[0])
    def _():
        x = x_ref[...]
        a = jnp.dot(x, w1c[...], preferred_element_type=F32)
        b = jnp.dot(x, w3c[...], preferred_element_type=F32)
        hmid = (a * jax.nn.sigmoid(a)) * b
        o_ref[...] = jnp.dot(hmid.astype(BF16), w2c[...], preferred_element_type=F32)

    @pl.when(gi >= nu_ref[0])
    def _():
        o_ref[...] = jnp.zeros_like(o_ref)


def _moe_call(layer, block_e, n_used, x_part, first, n_blocks, w1, w3, w2, y_prev=None):
    d = x_part.shape[1]
    steps = x_part.shape[0] // TMOE
    ff = w1.shape[-1]
    in_specs = [pl.BlockSpec((TMOE, d), lambda i, be, nu: (i, 0)),
                pl.BlockSpec((1, 1, d, ff), lambda i, be, nu: (layer, be[first + i], 0, 0)),
                pl.BlockSpec((1, 1, d, ff), lambda i, be, nu: (layer, be[first + i], 0, 0)),
                pl.BlockSpec((1, 1, ff, d), lambda i, be, nu: (layer, be[first + i], 0, 0))]
    ins = [block_e, n_used, x_part, w1, w3, w2]
    aliases = {}
    if y_prev is not None:
        in_specs.append(pl.BlockSpec(memory_space=pl.ANY))
        aliases = {len(ins): 0}
        ins.append(y_prev)
    grid_spec = pltpu.PrefetchScalarGridSpec(
        num_scalar_prefetch=2,
        grid=(steps,),
        in_specs=in_specs,
        out_specs=pl.BlockSpec((TMOE, d), lambda i, be, nu: (first + i, 0)),
        scratch_shapes=[pltpu.VMEM((d, ff), BF16), pltpu.VMEM((d, ff), BF16), pltpu.VMEM((ff, d), BF16)],
    )
    return pl.pallas_call(
        functools.partial(_moe_kernel, first),
        grid_spec=grid_spec,
        out_shape=jax.ShapeDtypeStruct((n_blocks * TMOE, d), F32),
        input_output_aliases=aliases,
        compiler_params=_cparams(1),
        name="moe_experts",
    )(*ins)


def _router_kernel(lg_ref, tri_ref, rt_ref, cnt_ref, ga_ref, gb_ref, run):
    i = pl.program_id(0)

    @pl.when(i == 0)
    def _():
        run[...] = jnp.zeros_like(run)

    lg = lg_ref[...]
    lane = lax.broadcasted_iota(jnp.int32, lg.shape, 1)
    gmask = lane < N_GROUPS
    gl = jnp.where(gmask, lg, NEG)
    gmax = jnp.max(gl, axis=1, keepdims=True)
    grp = jnp.min(jnp.where(gl == gmax, lane, LANES), axis=1, keepdims=True)
    p_grp = 1.0 / jnp.sum(jnp.where(gmask, jnp.exp(lg - gmax), 0.0), axis=1, keepdims=True)
    first = N_GROUPS + EXPERTS_PER_GROUP * grp
    emask = (lane >= first) & (lane < first + EXPERTS_PER_GROUP)
    el = jnp.where(emask, lg, NEG)
    e1 = jnp.max(el, axis=1, keepdims=True)
    i1 = jnp.min(jnp.where(el == e1, lane, LANES), axis=1, keepdims=True)
    el2 = jnp.where(lane == i1, NEG, el)
    e2 = jnp.max(el2, axis=1, keepdims=True)
    i2 = jnp.min(jnp.where(el2 == e2, lane, LANES), axis=1, keepdims=True)
    tt = jnp.exp(e2 - e1)
    g1 = p_grp / (1.0 + tt)
    g2 = g1 * tt
    oh1 = lane == i1
    oh2 = lane == i2
    onehot = jnp.where(oh1 | oh2, 1.0, 0.0)
    rank_all = jnp.dot(tri_ref[...], onehot.astype(BF16), preferred_element_type=F32) + run[...]
    r1 = jnp.sum(jnp.where(oh1, rank_all, 0.0), axis=1, keepdims=True)
    r2 = jnp.sum(jnp.where(oh2, rank_all, 0.0), axis=1, keepdims=True)
    run[...] = run[...] + jnp.sum(onehot, axis=0, keepdims=True)
    cnt_ref[...] = run[...]
    vals = [(i1 - N_GROUPS).astype(F32), (i2 - N_GROUPS).astype(F32), r1, r2, g1, g2]
    packed = jnp.zeros(lg.shape, F32)
    for n, v in enumerate(vals):
        packed = jnp.where(lane == n, v, packed)
    rt_ref[...] = packed.T[0:8, :]
    ga_ref[...] = jnp.broadcast_to(g1, lg.shape)
    gb_ref[...] = jnp.broadcast_to(g2, lg.shape)


def _router_call(logits):
    n_tok = logits.shape[0]
    tri = (jnp.arange(TM)[:, None] > jnp.arange(TM)[None, :]).astype(BF16)
    return pl.pallas_call(
        _router_kernel,
        grid=(n_tok // TM,),
        in_specs=[pl.BlockSpec((TM, LANES), lambda i: (i, 0)),
                  pl.BlockSpec((TM, TM), lambda i: (0, 0))],
        out_specs=[pl.BlockSpec((8, TM), lambda i: (0, i)),
                   pl.BlockSpec((1, LANES), lambda i: (0, 0)),
                   pl.BlockSpec((TM, LANES), lambda i: (i, 0)),
                   pl.BlockSpec((TM, LANES), lambda i: (i, 0))],
        out_shape=[jax.ShapeDtypeStruct((8, n_tok), F32), jax.ShapeDtypeStruct((1, LANES), F32),
                   jax.ShapeDtypeStruct((n_tok, LANES), F32), jax.ShapeDtypeStruct((n_tok, LANES), F32)],
        scratch_shapes=[pltpu.VMEM((1, LANES), F32)],
        compiler_params=_cparams(1),
        name="router",
    )(logits, tri)


def _dispatch_plan(rt, cnt):
    n_tok = rt.shape[1]
    eid = rt[0:2].astype(jnp.int32)
    rank = rt[2:4].astype(jnp.int32)
    counts = cnt[0, N_GROUPS:N_GROUPS + N_EXPERTS].astype(jnp.int32)
    padded = (counts + TMOE - 1) // TMOE * TMOE
    pad_end = jnp.cumsum(padded)
    pad_start = pad_end - padded
    experts = jnp.arange(N_EXPERTS, dtype=jnp.int32)[:, None, None]
    dest = jnp.sum(jnp.where(eid[None] == experts, pad_start[:, None, None], 0), axis=0) + rank
    n_assign = n_tok * TOP_K
    n_blocks = (n_assign + N_EXPERTS * (TMOE - 1) + TMOE - 1) // TMOE
    n_blocks += n_blocks % 2
    n_slots = n_blocks * TMOE
    tok = jnp.arange(n_tok, dtype=jnp.int32)
    slot_tok = jnp.zeros((n_slots,), jnp.int32).at[dest.reshape(n_assign)].set(jnp.concatenate([tok, tok]))
    block_start = jnp.arange(n_blocks, dtype=jnp.int32) * TMOE
    block_e = jnp.minimum(jnp.sum((block_start[:, None] >= pad_end[None, :]).astype(jnp.int32), axis=1),
                          N_EXPERTS - 1).astype(jnp.int32)
    n_used = (pad_end[-1:] // TMOE).astype(jnp.int32)
    return slot_tok, block_e, n_used, dest


def _final_kernel(x_ref, ya_ref, yb_ref, ga_ref, gb_ref, pmod_ref, g_ref, o_ref):
    x = _moe_residual(x_ref, ya_ref, yb_ref, ga_ref, gb_ref, pmod_ref)
    o_ref[0] = _rms(x, g_ref[...], NORM_EPS)


def _final_call(x, res, g, s_len):
    bsz, t, d = x.shape
    ya, yb, ga, gb, pmod = res
    return pl.pallas_call(
        _final_kernel,
        grid=(bsz, s_len // TM),
        in_specs=[_row_spec(d), _row_spec(d), _row_spec(d), _row_spec(LANES), _row_spec(LANES),
                  pl.BlockSpec((1, 1, 8, d), lambda b, i: (b, 0, 0, 0)),
                  _full_spec((1, d))],
        out_specs=_row_spec(d),
        out_shape=jax.ShapeDtypeStruct((bsz, s_len, d), F32),
        compiler_params=_cparams(2),
        name="final_norm",
    )(x, ya, yb, ga, gb, pmod, g.reshape(1, d))


def _take_cols(w, idx):
    wz = jnp.concatenate([w, jnp.zeros((w.shape[0], 1), w.dtype)], axis=1)
    return jnp.take(wz, jnp.asarray(idx, dtype=jnp.int32), axis=1).astype(BF16)


def _ab_layouts():
    zc = 1184
    cols = []
    for j in range(4):
        cols.append(j * 128 + _PAIR_PERM)
    for g in range(2):
        base = 512 + g * 64
        cols.append(base + np.concatenate([np.arange(0, 32), np.arange(0, 32), np.arange(32, 64), np.arange(32, 64)]))
    for g in range(2):
        base = 640 + g * 64
        cols.append(base + np.concatenate([np.arange(64), np.arange(64)]))
    cols.append(768 + np.arange(256))
    cols.append(1024 + np.arange(128))
    kr = 1152
    z32 = np.full((32,), zc)
    cols.append(np.concatenate([kr + np.arange(16), kr + np.arange(16), z32,
                                kr + 16 + np.arange(16), kr + 16 + np.arange(16), z32]))
    w1_idx = np.concatenate(cols)

    zq = 768
    uq = []
    z32q = np.full((32,), zq)
    for j in range(4):
        a, b = 2 * j * 96, (2 * j + 1) * 96
        uq.append(np.concatenate([a + np.arange(64), b + np.arange(64),
                                  a + 64 + np.arange(16), b + 64 + np.arange(16), z32q,
                                  a + 80 + np.arange(16), b + 80 + np.arange(16), z32q]))
    uq_idx = np.concatenate(uq)

    kn, mv = [], []
    for h in range(8):
        kn.append(h * 128 + np.arange(64))
        mv.append(h * 128 + 64 + np.arange(64))
    ukv_idx = np.concatenate(kn + mv)
    return w1_idx, uq_idx, ukv_idx


def _diff_layout():
    cols = []
    for part in range(2):
        for h in range(8):
            cols.append(part * 1024 + h * 128 + _PAIR_PERM)
    cols.append(2048 + np.arange(1024))
    return np.concatenate(cols)


def kernel(x, c, ctx, c_ctx, norm_mix, norm_ffn, ada_w, ada_b, ab_w_in, mla_q_norm, mla_w_uq, mla_kv_norm, mla_w_ukv, swa_sink, ab_w_out, diff_w_in, diff_lambda_q1, diff_lambda_k1, diff_lambda_q2, diff_lambda_k2, diff_subln, diff_w_out, router_group_w, router_group_b, router_expert_w, router_expert_b, expert_w1, expert_w3, expert_w2, final_norm):
    bsz, s_len, d = x.shape
    c_len = ctx.shape[1]
    depth = ada_w.shape[0]
    assert d == D_MODEL and c_len == TM and s_len % TQ == 0 and s_len % (2 * TK) == 0
    t = s_len + c_len

    xs = (x, ctx)
    tables = _rope_tables(s_len, c_len)
    c64, s64, c32, s32 = tables

    n_rows = (bsz + 1 + 7) // 8 * 8
    rows = jnp.concatenate([c, c_ctx[None, :], jnp.zeros((n_rows - bsz - 1, d), F32)], axis=0)
    mod_all = _ada_call(rows, ada_w, ada_b)
    mod_lat = mod_all[:, :bsz].reshape(depth, bsz, 1, 6, d)
    mod_ctx = jnp.broadcast_to(mod_all[:, bsz].reshape(depth, 1, 1, 6, d), (depth, bsz, 1, 6, d))
    mods = jnp.concatenate([mod_lat, mod_ctx], axis=2)
    mods = jnp.concatenate([mods, jnp.zeros((depth, bsz, 2, 2, d), F32)], axis=3)

    w1_idx, uq_idx, ukv_idx = _ab_layouts()
    diff_idx = _diff_layout()

    res = None
    for l in range(depth):
        j = l // 2
        mod = mods[l]
        if l % 2 == 0:
            weights = [_take_cols(ab_w_in[j], w1_idx), mla_q_norm[j].reshape(1, -1), mla_kv_norm[j].reshape(1, -1),
                       _take_cols(mla_w_uq[j], uq_idx), _take_cols(mla_w_ukv[j], ukv_idx)]
            xs, (sq, sk, sv, mq, mk, mv) = _proj_call("ab", xs, res, mod, norm_mix[l], weights,
                                                      [c64, s64, c32, s32], [512, 256, -256, 1024, 1024, -512])
            o_a = _swa_call(swa_sink[j], sq, sk, sv)
            o_b = _dense_attn_call("mla", mq, mk, mv)
            w_out = ab_w_out[j].astype(BF16)
            attn_outs, out_ws = [o_a, o_b], [w_out[:512], w_out[512:]]
        else:
            lambda_init = 0.8 - 0.6 * math.exp(-0.3 * l)
            weights = [_take_cols(diff_w_in[j], diff_idx)]
            xs, (dq, dk, dv) = _proj_call("diff", xs, res, mod, norm_mix[l], weights, [c64, s64], [1024, 1024, -1024])
            extra = [diff_lambda_q1[j].reshape(1, -1), diff_lambda_k1[j].reshape(1, -1),
                     diff_lambda_q2[j].reshape(1, -1), diff_lambda_k2[j].reshape(1, -1),
                     diff_subln[j].reshape(1, -1)]
            o_d = _dense_attn_call("diff", dq, dk, dv, extra=extra, lambda_init=lambda_init)
            attn_outs, out_ws = [o_d], [diff_w_out[j].astype(BF16)]

        wr = jnp.concatenate([router_group_w[l], router_expert_w[l],
                              jnp.zeros((d, LANES - N_GROUPS - N_EXPERTS), F32)], axis=1)
        wr_hi = wr.astype(BF16)
        wr = jnp.concatenate([wr_hi, (wr - wr_hi.astype(F32)).astype(BF16)], axis=1)
        br = jnp.concatenate([router_group_b[l], router_expert_b[l],
                              jnp.zeros((LANES - N_GROUPS - N_EXPERTS,), F32)]).reshape(1, LANES)
        xs, h2, logits = _out_proj_call(attn_outs, out_ws, xs, mod, norm_ffn[l], wr, br)

        n_tok = bsz * t
        rt, cnt, ga, gb = _router_call(logits.reshape(n_tok, LANES))
        slot_tok, block_e, n_used, dest = _dispatch_plan(rt, cnt)
        h2f = h2.reshape(n_tok, d)
        n_half = slot_tok.shape[0] // 2
        xa = jnp.take(h2f, slot_tok[:n_half], axis=0, mode="clip")
        xb = jnp.take(h2f, slot_tok[n_half:], axis=0, mode="clip")
        n_blocks = slot_tok.shape[0] // TMOE
        yb = _moe_call(l, block_e, n_used, xa, 0, n_blocks, expert_w1, expert_w3, expert_w2)
        yb = _moe_call(l, block_e, n_used, xb, n_blocks // 2, n_blocks, expert_w1, expert_w3, expert_w2, y_prev=yb)
        res = (jnp.take(yb, dest[0], axis=0, mode="clip").reshape(bsz, t, d),
               jnp.take(yb, dest[1], axis=0, mode="clip").reshape(bsz, t, d),
               ga.reshape(bsz, t, LANES), gb.reshape(bsz, t, LANES), mod)

    return _final_call(xs, res, final_norm, s_len)
```

```python
import functools
import math

import numpy as np
import jax
import jax.numpy as jnp
from jax import lax
from jax.experimental import pallas as pl
from jax.experimental.pallas import tpu as pltpu

F32 = jnp.float32
BF16 = jnp.bfloat16

D_MODEL = 1024
GRID_W = 64
ROPE_BASE = 10000.0
NORM_EPS = 1e-6
DIFF_EPS = 1e-5
NEG = -1e30

SWA_WINDOW = 128
MLA_SCALE = (64 + 32) ** -0.5
HEAD_SCALE = 64 ** -0.5
LOG2E = math.log2(math.e)

N_GROUPS = 4
EXPERTS_PER_GROUP = 8
N_EXPERTS = 32
TOP_K = 2

LANES = 128
TM = 256
TK = 256
TQ = 1024
TMOE = 256
MOE_PARTS = 4
VMEM_LIMIT = 56 * 1024 * 1024


def _cparams(n_axes):
    return pltpu.CompilerParams(dimension_semantics=("arbitrary",) * n_axes,
                                vmem_limit_bytes=VMEM_LIMIT)


def _rms(x, g, eps):
    return x * lax.rsqrt(jnp.mean(x * x, axis=-1, keepdims=True) + eps) * g


def _rope_block(x, c, s):
    return x * c + pltpu.roll(x, 64, 1) * s


_PAIR_PERM = np.concatenate([np.arange(0, 32), np.arange(64, 96), np.arange(32, 64), np.arange(96, 128)])


def _rope_tables(s_len, c_len):
    rows = s_len // GRID_W
    row = jnp.repeat(jnp.arange(rows, dtype=F32), GRID_W)
    col = jnp.tile(jnp.arange(GRID_W, dtype=F32), rows)

    def tab(dim):
        nf = dim // 4
        inv = ROPE_BASE ** (-jnp.arange(nf, dtype=F32) / nf)
        ang = jnp.concatenate([row[:, None] * inv, col[:, None] * inv], axis=-1)
        return jnp.cos(ang), jnp.sin(ang)

    cos64, sin64 = tab(64)
    cos32, sin32 = tab(32)
    c64 = jnp.concatenate([cos64] * 4, axis=-1)
    s64 = jnp.concatenate([-sin64, -sin64, sin64, sin64], axis=-1)
    one = jnp.ones((s_len, 32), F32)
    zero = jnp.zeros((s_len, 32), F32)
    c32 = jnp.concatenate([cos32, cos32, one, cos32, cos32, one], axis=-1)
    s32 = jnp.concatenate([-sin32, -sin32, zero, sin32, sin32, zero], axis=-1)

    def ext(t, fill):
        return jnp.concatenate([t, jnp.full((c_len, LANES), fill, F32)], axis=0)

    return ext(c64, 1.0), ext(s64, 0.0), ext(c32, 1.0), ext(s32, 0.0)


def _ada_kernel(x_ref, w_ref, b_ref, o_ref):
    x = x_ref[...]
    sx = x * jax.nn.sigmoid(x)
    o_ref[0] = jnp.dot(sx.astype(BF16), w_ref[0].astype(BF16), preferred_element_type=F32) + b_ref[0]


def _ada_call(rows, ada_w, ada_b):
    depth, d, n6 = ada_w.shape
    r = rows.shape[0]
    tn = 1536
    return pl.pallas_call(
        _ada_kernel,
        grid=(depth, n6 // tn),
        in_specs=[pl.BlockSpec((r, d), lambda l, j: (0, 0)),
                  pl.BlockSpec((1, d, tn), lambda l, j: (l, 0, j)),
                  pl.BlockSpec((1, 1, tn), lambda l, j: (l, 0, j))],
        out_specs=pl.BlockSpec((1, r, tn), lambda l, j: (l, 0, j)),
        out_shape=jax.ShapeDtypeStruct((depth, r, n6), F32),
        compiler_params=_cparams(2),
        name="ada_mod",
    )(rows, ada_w, ada_b.reshape(depth, 1, n6))


def _first_layer_x(x_ref, ctx_ref, nlat):
    return jnp.where(pl.program_id(1) < nlat, x_ref[0], ctx_ref[0])


def _moe_residual(x_ref, ya_ref, yb_ref, ga_ref, gb_ref, pmod_ref):
    reps = x_ref.shape[-1] // LANES
    ga = jnp.concatenate([ga_ref[0]] * reps, axis=1)
    gb = jnp.concatenate([gb_ref[0]] * reps, axis=1)
    y = ga * ya_ref[0] + gb * yb_ref[0]
    return x_ref[0] + pmod_ref[0, 0][5:6, :] * y


def _prenorm(has_res, nlat, x_refs, mod_ref, g_ref, xo_ref):
    if has_res:
        x = _moe_residual(*x_refs)
        xo_ref[0] = x
    else:
        x = _first_layer_x(x_refs[0], x_refs[1], nlat)
    m = mod_ref[0, 0]
    return _rms(x, g_ref[...], NORM_EPS) * (1.0 + m[1:2, :]) + m[0:1, :]


def _ab_proj_kernel(has_res, nlat, *refs):
    n_x = 6 if has_res else 2
    x_refs, refs = refs[:n_x], refs[n_x:]
    (mod_ref, g_ref, w1_ref, qn_ref, kvn_ref, wuq_ref, wukv_ref,
     c64_ref, s64_ref, c32_ref, s32_ref) = refs[:11]
    outs = refs[11:]
    if has_res:
        xo_ref, outs = outs[0], outs[1:]
    else:
        xo_ref = None
    sq_ref, sk_ref, sv_ref, mq_ref, mk_ref, mv_ref = outs

    h = _prenorm(has_res, nlat, x_refs, mod_ref, g_ref, xo_ref)
    p = jnp.dot(h.astype(BF16), w1_ref[...], preferred_element_type=F32)
    c64, s64, c32, s32 = c64_ref[...], s64_ref[...], c32_ref[...], s32_ref[...]
    for j in range(4):
        blk = _rope_block(p[:, j * 128:(j + 1) * 128], c64, s64)
        sq_ref[0, :, j * 128:(j + 1) * 128] = (blk * (HEAD_SCALE * LOG2E)).astype(BF16)
    for j in range(2):
        blk = _rope_block(p[:, 512 + j * 128:512 + (j + 1) * 128], c64, s64)
        sk_ref[0, :, j * 128:(j + 1) * 128] = blk.astype(BF16)
    sv_ref[0, 0] = p[:, 768:1024].T.astype(BF16)
    cq = p[:, 1024:1280]
    ckv = p[:, 1280:1408]
    kr = _rope_block(p[:, 1408:1536], c32, s32).astype(BF16)
    qm = jnp.dot(_rms(cq, qn_ref[...], NORM_EPS).astype(BF16), wuq_ref[...], preferred_element_type=F32)
    kv = jnp.dot(_rms(ckv, kvn_ref[...], NORM_EPS).astype(BF16), wukv_ref[...], preferred_element_type=F32)
    for j in range(4):
        mq_ref[0, :, j * 256:j * 256 + 128] = (qm[:, j * 256:j * 256 + 128] * (MLA_SCALE * LOG2E)).astype(BF16)
        rr = _rope_block(qm[:, j * 256 + 128:(j + 1) * 256], c32, s32)
        mq_ref[0, :, j * 256 + 128:(j + 1) * 256] = (rr * (MLA_SCALE * LOG2E)).astype(BF16)
        mk_ref[0, :, j * 256:j * 256 + 128] = kv[:, j * 128:(j + 1) * 128].astype(BF16)
        mk_ref[0, :, j * 256 + 128:(j + 1) * 256] = kr
    mv_ref[0, 0] = kv[:, 512:1024].T.astype(BF16)


def _diff_proj_kernel(has_res, nlat, *refs):
    n_x = 6 if has_res else 2
    x_refs, refs = refs[:n_x], refs[n_x:]
    mod_ref, g_ref, w_ref, c64_ref, s64_ref = refs[:5]
    outs = refs[5:]
    if has_res:
        xo_ref, outs = outs[0], outs[1:]
    else:
        xo_ref = None
    q_ref, k_ref, v_ref = outs
    h = _prenorm(has_res, nlat, x_refs, mod_ref, g_ref, xo_ref)
    p = jnp.dot(h.astype(BF16), w_ref[...], preferred_element_type=F32)
    c64, s64 = c64_ref[...], s64_ref[...]
    for j in range(8):
        blk = _rope_block(p[:, j * 128:(j + 1) * 128], c64, s64)
        q_ref[0, :, j * 128:(j + 1) * 128] = (blk * (HEAD_SCALE * LOG2E)).astype(BF16)
        blk = _rope_block(p[:, 1024 + j * 128:1024 + (j + 1) * 128], c64, s64)
        k_ref[0, :, j * 128:(j + 1) * 128] = blk.astype(BF16)
    v_ref[0, 0] = p[:, 2048:3072].T.astype(BF16)


def _row_spec(width):
    return pl.BlockSpec((1, TM, width), lambda b, i: (b, i, 0))


def _mod_spec(nlat):
    return pl.BlockSpec((1, 1, 8, D_MODEL), lambda b, i: (b, i // nlat, 0, 0))


def _full_spec(shape):
    nd = len(shape)
    return pl.BlockSpec(shape, lambda b, i: (0,) * nd)


def _tab_spec():
    return pl.BlockSpec((TM, LANES), lambda b, i: (i, 0))


def _split_specs(nlat, d):
    return [pl.BlockSpec((1, TM, d), lambda b, i: (b, jnp.minimum(i, nlat - 1), 0)),
            pl.BlockSpec((1, TM, d), lambda b, i: (b, 0, 0))]


def _proj_call(kind, x, res, mod, g, weights, tables, out_widths):
    has_res = res is not None
    if has_res:
        bsz, t, d = x.shape
    else:
        bsz, t, d = x[0].shape[0], x[0].shape[1] + x[1].shape[1], x[0].shape[2]
    nt = t // TM
    nlat = nt - 1
    if has_res:
        ya, yb, ga, gb, pmod = res
        ins = [x, ya, yb, ga, gb, pmod]
        specs = [_row_spec(d), _row_spec(d), _row_spec(d), _row_spec(LANES), _row_spec(LANES), _mod_spec(nlat)]
    else:
        ins = list(x)
        specs = _split_specs(nlat, d)
    ins += [mod, g.reshape(1, d)]
    specs += [_mod_spec(nlat), _full_spec((1, d))]
    for w in weights:
        ins.append(w)
        specs.append(_full_spec(w.shape))
    for tb in tables:
        ins.append(tb)
        specs.append(_tab_spec())
    out_shapes, out_specs = [], []
    if has_res:
        out_shapes.append(jax.ShapeDtypeStruct((bsz, t, d), F32))
        out_specs.append(_row_spec(d))
    for w in out_widths:
        if w < 0:
            out_shapes.append(jax.ShapeDtypeStruct((bsz, nt, -w, TM), BF16))
            out_specs.append(pl.BlockSpec((1, 1, -w, TM), lambda b, i: (b, i, 0, 0)))
        else:
            out_shapes.append(jax.ShapeDtypeStruct((bsz, t, w), BF16))
            out_specs.append(_row_spec(w))
    body = _ab_proj_kernel if kind == "ab" else _diff_proj_kernel
    outs = pl.pallas_call(
        functools.partial(body, has_res, nlat),
        grid=(bsz, nt),
        in_specs=specs,
        out_specs=out_specs,
        out_shape=out_shapes,
        compiler_params=_cparams(2),
        name=kind + "_proj",
    )(*ins)
    if has_res:
        return outs[0], outs[1:]
    return x, outs


def _pair_masks(mode, lane):
    if mode == "mla":
        in_a = (lane < 64) | ((lane >= 128) & (lane < 144)) | ((lane >= 192) & (lane < 208))
        in_b = ((lane >= 64) & (lane < 128)) | ((lane >= 144) & (lane < 160)) | ((lane >= 208) & (lane < 224))
    else:
        in_a = (lane < 32) | ((lane >= 64) & (lane < 96))
        in_b = ((lane >= 32) & (lane < 64)) | (lane >= 96)
    return in_a, in_b


def _dense_attn_kernel(mode, lambda_init, nlat, *refs):
    n_in = 8 if mode == "diff" else 3
    q_ref, k_ref, vt_ref = refs[:3]
    o_ref = refs[n_in]
    s_bufs = refs[n_in + 1:n_in + 3]
    p_bufs = refs[n_in + 3:n_in + 5]
    acc_buf, m_buf, l_buf, a_buf = refs[n_in + 5:]
    i = pl.program_id(2)
    qt = q_ref[0].T
    width, tq = qt.shape
    row = lax.broadcasted_iota(jnp.int32, (width, 1), 0)
    in_a, in_b = _pair_masks(mode, row)
    zero = jnp.zeros_like(qt)
    q2 = jnp.concatenate([jnp.where(in_a, qt, zero), jnp.where(in_b, qt, zero)], axis=1)
    chunks = [(nlat, 1)] + [(2 * c, 2) for c in range(nlat // 2)]
    n_lat_q = (nlat * TK) // tq

    def scores(chunk, s_buf):
        k0, n = chunk
        s_buf[0:n * TK, :] = jnp.dot(k_ref[0, k0 * TK:(k0 + n) * TK, :], q2, preferred_element_type=F32)

    def pv(chunk, p_buf):
        k0, n = chunk
        out = jnp.dot(vt_ref[0, k0], p_buf[0:TK, :], preferred_element_type=F32)
        for r in range(1, n):
            out = out + jnp.dot(vt_ref[0, k0 + r], p_buf[r * TK:(r + 1) * TK, :], preferred_element_type=F32)
        return out

    def softmax(chunk, s_buf, p_buf, first):
        n = chunk[1]
        s = s_buf[0:n * TK, :]
        mx = jnp.max(s, axis=0, keepdims=True)
        if first:
            mn = mx
        else:
            m = m_buf[...]
            mn = jnp.maximum(m, mx)
            a_buf[...] = jnp.exp2(m - mn)
        p = jnp.exp2(s - mn)
        ps = jnp.sum(p, axis=0, keepdims=True)
        l_buf[...] = ps if first else a_buf[...] * l_buf[...] + ps
        m_buf[...] = mn
        p_buf[0:n * TK, :] = p.astype(BF16)

    def accumulate(chunk, p_buf, first):
        if first:
            acc_buf[...] = pv(chunk, p_buf)
        else:
            acc_buf[...] = a_buf[...] * acc_buf[...] + pv(chunk, p_buf)

    def pipeline(chs):
        scores(chs[0], s_bufs[0])
        for c, ch in enumerate(chs):
            if c >= 2:
                accumulate(chs[c - 1], p_bufs[(c - 1) % 2], first=False)
            if c + 1 < len(chs):
                scores(chs[c + 1], s_bufs[(c + 1) % 2])
            softmax(ch, s_bufs[c % 2], p_bufs[c % 2], first=c == 0)
            if c == 1:
                accumulate(chs[0], p_bufs[0], first=True)
        last = len(chs) - 1
        if last == 0:
            accumulate(chs[0], p_bufs[0], first=True)
        else:
            accumulate(chs[last], p_bufs[last % 2], first=False)

    @pl.when(i < n_lat_q)
    def _():
        pipeline(chunks)

    @pl.when(i >= n_lat_q)
    def _():
        pipeline(chunks[:1])

    o2 = acc_buf[...] * (1.0 / l_buf[...])
    oa, ob = o2[:, :tq], o2[:, tq:]
    if mode == "diff":
        lq1_ref, lk1_ref, lq2_ref, lk2_ref, sub_ref = refs[3:8]
        lam = (jnp.exp(jnp.sum(lq1_ref[...] * lk1_ref[...], axis=1, keepdims=True))
               - jnp.exp(jnp.sum(lq2_ref[...] * lk2_ref[...], axis=1, keepdims=True)) + lambda_init)
        o = (oa - lam * ob).T
        o = _rms(o, sub_ref[...], DIFF_EPS) * (1.0 - lambda_init)
    else:
        vrow = lax.broadcasted_iota(jnp.int32, (LANES, 1), 0)
        o = jnp.where(vrow < 64, oa, ob).T
    o_ref[0] = o.astype(BF16)


def _dense_attn_call(mode, q, k, vt, extra=(), lambda_init=0.0):
    bsz, t, qtot = q.shape
    width = 256 if mode == "mla" else 128
    npairs = qtot // width
    nq = pl.cdiv(t, TQ)
    ins = [q, k, vt]
    specs = [pl.BlockSpec((1, TQ, width), lambda b, j, i: (b, i, j)),
             pl.BlockSpec((1, t, width), lambda b, j, i: (b, 0, j)),
             pl.BlockSpec((1, t // TK, LANES, TK), lambda b, j, i: (b, 0, j, 0))]
    for e in extra:
        ins.append(e)
        specs.append(pl.BlockSpec(e.shape, lambda b, j, i: (0, 0)))
    return pl.pallas_call(
        functools.partial(_dense_attn_kernel, mode, lambda_init, t // TK - 1),
        grid=(bsz, npairs, nq),
        in_specs=specs,
        out_specs=pl.BlockSpec((1, TQ, LANES), lambda b, j, i: (b, i, j)),
        out_shape=jax.ShapeDtypeStruct((bsz, t, npairs * LANES), BF16),
        scratch_shapes=[pltpu.VMEM((2 * TK, 2 * TQ), F32), pltpu.VMEM((2 * TK, 2 * TQ), F32),
                        pltpu.VMEM((2 * TK, 2 * TQ), BF16), pltpu.VMEM((2 * TK, 2 * TQ), BF16),
                        pltpu.VMEM((LANES, 2 * TQ), F32), pltpu.VMEM((1, 2 * TQ), F32),
                        pltpu.VMEM((1, 2 * TQ), F32), pltpu.VMEM((1, 2 * TQ), F32)],
        compiler_params=_cparams(3),
        name=mode + "_attn",
    )(*ins)


def _swa_kernel(nlat, sink_ref, q_ref, k_ref, vt_ref, o_ref):
    g = pl.program_id(1)
    i = pl.program_id(2)
    s_len = nlat * TM
    half = TM // 2
    q = q_ref[0]
    row = lax.broadcasted_iota(jnp.int32, (LANES, 1), 0)
    in_a, in_b = _pair_masks("pair", row)
    cols = []
    for pr in range(2):
        qt = q[:, pr * LANES:(pr + 1) * LANES].T
        zero = jnp.zeros_like(qt)
        cols += [jnp.where(in_a, qt, zero), jnp.where(in_b, qt, zero)]
    q4 = jnp.concatenate(cols, axis=1)

    start0 = pl.multiple_of(jnp.maximum(i * TM - half, 0), half)
    start1 = pl.multiple_of(i * TM, TM)
    start2 = pl.multiple_of(jnp.minimum((i + 1) * TM, s_len + half), half)
    k_cat = jnp.concatenate([k_ref[0, pl.ds(start0, half), :], k_ref[0, pl.ds(start1, TM), :],
                             k_ref[0, pl.ds(start2, half), :], k_ref[0, s_len:s_len + TM, :]], axis=0)
    b0 = jnp.maximum(i - 1, 0)
    b2 = jnp.minimum(i + 1, nlat)
    vt_cat = jnp.concatenate([vt_ref[0, b0][:, half:], vt_ref[0, i], vt_ref[0, b2][:, :half],
                              vt_ref[0, nlat]], axis=1)

    r = lax.broadcasted_iota(jnp.int32, (2 * TM, 1), 0)
    far = -4 * SWA_WINDOW
    lat = i < nlat
    pos0 = jnp.where(lat & (i >= 1), start0 + r, far)
    pos1 = jnp.where(lat, start1 + r - half, far)
    pos2 = jnp.where(lat & (i + 1 < nlat), start2 + r - half - TM, far)
    kpos = jnp.where(r < half, pos0, jnp.where(r < half + TM, pos1, pos2))
    qpos = i * TM + lax.broadcasted_iota(jnp.int32, (1, TM), 1)
    band = jnp.abs(qpos - kpos) <= SWA_WINDOW
    band4 = jnp.concatenate([band] * 4, axis=1)

    s = jnp.dot(k_cat, q4, preferred_element_type=F32)
    s_loc = jnp.where(band4, s[:2 * TM], NEG)
    s_ctx = s[2 * TM:]
    sink = jnp.concatenate([jnp.full((1, TM), sink_ref[4 * g + h], F32) for h in range(4)], axis=1) * LOG2E
    m = jnp.maximum(jnp.maximum(jnp.max(s_loc, axis=0, keepdims=True), jnp.max(s_ctx, axis=0, keepdims=True)), sink)
    p_loc = jnp.exp2(s_loc - m)
    p_ctx = jnp.exp2(s_ctx - m)
    l = jnp.sum(p_loc, axis=0, keepdims=True) + jnp.sum(p_ctx, axis=0, keepdims=True) + jnp.exp2(sink - m)
    p = jnp.concatenate([p_loc, p_ctx], axis=0).astype(BF16)
    o4 = jnp.dot(vt_cat, p, preferred_element_type=F32) * (1.0 / l)
    vrow = lax.broadcasted_iota(jnp.int32, (LANES, 1), 0)
    for pr in range(2):
        oa = o4[:, (2 * pr) * TM:(2 * pr + 1) * TM]
        ob = o4[:, (2 * pr + 1) * TM:(2 * pr + 2) * TM]
        o_ref[0, :, pr * LANES:(pr + 1) * LANES] = jnp.where(vrow < 64, oa, ob).T.astype(BF16)


def _swa_call(sink, q, k, vt):
    bsz, t, qtot = q.shape
    nkv = k.shape[-1] // LANES
    nt = t // TM
    grid_spec = pltpu.PrefetchScalarGridSpec(
        num_scalar_prefetch=1,
        grid=(bsz, nkv, nt),
        in_specs=[pl.BlockSpec((1, TM, 2 * LANES), lambda b, g, i, s: (b, i, g)),
                  pl.BlockSpec((1, t, LANES), lambda b, g, i, s: (b, 0, g)),
                  pl.BlockSpec((1, nt, LANES, TM), lambda b, g, i, s: (b, 0, g, 0))],
        out_specs=pl.BlockSpec((1, TM, 2 * LANES), lambda b, g, i, s: (b, i, g)),
    )
    return pl.pallas_call(
        functools.partial(_swa_kernel, nt - 1),
        grid_spec=grid_spec,
        out_shape=jax.ShapeDtypeStruct((bsz, t, qtot), BF16),
        compiler_params=_cparams(3),
        name="swa_attn",
    )(sink, q, k, vt)


def _out_proj_kernel(n_o, split, nlat, *refs):
    o_refs = refs[:n_o]
    w_refs = refs[n_o:2 * n_o]
    refs = refs[2 * n_o:]
    if split:
        x_in = _first_layer_x(refs[0], refs[1], nlat)
        refs = refs[2:]
    else:
        x_in = refs[0][0]
        refs = refs[1:]
    mod_ref, g_ref, wr_ref, br_ref, xo_ref, h_ref, lg_ref = refs
    acc = jnp.dot(o_refs[0][0], w_refs[0][...], preferred_element_type=F32)
    for n in range(1, n_o):
        acc = acc + jnp.dot(o_refs[n][0], w_refs[n][...], preferred_element_type=F32)
    m = mod_ref[0, 0]
    x = x_in + m[2:3, :] * acc
    xo_ref[0] = x
    h = _rms(x, g_ref[...], NORM_EPS) * (1.0 + m[4:5, :]) + m[3:4, :]
    hi = h.astype(BF16)
    lo = (h - hi.astype(F32)).astype(BF16)
    h_ref[0] = hi
    both = jnp.dot(hi, wr_ref[...], preferred_element_type=F32)
    lg_ref[0] = (both[:, :LANES] + both[:, LANES:]
                 + jnp.dot(lo, wr_ref[:, :LANES], preferred_element_type=F32) + br_ref[...])


def _out_proj_call(os_, ws, x, mod, g, wr, br):
    split = isinstance(x, (tuple, list))
    if split:
        bsz, t, d = x[0].shape[0], x[0].shape[1] + x[1].shape[1], x[0].shape[2]
    else:
        bsz, t, d = x.shape
    nt = t // TM
    nlat = nt - 1
    n_o = len(os_)
    x_ins = list(x) if split else [x]
    x_specs = _split_specs(nlat, d) if split else [_row_spec(d)]
    ins = list(os_) + list(ws) + x_ins + [mod, g.reshape(1, d), wr, br]
    specs = ([_row_spec(o.shape[-1]) for o in os_] + [_full_spec(w.shape) for w in ws]
             + x_specs + [_mod_spec(nlat), _full_spec((1, d)), _full_spec(wr.shape), _full_spec(br.shape)])
    return pl.pallas_call(
        functools.partial(_out_proj_kernel, n_o, split, nlat),
        grid=(bsz, nt),
        in_specs=specs,
        out_specs=[_row_spec(d), _row_spec(d), _row_spec(LANES)],
        out_shape=[jax.ShapeDtypeStruct((bsz, t, d), F32),
                   jax.ShapeDtypeStruct((bsz, t, d), BF16),
                   jax.ShapeDtypeStruct((bsz, t, LANES), F32)],
        compiler_params=_cparams(2),
        name="out_proj",
    )(*ins)


def _moe_kernel(first, be_ref, nu_ref, x_ref, w1_ref, w3_ref, w2_ref, *rest):
    o_ref, w1c, w3c, w2c = rest[-4:]
    i = pl.program_id(0)
    gi = first + i
    e = be_ref[gi]

    @pl.when((i == 0) | (e != be_ref[jnp.maximum(gi - 1, 0)]))
    def _():
        w1c[...] = w1_ref[0, 0].astype(BF16)
        w3c[...] = w3_ref[0, 0].astype(BF16)
        w2c[...] = w2_ref[0, 0].astype(BF16)

    @pl.when(gi < nu_ref[0])
    def _():
        x = x_ref[...]
        a = jnp.dot(x, w1c[...], preferred_element_type=F32)
        b = jnp.dot(x, w3c[...], preferred_element_type=F32)
        hmid = (a * jax.nn.sigmoid(a)) * b
        o_ref[...] = jnp.dot(hmid.astype(BF16), w2c[...], preferred_element_type=F32)

    @pl.when(gi >= nu_ref[0])
    def _():
        o_ref[...] = jnp.zeros_like(o_ref)


def _moe_call(layer, block_e, n_used, x_part, first, n_blocks, w1, w3, w2, y_prev=None):
    d = x_part.shape[1]
    steps = x_part.shape[0] // TMOE
    ff = w1.shape[-1]
    in_specs = [pl.BlockSpec((TMOE, d), lambda i, be, nu: (i, 0)),
                pl.BlockSpec((1, 1, d, ff), lambda i, be, nu: (layer, be[first + i], 0, 0)),
                pl.BlockSpec((1, 1, d, ff), lambda i, be, nu: (layer, be[first + i], 0, 0)),
                pl.BlockSpec((1, 1, ff, d), lambda i, be, nu: (layer, be[first + i], 0, 0))]
    ins = [block_e, n_used, x_part, w1, w3, w2]
    aliases = {}
    if y_prev is not None:
        in_specs.append(pl.BlockSpec(memory_space=pl.ANY))
        aliases = {len(ins): 0}
        ins.append(y_prev)
    grid_spec = pltpu.PrefetchScalarGridSpec(
        num_scalar_prefetch=2,
        grid=(steps,),
        in_specs=in_specs,
        out_specs=pl.BlockSpec((TMOE, d), lambda i, be, nu: (first + i, 0)),
        scratch_shapes=[pltpu.VMEM((d, ff), BF16), pltpu.VMEM((d, ff), BF16), pltpu.VMEM((ff, d), BF16)],
    )
    return pl.pallas_call(
        functools.partial(_moe_kernel, first),
        grid_spec=grid_spec,
        out_shape=jax.ShapeDtypeStruct((n_blocks * TMOE, d), F32),
        input_output_aliases=aliases,
        compiler_params=_cparams(1),
        name="moe_experts",
    )(*ins)


def _router_kernel(lg_ref, tri_ref, rt_ref, cnt_ref, ga_ref, gb_ref, run):
    i = pl.program_id(0)

    @pl.when(i == 0)
    def _():
        run[...] = jnp.zeros_like(run)

    lg = lg_ref[...]
    lane = lax.broadcasted_iota(jnp.int32, lg.shape, 1)
    gmask = lane < N_GROUPS
    gl = jnp.where(gmask, lg, NEG)
    gmax = jnp.max(gl, axis=1, keepdims=True)
    grp = jnp.min(jnp.where(gl == gmax, lane, LANES), axis=1, keepdims=True)
    p_grp = 1.0 / jnp.sum(jnp.where(gmask, jnp.exp(lg - gmax), 0.0), axis=1, keepdims=True)
    first = N_GROUPS + EXPERTS_PER_GROUP * grp
    emask = (lane >= first) & (lane < first + EXPERTS_PER_GROUP)
    el = jnp.where(emask, lg, NEG)
    e1 = jnp.max(el, axis=1, keepdims=True)
    i1 = jnp.min(jnp.where(el == e1, lane, LANES), axis=1, keepdims=True)
    el2 = jnp.where(lane == i1, NEG, el)
    e2 = jnp.max(el2, axis=1, keepdims=True)
    i2 = jnp.min(jnp.where(el2 == e2, lane, LANES), axis=1, keepdims=True)
    tt = jnp.exp(e2 - e1)
    g1 = p_grp / (1.0 + tt)
    g2 = g1 * tt
    oh1 = lane == i1
    oh2 = lane == i2
    onehot = jnp.where(oh1 | oh2, 1.0, 0.0)
    rank_all = jnp.dot(tri_ref[...], onehot.astype(BF16), preferred_element_type=F32) + run[...]
    r1 = jnp.sum(jnp.where(oh1, rank_all, 0.0), axis=1, keepdims=True)
    r2 = jnp.sum(jnp.where(oh2, rank_all, 0.0), axis=1, keepdims=True)
    run[...] = run[...] + jnp.sum(onehot, axis=0, keepdims=True)
    cnt_ref[...] = run[...]
    vals = [(i1 - N_GROUPS).astype(F32), (i2 - N_GROUPS).astype(F32), r1, r2, g1, g2]
    packed = jnp.zeros(lg.shape, F32)
    for n, v in enumerate(vals):
        packed = jnp.where(lane == n, v, packed)
    rt_ref[...] = packed.T[0:8, :]
    ga_ref[...] = jnp.broadcast_to(g1, lg.shape)
    gb_ref[...] = jnp.broadcast_to(g2, lg.shape)


def _router_call(logits):
    n_tok = logits.shape[0]
    tri = (jnp.arange(TM)[:, None] > jnp.arange(TM)[None, :]).astype(BF16)
    return pl.pallas_call(
        _router_kernel,
        grid=(n_tok // TM,),
        in_specs=[pl.BlockSpec((TM, LANES), lambda i: (i, 0)),
                  pl.BlockSpec((TM, TM), lambda i: (0, 0))],
        out_specs=[pl.BlockSpec((8, TM), lambda i: (0, i)),
                   pl.BlockSpec((1, LANES), lambda i: (0, 0)),
                   pl.BlockSpec((TM, LANES), lambda i: (i, 0)),
                   pl.BlockSpec((TM, LANES), lambda i: (i, 0))],
        out_shape=[jax.ShapeDtypeStruct((8, n_tok), F32), jax.ShapeDtypeStruct((1, LANES), F32),
                   jax.ShapeDtypeStruct((n_tok, LANES), F32), jax.ShapeDtypeStruct((n_tok, LANES), F32)],
        scratch_shapes=[pltpu.VMEM((1, LANES), F32)],
        compiler_params=_cparams(1),
        name="router",
    )(logits, tri)


def _dispatch_plan(rt, cnt):
    n_tok = rt.shape[1]
    eid = rt[0:2].astype(jnp.int32)
    rank = rt[2:4].astype(jnp.int32)
    counts = cnt[0, N_GROUPS:N_GROUPS + N_EXPERTS].astype(jnp.int32)
    padded = (counts + TMOE - 1) // TMOE * TMOE
    pad_end = jnp.cumsum(padded)
    pad_start = pad_end - padded
    experts = jnp.arange(N_EXPERTS, dtype=jnp.int32)[:, None, None]
    dest = jnp.sum(jnp.where(eid[None] == experts, pad_start[:, None, None], 0), axis=0) + rank
    n_assign = n_tok * TOP_K
    n_blocks = (n_assign + N_EXPERTS * (TMOE - 1) + TMOE - 1) // TMOE
    n_blocks = (n_blocks + MOE_PARTS - 1) // MOE_PARTS * MOE_PARTS
    n_slots = n_blocks * TMOE
    tok = jnp.arange(n_tok, dtype=jnp.int32)
    slot_tok = jnp.zeros((n_slots,), jnp.int32).at[dest.reshape(n_assign)].set(
        jnp.concatenate([tok, tok]), unique_indices=True)
    block_start = jnp.arange(n_blocks, dtype=jnp.int32) * TMOE
    block_e = jnp.minimum(jnp.sum((block_start[:, None] >= pad_end[None, :]).astype(jnp.int32), axis=1),
                          N_EXPERTS - 1).astype(jnp.int32)
    n_used = (pad_end[-1:] // TMOE).astype(jnp.int32)
    return slot_tok, block_e, n_used, dest


def _final_kernel(x_ref, ya_ref, yb_ref, ga_ref, gb_ref, pmod_ref, g_ref, o_ref):
    x = _moe_residual(x_ref, ya_ref, yb_ref, ga_ref, gb_ref, pmod_ref)
    o_ref[0] = _rms(x, g_ref[...], NORM_EPS)


def _final_call(x, res, g, s_len):
    bsz, t, d = x.shape
    ya, yb, ga, gb, pmod = res
    return pl.pallas_call(
        _final_kernel,
        grid=(bsz, s_len // TM),
        in_specs=[_row_spec(d), _row_spec(d), _row_spec(d), _row_spec(LANES), _row_spec(LANES),
                  pl.BlockSpec((1, 1, 8, d), lambda b, i: (b, 0, 0, 0)),
                  _full_spec((1, d))],
        out_specs=_row_spec(d),
        out_shape=jax.ShapeDtypeStruct((bsz, s_len, d), F32),
        compiler_params=_cparams(2),
        name="final_norm",
    )(x, ya, yb, ga, gb, pmod, g.reshape(1, d))


def _take_cols(w, idx):
    wz = jnp.concatenate([w, jnp.zeros((w.shape[0], 1), w.dtype)], axis=1)
    return jnp.take(wz, jnp.asarray(idx, dtype=jnp.int32), axis=1).astype(BF16)


def _ab_layouts():
    zc = 1184
    cols = []
    for j in range(4):
        cols.append(j * 128 + _PAIR_PERM)
    for g in range(2):
        base = 512 + g * 64
        cols.append(base + np.concatenate([np.arange(0, 32), np.arange(0, 32), np.arange(32, 64), np.arange(32, 64)]))
    for g in range(2):
        base = 640 + g * 64
        cols.append(base + np.concatenate([np.arange(64), np.arange(64)]))
    cols.append(768 + np.arange(256))
    cols.append(1024 + np.arange(128))
    kr = 1152
    z32 = np.full((32,), zc)
    cols.append(np.concatenate([kr + np.arange(16), kr + np.arange(16), z32,
                                kr + 16 + np.arange(16), kr + 16 + np.arange(16), z32]))
    w1_idx = np.concatenate(cols)

    zq = 768
    uq = []
    z32q = np.full((32,), zq)
    for j in range(4):
        a, b = 2 * j * 96, (2 * j + 1) * 96
        uq.append(np.concatenate([a + np.arange(64), b + np.arange(64),
                                  a + 64 + np.arange(16), b + 64 + np.arange(16), z32q,
                                  a + 80 + np.arange(16), b + 80 + np.arange(16), z32q]))
    uq_idx = np.concatenate(uq)

    kn, mv = [], []
    for h in range(8):
        kn.append(h * 128 + np.arange(64))
        mv.append(h * 128 + 64 + np.arange(64))
    ukv_idx = np.concatenate(kn + mv)
    return w1_idx, uq_idx, ukv_idx


def _diff_layout():
    cols = []
    for part in range(2):
        for h in range(8):
            cols.append(part * 1024 + h * 128 + _PAIR_PERM)
    cols.append(2048 + np.arange(1024))
    return np.concatenate(cols)


def kernel(x, c, ctx, c_ctx, norm_mix, norm_ffn, ada_w, ada_b, ab_w_in, mla_q_norm, mla_w_uq, mla_kv_norm, mla_w_ukv, swa_sink, ab_w_out, diff_w_in, diff_lambda_q1, diff_lambda_k1, diff_lambda_q2, diff_lambda_k2, diff_subln, diff_w_out, router_group_w, router_group_b, router_expert_w, router_expert_b, expert_w1, expert_w3, expert_w2, final_norm):
    bsz, s_len, d = x.shape
    c_len = ctx.shape[1]
    depth = ada_w.shape[0]
    assert d == D_MODEL and c_len == TM and s_len % TQ == 0 and s_len % (2 * TK) == 0
    t = s_len + c_len

    xs = (x, ctx)
    tables = _rope_tables(s_len, c_len)
    c64, s64, c32, s32 = tables

    n_rows = (bsz + 1 + 7) // 8 * 8
    rows = jnp.concatenate([c, c_ctx[None, :], jnp.zeros((n_rows - bsz - 1, d), F32)], axis=0)
    mod_all = _ada_call(rows, ada_w, ada_b)
    mod_lat = mod_all[:, :bsz].reshape(depth, bsz, 1, 6, d)
    mod_ctx = jnp.broadcast_to(mod_all[:, bsz].reshape(depth, 1, 1, 6, d), (depth, bsz, 1, 6, d))
    mods = jnp.concatenate([mod_lat, mod_ctx], axis=2)
    mods = jnp.concatenate([mods, jnp.zeros((depth, bsz, 2, 2, d), F32)], axis=3)

    w1_idx, uq_idx, ukv_idx = _ab_layouts()
    diff_idx = _diff_layout()

    res = None
    for l in range(depth):
        j = l // 2
        mod = mods[l]
        if l % 2 == 0:
            weights = [_take_cols(ab_w_in[j], w1_idx), mla_q_norm[j].reshape(1, -1), mla_kv_norm[j].reshape(1, -1),
                       _take_cols(mla_w_uq[j], uq_idx), _take_cols(mla_w_ukv[j], ukv_idx)]
            xs, (sq, sk, sv, mq, mk, mv) = _proj_call("ab", xs, res, mod, norm_mix[l], weights,
                                                      [c64, s64, c32, s32], [512, 256, -256, 1024, 1024, -512])
            o_a = _swa_call(swa_sink[j], sq, sk, sv)
            o_b = _dense_attn_call("mla", mq, mk, mv)
            w_out = ab_w_out[j].astype(BF16)
            attn_outs, out_ws = [o_a, o_b], [w_out[:512], w_out[512:]]
        else:
            lambda_init = 0.8 - 0.6 * math.exp(-0.3 * l)
            weights = [_take_cols(diff_w_in[j], diff_idx)]
            xs, (dq, dk, dv) = _proj_call("diff", xs, res, mod, norm_mix[l], weights, [c64, s64], [1024, 1024, -1024])
            extra = [diff_lambda_q1[j].reshape(1, -1), diff_lambda_k1[j].reshape(1, -1),
                     diff_lambda_q2[j].reshape(1, -1), diff_lambda_k2[j].reshape(1, -1),
                     diff_subln[j].reshape(1, -1)]
            o_d = _dense_attn_call("diff", dq, dk, dv, extra=extra, lambda_init=lambda_init)
            attn_outs, out_ws = [o_d], [diff_w_out[j].astype(BF16)]

        wr = jnp.concatenate([router_group_w[l], router_expert_w[l],
                              jnp.zeros((d, LANES - N_GROUPS - N_EXPERTS), F32)], axis=1)
        wr_hi = wr.astype(BF16)
        wr = jnp.concatenate([wr_hi, (wr - wr_hi.astype(F32)).astype(BF16)], axis=1)
        br = jnp.concatenate([router_group_b[l], router_expert_b[l],
                              jnp.zeros((LANES - N_GROUPS - N_EXPERTS,), F32)]).reshape(1, LANES)
        xs, h2, logits = _out_proj_call(attn_outs, out_ws, xs, mod, norm_ffn[l], wr, br)

        n_tok = bsz * t
        rt, cnt, ga, gb = _router_call(logits.reshape(n_tok, LANES))
        slot_tok, block_e, n_used, dest = _dispatch_plan(rt, cnt)
        h2f = h2.reshape(n_tok, d)
        n_blocks = slot_tok.shape[0] // TMOE
        part = n_blocks // MOE_PARTS
        yb = None
        for p in range(MOE_PARTS):
            x_part = jnp.take(h2f, slot_tok[p * part * TMOE:(p + 1) * part * TMOE], axis=0, mode="clip")
            yb = _moe_call(l, block_e, n_used, x_part, p * part, n_blocks, expert_w1, expert_w3, expert_w2, y_prev=yb)
        res = (jnp.take(yb, dest[0], axis=0, mode="clip").reshape(bsz, t, d),
               jnp.take(yb, dest[1], axis=0, mode="clip").reshape(bsz, t, d),
               ga.reshape(bsz, t, LANES), gb.reshape(bsz, t, LANES), mod)

    return _final_call(xs, res, final_norm, s_len)
```

```python
import functools
import math

import numpy as np
import jax
import jax.numpy as jnp
from jax import lax
from jax.experimental import pallas as pl
from jax.experimental.pallas import tpu as pltpu

F32 = jnp.float32
BF16 = jnp.bfloat16

D_MODEL = 1024
GRID_W = 64
ROPE_BASE = 10000.0
NORM_EPS = 1e-6
DIFF_EPS = 1e-5
NEG = -1e30

SWA_WINDOW = 128
MLA_SCALE = (64 + 32) ** -0.5
HEAD_SCALE = 64 ** -0.5
LOG2E = math.log2(math.e)

N_GROUPS = 4
EXPERTS_PER_GROUP = 8
N_EXPERTS = 32
TOP_K = 2

LANES = 128
TM = 256
TK = 256
TQ = 1024
TMOE = 256
MOE_PARTS = 4
VMEM_LIMIT = 56 * 1024 * 1024


def _cparams(n_axes):
    return pltpu.CompilerParams(dimension_semantics=("arbitrary",) * n_axes,
                                vmem_limit_bytes=VMEM_LIMIT)


def _rms(x, g, eps):
    return x * lax.rsqrt(jnp.mean(x * x, axis=-1, keepdims=True) + eps) * g


def _rope_block(x, c, s):
    return x * c + pltpu.roll(x, 64, 1) * s


_PAIR_PERM = np.concatenate([np.arange(0, 32), np.arange(64, 96), np.arange(32, 64), np.arange(96, 128)])


def _rope_tables(s_len, c_len):
    rows = s_len // GRID_W
    row = jnp.repeat(jnp.arange(rows, dtype=F32), GRID_W)
    col = jnp.tile(jnp.arange(GRID_W, dtype=F32), rows)

    def tab(dim):
        nf = dim // 4
        inv = ROPE_BASE ** (-jnp.arange(nf, dtype=F32) / nf)
        ang = jnp.concatenate([row[:, None] * inv, col[:, None] * inv], axis=-1)
        return jnp.cos(ang), jnp.sin(ang)

    cos64, sin64 = tab(64)
    cos32, sin32 = tab(32)
    c64 = jnp.concatenate([cos64] * 4, axis=-1)
    s64 = jnp.concatenate([-sin64, -sin64, sin64, sin64], axis=-1)
    one = jnp.ones((s_len, 32), F32)
    zero = jnp.zeros((s_len, 32), F32)
    c32 = jnp.concatenate([cos32, cos32, one, cos32, cos32, one], axis=-1)
    s32 = jnp.concatenate([-sin32, -sin32, zero, sin32, sin32, zero], axis=-1)

    def ext(t, fill):
        return jnp.concatenate([t, jnp.full((c_len, LANES), fill, F32)], axis=0)

    return ext(c64, 1.0), ext(s64, 0.0), ext(c32, 1.0), ext(s32, 0.0)


def _ada_kernel(x_ref, w_ref, b_ref, o_ref):
    x = x_ref[...]
    sx = x * jax.nn.sigmoid(x)
    o_ref[0] = jnp.dot(sx.astype(BF16), w_ref[0].astype(BF16), preferred_element_type=F32) + b_ref[0]


def _ada_call(rows, ada_w, ada_b):
    depth, d, n6 = ada_w.shape
    r = rows.shape[0]
    tn = 1536
    return pl.pallas_call(
        _ada_kernel,
        grid=(depth, n6 // tn),
        in_specs=[pl.BlockSpec((r, d), lambda l, j: (0, 0)),
                  pl.BlockSpec((1, d, tn), lambda l, j: (l, 0, j)),
                  pl.BlockSpec((1, 1, tn), lambda l, j: (l, 0, j))],
        out_specs=pl.BlockSpec((1, r, tn), lambda l, j: (l, 0, j)),
        out_shape=jax.ShapeDtypeStruct((depth, r, n6), F32),
        compiler_params=_cparams(2),
        name="ada_mod",
    )(rows, ada_w, ada_b.reshape(depth, 1, n6))


def _first_layer_x(x_ref, ctx_ref, nlat):
    return jnp.where(pl.program_id(1) < nlat, x_ref[0], ctx_ref[0])


def _moe_residual(x_ref, ya_ref, yb_ref, ga_ref, gb_ref, pmod_ref):
    reps = x_ref.shape[-1] // LANES
    ga = jnp.concatenate([ga_ref[0]] * reps, axis=1)
    gb = jnp.concatenate([gb_ref[0]] * reps, axis=1)
    y = ga * ya_ref[0] + gb * yb_ref[0]
    return x_ref[0] + pmod_ref[0, 0][5:6, :] * y


def _prenorm(has_res, nlat, x_refs, mod_ref, g_ref, xo_ref):
    if has_res:
        x = _moe_residual(*x_refs)
        xo_ref[0] = x
    else:
        x = _first_layer_x(x_refs[0], x_refs[1], nlat)
    m = mod_ref[0, 0]
    return _rms(x, g_ref[...], NORM_EPS) * (1.0 + m[1:2, :]) + m[0:1, :]


def _ab_proj_kernel(has_res, nlat, *refs):
    n_x = 6 if has_res else 2
    x_refs, refs = refs[:n_x], refs[n_x:]
    (mod_ref, g_ref, w1_ref, qn_ref, kvn_ref, wuq_ref, wukv_ref,
     c64_ref, s64_ref, c32_ref, s32_ref) = refs[:11]
    outs = refs[11:]
    if has_res:
        xo_ref, outs = outs[0], outs[1:]
    else:
        xo_ref = None
    sq_ref, sk_ref, sv_ref, mq_ref, mk_ref, mv_ref = outs

    h = _prenorm(has_res, nlat, x_refs, mod_ref, g_ref, xo_ref)
    p = jnp.dot(h.astype(BF16), w1_ref[...], preferred_element_type=F32)
    c64, s64, c32, s32 = c64_ref[...], s64_ref[...], c32_ref[...], s32_ref[...]
    for j in range(4):
        blk = _rope_block(p[:, j * 128:(j + 1) * 128], c64, s64)
        sq_ref[0, :, j * 128:(j + 1) * 128] = (blk * (HEAD_SCALE * LOG2E)).astype(BF16)
    for j in range(2):
        blk = _rope_block(p[:, 512 + j * 128:512 + (j + 1) * 128], c64, s64)
        sk_ref[0, :, j * 128:(j + 1) * 128] = blk.astype(BF16)
    sv_ref[0, 0] = p[:, 768:1024].T.astype(BF16)
    cq = p[:, 1024:1280]
    ckv = p[:, 1280:1408]
    kr = _rope_block(p[:, 1408:1536], c32, s32).astype(BF16)
    qm = jnp.dot(_rms(cq, qn_ref[...], NORM_EPS).astype(BF16), wuq_ref[...], preferred_element_type=F32)
    kv = jnp.dot(_rms(ckv, kvn_ref[...], NORM_EPS).astype(BF16), wukv_ref[...], preferred_element_type=F32)
    for j in range(4):
        mq_ref[0, :, j * 256:j * 256 + 128] = (qm[:, j * 256:j * 256 + 128] * (MLA_SCALE * LOG2E)).astype(BF16)
        rr = _rope_block(qm[:, j * 256 + 128:(j + 1) * 256], c32, s32)
        mq_ref[0, :, j * 256 + 128:(j + 1) * 256] = (rr * (MLA_SCALE * LOG2E)).astype(BF16)
        mk_ref[0, :, j * 256:j * 256 + 128] = kv[:, j * 128:(j + 1) * 128].astype(BF16)
        mk_ref[0, :, j * 256 + 128:(j + 1) * 256] = kr
    mv_ref[0, 0] = kv[:, 512:1024].T.astype(BF16)


def _diff_proj_kernel(has_res, nlat, *refs):
    n_x = 6 if has_res else 2
    x_refs, refs = refs[:n_x], refs[n_x:]
    mod_ref, g_ref, w_ref, c64_ref, s64_ref = refs[:5]
    outs = refs[5:]
    if has_res:
        xo_ref, outs = outs[0], outs[1:]
    else:
        xo_ref = None
    q_ref, k_ref, v_ref = outs
    h = _prenorm(has_res, nlat, x_refs, mod_ref, g_ref, xo_ref)
    p = jnp.dot(h.astype(BF16), w_ref[...], preferred_element_type=F32)
    c64, s64 = c64_ref[...], s64_ref[...]
    for j in range(8):
        blk = _rope_block(p[:, j * 128:(j + 1) * 128], c64, s64)
        q_ref[0, :, j * 128:(j + 1) * 128] = (blk * (HEAD_SCALE * LOG2E)).astype(BF16)
        blk = _rope_block(p[:, 1024 + j * 128:1024 + (j + 1) * 128], c64, s64)
        k_ref[0, :, j * 128:(j + 1) * 128] = blk.astype(BF16)
    v_ref[0, 0] = p[:, 2048:3072].T.astype(BF16)


def _row_spec(width):
    return pl.BlockSpec((1, TM, width), lambda b, i: (b, i, 0))


def _mod_spec(nlat):
    return pl.BlockSpec((1, 1, 8, D_MODEL), lambda b, i: (b, i // nlat, 0, 0))


def _full_spec(shape):
    nd = len(shape)
    return pl.BlockSpec(shape, lambda b, i: (0,) * nd)


def _tab_spec():
    return pl.BlockSpec((TM, LANES), lambda b, i: (i, 0))


def _split_specs(nlat, d):
    return [pl.BlockSpec((1, TM, d), lambda b, i: (b, jnp.minimum(i, nlat - 1), 0)),
            pl.BlockSpec((1, TM, d), lambda b, i: (b, 0, 0))]


def _proj_call(kind, x, res, mod, g, weights, tables, out_widths):
    has_res = res is not None
    if has_res:
        bsz, t, d = x.shape
    else:
        bsz, t, d = x[0].shape[0], x[0].shape[1] + x[1].shape[1], x[0].shape[2]
    nt = t // TM
    nlat = nt - 1
    if has_res:
        ya, yb, ga, gb, pmod = res
        ins = [x, ya, yb, ga, gb, pmod]
        specs = [_row_spec(d), _row_spec(d), _row_spec(d), _row_spec(LANES), _row_spec(LANES), _mod_spec(nlat)]
    else:
        ins = list(x)
        specs = _split_specs(nlat, d)
    ins += [mod, g.reshape(1, d)]
    specs += [_mod_spec(nlat), _full_spec((1, d))]
    for w in weights:
        ins.append(w)
        specs.append(_full_spec(w.shape))
    for tb in tables:
        ins.append(tb)
        specs.append(_tab_spec())
    out_shapes, out_specs = [], []
    if has_res:
        out_shapes.append(jax.ShapeDtypeStruct((bsz, t, d), F32))
        out_specs.append(_row_spec(d))
    for w in out_widths:
        if w < 0:
            out_shapes.append(jax.ShapeDtypeStruct((bsz, nt, -w, TM), BF16))
            out_specs.append(pl.BlockSpec((1, 1, -w, TM), lambda b, i: (b, i, 0, 0)))
        else:
            out_shapes.append(jax.ShapeDtypeStruct((bsz, t, w), BF16))
            out_specs.append(_row_spec(w))
    body = _ab_proj_kernel if kind == "ab" else _diff_proj_kernel
    outs = pl.pallas_call(
        functools.partial(body, has_res, nlat),
        grid=(bsz, nt),
        in_specs=specs,
        out_specs=out_specs,
        out_shape=out_shapes,
        compiler_params=_cparams(2),
        name=kind + "_proj",
    )(*ins)
    if has_res:
        return outs[0], outs[1:]
    return x, outs


def _pair_masks(mode, lane):
    if mode == "mla":
        in_a = (lane < 64) | ((lane >= 128) & (lane < 144)) | ((lane >= 192) & (lane < 208))
        in_b = ((lane >= 64) & (lane < 128)) | ((lane >= 144) & (lane < 160)) | ((lane >= 208) & (lane < 224))
    else:
        in_a = (lane < 32) | ((lane >= 64) & (lane < 96))
        in_b = ((lane >= 32) & (lane < 64)) | (lane >= 96)
    return in_a, in_b


def _dense_attn_kernel(mode, lambda_init, nlat, *refs):
    n_in = 8 if mode == "diff" else 3
    q_ref, k_ref, vt_ref = refs[:3]
    o_ref = refs[n_in]
    s_bufs = refs[n_in + 1:n_in + 3]
    p_bufs = refs[n_in + 3:n_in + 5]
    acc_buf, m_buf, l_buf, a_buf = refs[n_in + 5:]
    i = pl.program_id(2)
    qt = q_ref[0].T
    width, tq = qt.shape
    row = lax.broadcasted_iota(jnp.int32, (width, 1), 0)
    in_a, in_b = _pair_masks(mode, row)
    zero = jnp.zeros_like(qt)
    q2 = jnp.concatenate([jnp.where(in_a, qt, zero), jnp.where(in_b, qt, zero)], axis=1)
    chunks = [(nlat, 1)] + [(2 * c, 2) for c in range(nlat // 2)]
    n_lat_q = (nlat * TK) // tq

    def scores(chunk, s_buf):
        k0, n = chunk
        s_buf[0:n * TK, :] = jnp.dot(k_ref[0, k0 * TK:(k0 + n) * TK, :], q2, preferred_element_type=F32)

    def pv(chunk, p_buf):
        k0, n = chunk
        out = jnp.dot(vt_ref[0, k0], p_buf[0:TK, :], preferred_element_type=F32)
        for r in range(1, n):
            out = out + jnp.dot(vt_ref[0, k0 + r], p_buf[r * TK:(r + 1) * TK, :], preferred_element_type=F32)
        return out

    def softmax(chunk, s_buf, p_buf, first):
        n = chunk[1]
        s = s_buf[0:n * TK, :]
        mx = jnp.max(s, axis=0, keepdims=True)
        if first:
            mn = mx
        else:
            m = m_buf[...]
            mn = jnp.maximum(m, mx)
            a_buf[...] = jnp.exp2(m - mn)
        p = jnp.exp2(s - mn)
        ps = jnp.sum(p, axis=0, keepdims=True)
        l_buf[...] = ps if first else a_buf[...] * l_buf[...] + ps
        m_buf[...] = mn
        p_buf[0:n * TK, :] = p.astype(BF16)

    def accumulate(chunk, p_buf, first):
        if first:
            acc_buf[...] = pv(chunk, p_buf)
        else:
            acc_buf[...] = a_buf[...] * acc_buf[...] + pv(chunk, p_buf)

    def pipeline(chs):
        scores(chs[0], s_bufs[0])
        for c, ch in enumerate(chs):
            if c >= 2:
                accumulate(chs[c - 1], p_bufs[(c - 1) % 2], first=False)
            if c + 1 < len(chs):
                scores(chs[c + 1], s_bufs[(c + 1) % 2])
            softmax(ch, s_bufs[c % 2], p_bufs[c % 2], first=c == 0)
            if c == 1:
                accumulate(chs[0], p_bufs[0], first=True)
        last = len(chs) - 1
        if last == 0:
            accumulate(chs[0], p_bufs[0], first=True)
        else:
            accumulate(chs[last], p_bufs[last % 2], first=False)

    @pl.when(i < n_lat_q)
    def _():
        pipeline(chunks)

    @pl.when(i >= n_lat_q)
    def _():
        pipeline(chunks[:1])

    o2 = acc_buf[...] * (1.0 / l_buf[...])
    oa, ob = o2[:, :tq], o2[:, tq:]
    if mode == "diff":
        lq1_ref, lk1_ref, lq2_ref, lk2_ref, sub_ref = refs[3:8]
        lam = (jnp.exp(jnp.sum(lq1_ref[...] * lk1_ref[...], axis=1, keepdims=True))
               - jnp.exp(jnp.sum(lq2_ref[...] * lk2_ref[...], axis=1, keepdims=True)) + lambda_init)
        o = (oa - lam * ob).T
        o = _rms(o, sub_ref[...], DIFF_EPS) * (1.0 - lambda_init)
    else:
        vrow = lax.broadcasted_iota(jnp.int32, (LANES, 1), 0)
        o = jnp.where(vrow < 64, oa, ob).T
    o_ref[0] = o.astype(BF16)


def _dense_attn_call(mode, q, k, vt, extra=(), lambda_init=0.0):
    bsz, t, qtot = q.shape
    width = 256 if mode == "mla" else 128
    npairs = qtot // width
    nq = pl.cdiv(t, TQ)
    ins = [q, k, vt]
    specs = [pl.BlockSpec((1, TQ, width), lambda b, j, i: (b, i, j)),
             pl.BlockSpec((1, t, width), lambda b, j, i: (b, 0, j)),
             pl.BlockSpec((1, t // TK, LANES, TK), lambda b, j, i: (b, 0, j, 0))]
    for e in extra:
        ins.append(e)
        specs.append(pl.BlockSpec(e.shape, lambda b, j, i: (0, 0)))
    return pl.pallas_call(
        functools.partial(_dense_attn_kernel, mode, lambda_init, t // TK - 1),
        grid=(bsz, npairs, nq),
        in_specs=specs,
        out_specs=pl.BlockSpec((1, TQ, LANES), lambda b, j, i: (b, i, j)),
        out_shape=jax.ShapeDtypeStruct((bsz, t, npairs * LANES), BF16),
        scratch_shapes=[pltpu.VMEM((2 * TK, 2 * TQ), F32), pltpu.VMEM((2 * TK, 2 * TQ), F32),
                        pltpu.VMEM((2 * TK, 2 * TQ), BF16), pltpu.VMEM((2 * TK, 2 * TQ), BF16),
                        pltpu.VMEM((LANES, 2 * TQ), F32), pltpu.VMEM((1, 2 * TQ), F32),
                        pltpu.VMEM((1, 2 * TQ), F32), pltpu.VMEM((1, 2 * TQ), F32)],
        compiler_params=_cparams(3),
        name=mode + "_attn",
    )(*ins)


def _swa_kernel(nlat, n_kv, sink_ref, q_ref, k_ref, vt_ref, o_ref):
    i = pl.program_id(1)
    s_len = nlat * TM
    half = TM // 2
    row = lax.broadcasted_iota(jnp.int32, (LANES, 1), 0)
    in_a, in_b = _pair_masks("pair", row)
    start0 = pl.multiple_of(jnp.maximum(i * TM - half, 0), half)
    start1 = pl.multiple_of(i * TM, TM)
    start2 = pl.multiple_of(jnp.minimum((i + 1) * TM, s_len + half), half)
    b0 = jnp.maximum(i - 1, 0)
    b2 = jnp.minimum(i + 1, nlat)

    r = lax.broadcasted_iota(jnp.int32, (2 * TM, 1), 0)
    far = -4 * SWA_WINDOW
    lat = i < nlat
    pos0 = jnp.where(lat & (i >= 1), start0 + r, far)
    pos1 = jnp.where(lat, start1 + r - half, far)
    pos2 = jnp.where(lat & (i + 1 < nlat), start2 + r - half - TM, far)
    kpos = jnp.where(r < half, pos0, jnp.where(r < half + TM, pos1, pos2))
    qpos = i * TM + lax.broadcasted_iota(jnp.int32, (1, TM), 1)
    band = jnp.abs(qpos - kpos) <= SWA_WINDOW
    band4 = jnp.concatenate([band] * 4, axis=1)
    vrow = lax.broadcasted_iota(jnp.int32, (LANES, 1), 0)

    for g in range(n_kv):
        cols = []
        for pr in range(2):
            c0 = (2 * g + pr) * LANES
            qt = q_ref[0, :, c0:c0 + LANES].T
            zero = jnp.zeros_like(qt)
            cols += [jnp.where(in_a, qt, zero), jnp.where(in_b, qt, zero)]
        q4 = jnp.concatenate(cols, axis=1)
        kl = slice(g * LANES, (g + 1) * LANES)
        k_cat = jnp.concatenate([k_ref[0, pl.ds(start0, half), kl], k_ref[0, pl.ds(start1, TM), kl],
                                 k_ref[0, pl.ds(start2, half), kl], k_ref[0, s_len:s_len + TM, kl]], axis=0)
        vt_cat = jnp.concatenate([vt_ref[0, b0, kl, half:], vt_ref[0, i, kl, :], vt_ref[0, b2, kl, :half],
                                  vt_ref[0, nlat, kl, :]], axis=1)
        s = jnp.dot(k_cat, q4, preferred_element_type=F32)
        s_loc = jnp.where(band4, s[:2 * TM], NEG)
        s_ctx = s[2 * TM:]
        sink = jnp.concatenate([jnp.full((1, TM), sink_ref[4 * g + h], F32) for h in range(4)], axis=1) * LOG2E
        m = jnp.maximum(jnp.maximum(jnp.max(s_loc, axis=0, keepdims=True), jnp.max(s_ctx, axis=0, keepdims=True)), sink)
        p_loc = jnp.exp2(s_loc - m)
        p_ctx = jnp.exp2(s_ctx - m)
        l = jnp.sum(p_loc, axis=0, keepdims=True) + jnp.sum(p_ctx, axis=0, keepdims=True) + jnp.exp2(sink - m)
        p = jnp.concatenate([p_loc, p_ctx], axis=0).astype(BF16)
        o4 = jnp.dot(vt_cat, p, preferred_element_type=F32) * (1.0 / l)
        for pr in range(2):
            oa = o4[:, (2 * pr) * TM:(2 * pr + 1) * TM]
            ob = o4[:, (2 * pr + 1) * TM:(2 * pr + 2) * TM]
            c0 = (2 * g + pr) * LANES
            o_ref[0, :, c0:c0 + LANES] = jnp.where(vrow < 64, oa, ob).T.astype(BF16)


def _swa_call(sink, q, k, vt):
    bsz, t, qtot = q.shape
    kw = k.shape[-1]
    nt = t // TM
    grid_spec = pltpu.PrefetchScalarGridSpec(
        num_scalar_prefetch=1,
        grid=(bsz, nt),
        in_specs=[pl.BlockSpec((1, TM, qtot), lambda b, i, s: (b, i, 0)),
                  pl.BlockSpec((1, t, kw), lambda b, i, s: (b, 0, 0)),
                  pl.BlockSpec((1, nt, kw, TM), lambda b, i, s: (b, 0, 0, 0))],
        out_specs=pl.BlockSpec((1, TM, qtot), lambda b, i, s: (b, i, 0)),
    )
    return pl.pallas_call(
        functools.partial(_swa_kernel, nt - 1, kw // LANES),
        grid_spec=grid_spec,
        out_shape=jax.ShapeDtypeStruct((bsz, t, qtot), BF16),
        compiler_params=_cparams(2),
        name="swa_attn",
    )(sink, q, k, vt)


def _out_proj_kernel(n_o, split, nlat, *refs):
    o_refs = refs[:n_o]
    w_refs = refs[n_o:2 * n_o]
    refs = refs[2 * n_o:]
    if split:
        x_in = _first_layer_x(refs[0], refs[1], nlat)
        refs = refs[2:]
    else:
        x_in = refs[0][0]
        refs = refs[1:]
    mod_ref, g_ref, wr_ref, br_ref, xo_ref, h_ref, lg_ref = refs
    acc = jnp.dot(o_refs[0][0], w_refs[0][...], preferred_element_type=F32)
    for n in range(1, n_o):
        acc = acc + jnp.dot(o_refs[n][0], w_refs[n][...], preferred_element_type=F32)
    m = mod_ref[0, 0]
    x = x_in + m[2:3, :] * acc
    xo_ref[0] = x
    h = _rms(x, g_ref[...], NORM_EPS) * (1.0 + m[4:5, :]) + m[3:4, :]
    hi = h.astype(BF16)
    lo = (h - hi.astype(F32)).astype(BF16)
    h_ref[0] = hi
    both = jnp.dot(hi, wr_ref[...], preferred_element_type=F32)
    lg_ref[0] = (both[:, :LANES] + both[:, LANES:]
                 + jnp.dot(lo, wr_ref[:, :LANES], preferred_element_type=F32) + br_ref[...])


def _out_proj_call(os_, ws, x, mod, g, wr, br):
    split = isinstance(x, (tuple, list))
    if split:
        bsz, t, d = x[0].shape[0], x[0].shape[1] + x[1].shape[1], x[0].shape[2]
    else:
        bsz, t, d = x.shape
    nt = t // TM
    nlat = nt - 1
    n_o = len(os_)
    x_ins = list(x) if split else [x]
    x_specs = _split_specs(nlat, d) if split else [_row_spec(d)]
    ins = list(os_) + list(ws) + x_ins + [mod, g.reshape(1, d), wr, br]
    specs = ([_row_spec(o.shape[-1]) for o in os_] + [_full_spec(w.shape) for w in ws]
             + x_specs + [_mod_spec(nlat), _full_spec((1, d)), _full_spec(wr.shape), _full_spec(br.shape)])
    return pl.pallas_call(
        functools.partial(_out_proj_kernel, n_o, split, nlat),
        grid=(bsz, nt),
        in_specs=specs,
        out_specs=[_row_spec(d), _row_spec(d), _row_spec(LANES)],
        out_shape=[jax.ShapeDtypeStruct((bsz, t, d), F32),
                   jax.ShapeDtypeStruct((bsz, t, d), BF16),
                   jax.ShapeDtypeStruct((bsz, t, LANES), F32)],
        compiler_params=_cparams(2),
        name="out_proj",
    )(*ins)


def _moe_kernel(first, be_ref, nu_ref, x_ref, w1_ref, w3_ref, w2_ref, *rest):
    o_ref, w1c, w3c, w2c = rest[-4:]
    i = pl.program_id(0)
    gi = first + i
    e = be_ref[gi]

    @pl.when((i == 0) | (e != be_ref[jnp.maximum(gi - 1, 0)]))
    def _():
        w1c[...] = w1_ref[0, 0].astype(BF16)
        w3c[...] = w3_ref[0, 0].astype(BF16)
        w2c[...] = w2_ref[0, 0].astype(BF16)

    @pl.when(gi < nu_ref[0])
    def _():
        x = x_ref[...]
        a = jnp.dot(x, w1c[...], preferred_element_type=F32)
        b = jnp.dot(x, w3c[...], preferred_element_type=F32)
        hmid = (a * jax.nn.sigmoid(a)) * b
        o_ref[...] = jnp.dot(hmid.astype(BF16), w2c[...], preferred_element_type=F32)

    @pl.when(gi >= nu_ref[0])
    def _():
        o_ref[...] = jnp.zeros_like(o_ref)


def _moe_call(layer, block_e, n_used, x_part, first, n_blocks, w1, w3, w2, y_prev=None):
    d = x_part.shape[1]
    steps = x_part.shape[0] // TMOE
    ff = w1.shape[-1]
    in_specs = [pl.BlockSpec((TMOE, d), lambda i, be, nu: (i, 0)),
                pl.BlockSpec((1, 1, d, ff), lambda i, be, nu: (layer, be[first + i], 0, 0)),
                pl.BlockSpec((1, 1, d, ff), lambda i, be, nu: (layer, be[first + i], 0, 0)),
                pl.BlockSpec((1, 1, ff, d), lambda i, be, nu: (layer, be[first + i], 0, 0))]
    ins = [block_e, n_used, x_part, w1, w3, w2]
    aliases = {}
    if y_prev is not None:
        in_specs.append(pl.BlockSpec(memory_space=pl.ANY))
        aliases = {len(ins): 0}
        ins.append(y_prev)
    grid_spec = pltpu.PrefetchScalarGridSpec(
        num_scalar_prefetch=2,
        grid=(steps,),
        in_specs=in_specs,
        out_specs=pl.BlockSpec((TMOE, d), lambda i, be, nu: (first + i, 0)),
        scratch_shapes=[pltpu.VMEM((d, ff), BF16), pltpu.VMEM((d, ff), BF16), pltpu.VMEM((ff, d), BF16)],
    )
    return pl.pallas_call(
        functools.partial(_moe_kernel, first),
        grid_spec=grid_spec,
        out_shape=jax.ShapeDtypeStruct((n_blocks * TMOE, d), F32),
        input_output_aliases=aliases,
        compiler_params=_cparams(1),
        name="moe_experts",
    )(*ins)


def _router_kernel(lg_ref, tri_ref, rt_ref, cnt_ref, ga_ref, gb_ref, run):
    i = pl.program_id(0)

    @pl.when(i == 0)
    def _():
        run[...] = jnp.zeros_like(run)

    lg = lg_ref[...]
    lane = lax.broadcasted_iota(jnp.int32, lg.shape, 1)
    gmask = lane < N_GROUPS
    gl = jnp.where(gmask, lg, NEG)
    gmax = jnp.max(gl, axis=1, keepdims=True)
    grp = jnp.min(jnp.where(gl == gmax, lane, LANES), axis=1, keepdims=True)
    p_grp = 1.0 / jnp.sum(jnp.where(gmask, jnp.exp(lg - gmax), 0.0), axis=1, keepdims=True)
    first = N_GROUPS + EXPERTS_PER_GROUP * grp
    emask = (lane >= first) & (lane < first + EXPERTS_PER_GROUP)
    el = jnp.where(emask, lg, NEG)
    e1 = jnp.max(el, axis=1, keepdims=True)
    i1 = jnp.min(jnp.where(el == e1, lane, LANES), axis=1, keepdims=True)
    el2 = jnp.where(lane == i1, NEG, el)
    e2 = jnp.max(el2, axis=1, keepdims=True)
    i2 = jnp.min(jnp.where(el2 == e2, lane, LANES), axis=1, keepdims=True)
    tt = jnp.exp(e2 - e1)
    g1 = p_grp / (1.0 + tt)
    g2 = g1 * tt
    oh1 = lane == i1
    oh2 = lane == i2
    onehot = jnp.where(oh1 | oh2, 1.0, 0.0)
    rank_all = jnp.dot(tri_ref[...], onehot.astype(BF16), preferred_element_type=F32) + run[...]
    r1 = jnp.sum(jnp.where(oh1, rank_all, 0.0), axis=1, keepdims=True)
    r2 = jnp.sum(jnp.where(oh2, rank_all, 0.0), axis=1, keepdims=True)
    run[...] = run[...] + jnp.sum(onehot, axis=0, keepdims=True)
    cnt_ref[...] = run[...]
    vals = [(i1 - N_GROUPS).astype(F32), (i2 - N_GROUPS).astype(F32), r1, r2, g1, g2]
    packed = jnp.zeros(lg.shape, F32)
    for n, v in enumerate(vals):
        packed = jnp.where(lane == n, v, packed)
    rt_ref[...] = packed.T[0:8, :]
    ga_ref[...] = jnp.broadcast_to(g1, lg.shape)
    gb_ref[...] = jnp.broadcast_to(g2, lg.shape)


def _router_call(logits):
    n_tok = logits.shape[0]
    tri = (jnp.arange(TM)[:, None] > jnp.arange(TM)[None, :]).astype(BF16)
    return pl.pallas_call(
        _router_kernel,
        grid=(n_tok // TM,),
        in_specs=[pl.BlockSpec((TM, LANES), lambda i: (i, 0)),
                  pl.BlockSpec((TM, TM), lambda i: (0, 0))],
        out_specs=[pl.BlockSpec((8, TM), lambda i: (0, i)),
                   pl.BlockSpec((1, LANES), lambda i: (0, 0)),
                   pl.BlockSpec((TM, LANES), lambda i: (i, 0)),
                   pl.BlockSpec((TM, LANES), lambda i: (i, 0))],
        out_shape=[jax.ShapeDtypeStruct((8, n_tok), F32), jax.ShapeDtypeStruct((1, LANES), F32),
                   jax.ShapeDtypeStruct((n_tok, LANES), F32), jax.ShapeDtypeStruct((n_tok, LANES), F32)],
        scratch_shapes=[pltpu.VMEM((1, LANES), F32)],
        compiler_params=_cparams(1),
        name="router",
    )(logits, tri)


def _dispatch_plan(rt, cnt):
    n_tok = rt.shape[1]
    eid = rt[0:2].astype(jnp.int32)
    rank = rt[2:4].astype(jnp.int32)
    counts = cnt[0, N_GROUPS:N_GROUPS + N_EXPERTS].astype(jnp.int32)
    padded = (counts + TMOE - 1) // TMOE * TMOE
    pad_end = jnp.cumsum(padded)
    pad_start = pad_end - padded
    experts = jnp.arange(N_EXPERTS, dtype=jnp.int32)[:, None, None]
    dest = jnp.sum(jnp.where(eid[None] == experts, pad_start[:, None, None], 0), axis=0) + rank
    n_assign = n_tok * TOP_K
    n_blocks = (n_assign + N_EXPERTS * (TMOE - 1) + TMOE - 1) // TMOE
    n_blocks = (n_blocks + MOE_PARTS - 1) // MOE_PARTS * MOE_PARTS
    n_slots = n_blocks * TMOE
    tok = jnp.arange(n_tok, dtype=jnp.int32)
    slot_tok = jnp.zeros((n_slots,), jnp.int32).at[dest.reshape(n_assign)].set(
        jnp.concatenate([tok, tok]), unique_indices=True)
    block_start = jnp.arange(n_blocks, dtype=jnp.int32) * TMOE
    block_e = jnp.minimum(jnp.sum((block_start[:, None] >= pad_end[None, :]).astype(jnp.int32), axis=1),
                          N_EXPERTS - 1).astype(jnp.int32)
    n_used = (pad_end[-1:] // TMOE).astype(jnp.int32)
    return slot_tok, block_e, n_used, dest


def _final_kernel(x_ref, ya_ref, yb_ref, ga_ref, gb_ref, pmod_ref, g_ref, o_ref):
    x = _moe_residual(x_ref, ya_ref, yb_ref, ga_ref, gb_ref, pmod_ref)
    o_ref[0] = _rms(x, g_ref[...], NORM_EPS)


def _final_call(x, res, g, s_len):
    bsz, t, d = x.shape
    ya, yb, ga, gb, pmod = res
    return pl.pallas_call(
        _final_kernel,
        grid=(bsz, s_len // TM),
        in_specs=[_row_spec(d), _row_spec(d), _row_spec(d), _row_spec(LANES), _row_spec(LANES),
                  pl.BlockSpec((1, 1, 8, d), lambda b, i: (b, 0, 0, 0)),
                  _full_spec((1, d))],
        out_specs=_row_spec(d),
        out_shape=jax.ShapeDtypeStruct((bsz, s_len, d), F32),
        compiler_params=_cparams(2),
        name="final_norm",
    )(x, ya, yb, ga, gb, pmod, g.reshape(1, d))


def _take_cols(w, idx):
    wz = jnp.concatenate([w, jnp.zeros((w.shape[0], 1), w.dtype)], axis=1)
    return jnp.take(wz, jnp.asarray(idx, dtype=jnp.int32), axis=1).astype(BF16)


def _ab_layouts():
    zc = 1184
    cols = []
    for j in range(4):
        cols.append(j * 128 + _PAIR_PERM)
    for g in range(2):
        base = 512 + g * 64
        cols.append(base + np.concatenate([np.arange(0, 32), np.arange(0, 32), np.arange(32, 64), np.arange(32, 64)]))
    for g in range(2):
        base = 640 + g * 64
        cols.append(base + np.concatenate([np.arange(64), np.arange(64)]))
    cols.append(768 + np.arange(256))
    cols.append(1024 + np.arange(128))
    kr = 1152
    z32 = np.full((32,), zc)
    cols.append(np.concatenate([kr + np.arange(16), kr + np.arange(16), z32,
                                kr + 16 + np.arange(16), kr + 16 + np.arange(16), z32]))
    w1_idx = np.concatenate(cols)

    zq = 768
    uq = []
    z32q = np.full((32,), zq)
    for j in range(4):
        a, b = 2 * j * 96, (2 * j + 1) * 96
        uq.append(np.concatenate([a + np.arange(64), b + np.arange(64),
                                  a + 64 + np.arange(16), b + 64 + np.arange(16), z32q,
                                  a + 80 + np.arange(16), b + 80 + np.arange(16), z32q]))
    uq_idx = np.concatenate(uq)

    kn, mv = [], []
    for h in range(8):
        kn.append(h * 128 + np.arange(64))
        mv.append(h * 128 + 64 + np.arange(64))
    ukv_idx = np.concatenate(kn + mv)
    return w1_idx, uq_idx, ukv_idx


def _diff_layout():
    cols = []
    for part in range(2):
        for h in range(8):
            cols.append(part * 1024 + h * 128 + _PAIR_PERM)
    cols.append(2048 + np.arange(1024))
    return np.concatenate(cols)


def kernel(x, c, ctx, c_ctx, norm_mix, norm_ffn, ada_w, ada_b, ab_w_in, mla_q_norm, mla_w_uq, mla_kv_norm, mla_w_ukv, swa_sink, ab_w_out, diff_w_in, diff_lambda_q1, diff_lambda_k1, diff_lambda_q2, diff_lambda_k2, diff_subln, diff_w_out, router_group_w, router_group_b, router_expert_w, router_expert_b, expert_w1, expert_w3, expert_w2, final_norm):
    bsz, s_len, d = x.shape
    c_len = ctx.shape[1]
    depth = ada_w.shape[0]
    assert d == D_MODEL and c_len == TM and s_len % TQ == 0 and s_len % (2 * TK) == 0
    t = s_len + c_len

    xs = (x, ctx)
    tables = _rope_tables(s_len, c_len)
    c64, s64, c32, s32 = tables

    n_rows = (bsz + 1 + 7) // 8 * 8
    rows = jnp.concatenate([c, c_ctx[None, :], jnp.zeros((n_rows - bsz - 1, d), F32)], axis=0)
    mod_all = _ada_call(rows, ada_w, ada_b)
    mod_lat = mod_all[:, :bsz].reshape(depth, bsz, 1, 6, d)
    mod_ctx = jnp.broadcast_to(mod_all[:, bsz].reshape(depth, 1, 1, 6, d), (depth, bsz, 1, 6, d))
    mods = jnp.concatenate([mod_lat, mod_ctx], axis=2)
    mods = jnp.concatenate([mods, jnp.zeros((depth, bsz, 2, 2, d), F32)], axis=3)

    w1_idx, uq_idx, ukv_idx = _ab_layouts()
    diff_idx = _diff_layout()

    res = None
    for l in range(depth):
        j = l // 2
        mod = mods[l]
        if l % 2 == 0:
            weights = [_take_cols(ab_w_in[j], w1_idx), mla_q_norm[j].reshape(1, -1), mla_kv_norm[j].reshape(1, -1),
                       _take_cols(mla_w_uq[j], uq_idx), _take_cols(mla_w_ukv[j], ukv_idx)]
            xs, (sq, sk, sv, mq, mk, mv) = _proj_call("ab", xs, res, mod, norm_mix[l], weights,
                                                      [c64, s64, c32, s32], [512, 256, -256, 1024, 1024, -512])
            o_a = _swa_call(swa_sink[j], sq, sk, sv)
            o_b = _dense_attn_call("mla", mq, mk, mv)
            w_out = ab_w_out[j].astype(BF16)
            attn_outs, out_ws = [o_a, o_b], [w_out[:512], w_out[512:]]
        else:
            lambda_init = 0.8 - 0.6 * math.exp(-0.3 * l)
            weights = [_take_cols(diff_w_in[j], diff_idx)]
            xs, (dq, dk, dv) = _proj_call("diff", xs, res, mod, norm_mix[l], weights, [c64, s64], [1024, 1024, -1024])
            extra = [diff_lambda_q1[j].reshape(1, -1), diff_lambda_k1[j].reshape(1, -1),
                     diff_lambda_q2[j].reshape(1, -1), diff_lambda_k2[j].reshape(1, -1),
                     diff_subln[j].reshape(1, -1)]
            o_d = _dense_attn_call("diff", dq, dk, dv, extra=extra, lambda_init=lambda_init)
            attn_outs, out_ws = [o_d], [diff_w_out[j].astype(BF16)]

        wr = jnp.concatenate([router_group_w[l], router_expert_w[l],
                              jnp.zeros((d, LANES - N_GROUPS - N_EXPERTS), F32)], axis=1)
        wr_hi = wr.astype(BF16)
        wr = jnp.concatenate([wr_hi, (wr - wr_hi.astype(F32)).astype(BF16)], axis=1)
        br = jnp.concatenate([router_group_b[l], router_expert_b[l],
                              jnp.zeros((LANES - N_GROUPS - N_EXPERTS,), F32)]).reshape(1, LANES)
        xs, h2, logits = _out_proj_call(attn_outs, out_ws, xs, mod, norm_ffn[l], wr, br)

        n_tok = bsz * t
        rt, cnt, ga, gb = _router_call(logits.reshape(n_tok, LANES))
        slot_tok, block_e, n_used, dest = _dispatch_plan(rt, cnt)
        h2f = h2.reshape(n_tok, d)
        n_blocks = slot_tok.shape[0] // TMOE
        part = n_blocks // MOE_PARTS
        yb = None
        for p in range(MOE_PARTS):
            x_part = jnp.take(h2f, slot_tok[p * part * TMOE:(p + 1) * part * TMOE], axis=0, mode="clip")
            yb = _moe_call(l, block_e, n_used, x_part, p * part, n_blocks, expert_w1, expert_w3, expert_w2, y_prev=yb)
        res = (jnp.take(yb, dest[0], axis=0, mode="clip").reshape(bsz, t, d),
               jnp.take(yb, dest[1], axis=0, mode="clip").reshape(bsz, t, d),
               ga.reshape(bsz, t, LANES), gb.reshape(bsz, t, LANES), mod)

    return _final_call(xs, res, final_norm, s_len)
```
